```python
import jax, jax.numpy as jnp
from jax import lax
import numpy as np

D_MODEL = 1024
BATCH = 32
SEQ = 256
DEPTH = 1
DEC_BATCH = 4
DEC_SEQ = 4096
PAST_LEN = 256

GRID_W = 64
FFN_DIM = 2816
GMLP_GROUPS = 8
GMLP_GROUP_DIM = 128
GMLP_DIM = GMLP_GROUPS * GMLP_GROUP_DIM
CHUNK = 128
MLA_HEADS = 8
QK_NOPE_DIM = 128
QK_ROPE_DIM = 64
V_HEAD_DIM = 128
Q_LORA_RANK = 256
KV_LORA_RANK = 256
MLA_DIM = MLA_HEADS * V_HEAD_DIM
ROPE_BASE = 10000.0
Q_BLOCK = 128
N_MOD = 9
EPS = 1e-6
IN_DIM = 2 * GMLP_DIM + Q_LORA_RANK + KV_LORA_RANK + QK_ROPE_DIM + 2 * D_MODEL

kernel_name = "hybrid_gmlp_mla_macaron_diffusion_step"


def rms_norm(x, g):
    xf = x.astype(jnp.float32)
    y = xf * lax.rsqrt(jnp.mean(xf * xf, axis=-1, keepdims=True) + EPS)
    return (y * g.astype(jnp.float32)).astype(x.dtype)


def modulate(x, shift, scale):
    return x * (1 + scale[:, None, :]) + shift[:, None, :]


def adaln(cond, mod_w, mod_b):
    m = jax.nn.silu(cond) @ mod_w + mod_b
    return m.reshape(cond.shape[0], N_MOD, D_MODEL)


def swiglu(h, w_in, w_out):
    g, u = jnp.split(h @ w_in, 2, axis=-1)
    return (jax.nn.silu(g) * u) @ w_out


def axial_angles(L):
    rows = L // GRID_W
    r = jnp.repeat(jnp.arange(rows, dtype=jnp.float32), GRID_W)
    col = jnp.tile(jnp.arange(GRID_W, dtype=jnp.float32), rows)
    half = QK_ROPE_DIM // 2
    inv = 1.0 / (ROPE_BASE ** (jnp.arange(0, half, 2, dtype=jnp.float32) / half))
    return r[:, None] * inv, col[:, None] * inv


def rope_half(x, ang):
    x1, x2 = jnp.split(x, 2, axis=-1)
    cos, sin = jnp.cos(ang), jnp.sin(ang)
    return jnp.concatenate([x1 * cos - x2 * sin, x2 * cos + x1 * sin], axis=-1)


def axial_rope(x, ang_r, ang_c):
    xf = x.astype(jnp.float32)
    xr, xc = jnp.split(xf, 2, axis=-1)
    return jnp.concatenate([rope_half(xr, ang_r), rope_half(xc, ang_c)], axis=-1).astype(x.dtype)


def chunk_gmlp(u, v, v_norm, w_s, b_s):
    B, L, _ = u.shape
    nc = L // CHUNK
    vv = rms_norm(v, v_norm).reshape(B, nc, CHUNK, GMLP_GROUPS, GMLP_GROUP_DIM)
    mixed = jnp.einsum('gpq,bcqgd->bcpgd', w_s, vv) + b_s[:, :, None]
    return u * mixed.reshape(B, L, GMLP_DIM)


def q_up(q_lat, q_norm, w_q_up):
    B, L, _ = q_lat.shape
    q = (rms_norm(q_lat, q_norm) @ w_q_up).reshape(B, L, MLA_HEADS, QK_NOPE_DIM + QK_ROPE_DIM)
    return q[..., :QK_NOPE_DIM], q[..., QK_NOPE_DIM:]


def kv_up(ckv, w_kv_up):
    B, L, _ = ckv.shape
    kv = (ckv @ w_kv_up).reshape(B, L, MLA_HEADS, QK_NOPE_DIM + V_HEAD_DIM)
    return kv[..., :QK_NOPE_DIM], kv[..., QK_NOPE_DIM:]


def attend_block(qn, qr, kn, kr, v):
    s = jnp.einsum('bqhd,bkhd->bhqk', qn, kn) + jnp.einsum('bqhr,bkr->bhqk', qr, kr)
    s = s.astype(jnp.float32) * ((QK_NOPE_DIM + QK_ROPE_DIM) ** -0.5)
    p = jax.nn.softmax(s, axis=-1).astype(v.dtype)
    return jnp.einsum('bhqk,bkhd->bqhd', p, v)


def mla_attention(q_nope, q_rope, k_nope, k_rope, v):
    B, L = q_nope.shape[:2]
    nb = L // Q_BLOCK

    def blocks(t):
        return jnp.moveaxis(t.reshape(B, nb, Q_BLOCK, *t.shape[2:]), 1, 0)

    out = lax.map(lambda qs: attend_block(qs[0], qs[1], k_nope, k_rope, v),
                  (blocks(q_nope), blocks(q_rope)))
    return jnp.moveaxis(out, 0, 1).reshape(B, L, MLA_DIM)


def trunk_layer(x, mods, lw, ctx_ckv, ctx_krope):
    sh1, sc1, g1, sh2, sc2, g2, sh3, sc3, g3 = [mods[:, i] for i in range(N_MOD)]
    B, L, _ = x.shape
    h = modulate(rms_norm(x, lw['norm_ffn1']), sh1, sc1)
    x = x + 0.5 * g1[:, None, :] * swiglu(h, lw['ffn1_w_in'], lw['ffn1_w_out'])
    h = modulate(rms_norm(x, lw['norm_mix']), sh2, sc2)
    proj = h @ lw['w_in']
    offs = np.cumsum([GMLP_DIM, GMLP_DIM, Q_LORA_RANK, KV_LORA_RANK, QK_ROPE_DIM]).tolist()
    u, v, q_lat, ckv_raw, krope, gate_logits = jnp.split(proj, offs, axis=-1)
    out_a = chunk_gmlp(u, v, lw['gmlp_v_norm'], lw['gmlp_w_s'], lw['gmlp_b_s'])
    ckv = rms_norm(ckv_raw, lw['kv_norm'])
    q_nope, q_rope = q_up(q_lat, lw['q_norm'], lw['w_q_up'])
    k_nope, vals = kv_up(ckv, lw['w_kv_up'])
    k_rope = krope
    if ctx_ckv is not None:
        ang_r, ang_c = axial_angles(L)
        q_rope = axial_rope(q_rope, ang_r[:, None, :], ang_c[:, None, :])
        k_rope = axial_rope(krope, ang_r, ang_c)
        ck_nope, c_vals = kv_up(ctx_ckv, lw['w_kv_up'])
        k_nope = jnp.concatenate([ck_nope, k_nope], axis=1)
        vals = jnp.concatenate([c_vals, vals], axis=1)
        k_rope = jnp.concatenate([ctx_krope, k_rope], axis=1)
    out_b = mla_attention(q_nope, q_rope, k_nope, k_rope, vals)
    ga, gb = jnp.split(jax.nn.sigmoid(gate_logits), 2, axis=-1)
    merged = (ga * (out_a @ lw['w_a_proj']) + gb * (out_b @ lw['w_b_proj'])) @ lw['w_o']
    x = x + g2[:, None, :] * merged
    h = modulate(rms_norm(x, lw['norm_ffn2']), sh3, sc3)
    x = x + 0.5 * g3[:, None, :] * swiglu(h, lw['ffn2_w_in'], lw['ffn2_w_out'])
    return x, ckv, krope


def setup_inputs(seed: int = 0) -> dict:
    key = jax.random.key(seed)
    ks = iter(jax.random.split(key, 40))

    def nrm(shape, scale=1.0):
        return jax.random.normal(next(ks), shape, jnp.float32) * scale

    def gain(n):
        return 1.0 + nrm((DEPTH, n), 0.01)

    return {
        'x_prompt': nrm((BATCH, SEQ, D_MODEL)),
        'x_sample': nrm((DEC_BATCH, DEC_SEQ, D_MODEL)),
        'c': nrm((DEC_BATCH, D_MODEL)),
        'cache_ckv': nrm((DEC_BATCH, DEPTH, PAST_LEN, KV_LORA_RANK)),
        'cache_krope': nrm((DEC_BATCH, DEPTH, PAST_LEN, QK_ROPE_DIM)),
        'c_ctx': nrm((D_MODEL,)),
        'mod_w': nrm((DEPTH, D_MODEL, N_MOD * D_MODEL), 0.5 * D_MODEL ** -0.5),
        'mod_b': nrm((DEPTH, N_MOD * D_MODEL), 0.01),
        'norm_ffn1': gain(D_MODEL),
        'ffn1_w_in': nrm((DEPTH, D_MODEL, 2 * FFN_DIM), D_MODEL ** -0.5),
        'ffn1_w_out': nrm((DEPTH, FFN_DIM, D_MODEL), FFN_DIM ** -0.5),
        'norm_mix': gain(D_MODEL),
        'w_in': nrm((DEPTH, D_MODEL, IN_DIM), D_MODEL ** -0.5),
        'gmlp_v_norm': gain(GMLP_DIM),
        'gmlp_w_s': nrm((DEPTH, GMLP_GROUPS, CHUNK, CHUNK), CHUNK ** -0.5),
        'gmlp_b_s': 1.0 + nrm((DEPTH, CHUNK, GMLP_GROUPS), 0.01),
        'q_norm': gain(Q_LORA_RANK),
        'w_q_up': nrm((DEPTH, Q_LORA_RANK, MLA_HEADS * (QK_NOPE_DIM + QK_ROPE_DIM)), Q_LORA_RANK ** -0.5),
        'kv_norm': gain(KV_LORA_RANK),
        'w_kv_up': nrm((DEPTH, KV_LORA_RANK, MLA_HEADS * (QK_NOPE_DIM + V_HEAD_DIM)), KV_LORA_RANK ** -0.5),
        'w_a_proj': nrm((DEPTH, GMLP_DIM, D_MODEL), GMLP_DIM ** -0.5),
        'w_b_proj': nrm((DEPTH, MLA_DIM, D_MODEL), MLA_DIM ** -0.5),
        'w_o': nrm((DEPTH, D_MODEL, D_MODEL), D_MODEL ** -0.5),
        'norm_ffn2': gain(D_MODEL),
        'ffn2_w_in': nrm((DEPTH, D_MODEL, 2 * FFN_DIM), D_MODEL ** -0.5),
        'ffn2_w_out': nrm((DEPTH, FFN_DIM, D_MODEL), FFN_DIM ** -0.5),
        'norm_final': 1.0 + nrm((D_MODEL,), 0.01),
    }


def reference(x_prompt, x_sample, c, cache_ckv, cache_krope, c_ctx, mod_w, mod_b,
              norm_ffn1, ffn1_w_in, ffn1_w_out, norm_mix, w_in, gmlp_v_norm, gmlp_w_s,
              gmlp_b_s, q_norm, w_q_up, kv_norm, w_kv_up, w_a_proj, w_b_proj, w_o,
              norm_ffn2, ffn2_w_in, ffn2_w_out, norm_final):
    xp, xs = x_prompt, x_sample
    ckv_list, krope_list = [], []
    for l in range(DEPTH):
        lw = {
            'norm_ffn1': norm_ffn1[l], 'ffn1_w_in': ffn1_w_in[l], 'ffn1_w_out': ffn1_w_out[l],
            'norm_mix': norm_mix[l], 'w_in': w_in[l], 'gmlp_v_norm': gmlp_v_norm[l],
            'gmlp_w_s': gmlp_w_s[l], 'gmlp_b_s': gmlp_b_s[l], 'q_norm': q_norm[l],
            'w_q_up': w_q_up[l], 'kv_norm': kv_norm[l], 'w_kv_up': w_kv_up[l],
            'w_a_proj': w_a_proj[l], 'w_b_proj': w_b_proj[l], 'w_o': w_o[l],
            'norm_ffn2': norm_ffn2[l], 'ffn2_w_in': ffn2_w_in[l], 'ffn2_w_out': ffn2_w_out[l],
        }
        mods_ctx = adaln(c_ctx[None, :], mod_w[l], mod_b[l])
        xp, ckv_l, krope_l = trunk_layer(xp, mods_ctx, lw, None, None)
        ckv_list.append(ckv_l)
        krope_list.append(krope_l)
        mods_lat = adaln(c, mod_w[l], mod_b[l])
        xs, _, _ = trunk_layer(xs, mods_lat, lw, cache_ckv[:, l], cache_krope[:, l])
    y_prompt = rms_norm(xp, norm_final)
    y_sample = rms_norm(xs, norm_final)
    new_ckv = jnp.stack(ckv_list, axis=1)
    new_krope = jnp.stack(krope_list, axis=1)
    return (y_prompt, y_sample, new_ckv, new_krope)
```

```python
import functools
import math

import numpy as np
import jax
import jax.numpy as jnp
from jax import lax
from jax.experimental import pallas as pl
from jax.experimental.pallas import tpu as pltpu

D_MODEL = 1024
GRID_W = 64
FFN_DIM = 2816
GMLP_GROUPS = 8
GMLP_GROUP_DIM = 128
GMLP_DIM = GMLP_GROUPS * GMLP_GROUP_DIM
CHUNK = 128
MLA_HEADS = 8
QK_NOPE_DIM = 128
QK_ROPE_DIM = 64
V_HEAD_DIM = 128
Q_LORA_RANK = 256
KV_LORA_RANK = 256
ROPE_BASE = 10000.0
N_MOD = 9
EPS = 1e-6

QK_PAD = 256
ROPE_PAD = QK_PAD - QK_NOPE_DIM
LAT_DIM = Q_LORA_RANK + KV_LORA_RANK + 2 * ROPE_PAD
V7X_VMEM_LIMIT = 56 * 1024 * 1024

F32 = jnp.float32
BF16 = jnp.bfloat16


def _dot(a, b):
    return jnp.dot(a, b, preferred_element_type=F32)


def _rms(x, g):
    return x * lax.rsqrt(jnp.mean(x * x, axis=-1, keepdims=True) + EPS) * g


def _normed_input(x, mods_ref, gain_ref, base):
    shift = mods_ref[0, base:base + 1, :]
    scale = mods_ref[0, base + 1:base + 2, :]
    return _rms(x, gain_ref[...]) * (1.0 + scale) + shift


def _resident(shape):
    nd = len(shape)
    return pl.BlockSpec(shape, lambda *_: (0,) * nd, pipeline_mode=pl.Buffered(1))


def _params(n_grid):
    return pltpu.CompilerParams(dimension_semantics=("parallel",) * n_grid,
                                vmem_limit_bytes=V7X_VMEM_LIMIT)


def _mods_kernel(c_ref, w_ref, b_ref, o_ref):
    c = c_ref[...]
    s = (c * jax.nn.sigmoid(c)).astype(BF16)
    o_ref[...] = _dot(s, w_ref[...].astype(BF16)) + b_ref[...]


def _mods(cond, mod_w, mod_b):
    n = mod_w.shape[1]
    tn = D_MODEL
    return pl.pallas_call(
        _mods_kernel,
        grid=(n // tn,),
        in_specs=[pl.BlockSpec((8, D_MODEL), lambda j: (0, 0)),
                  pl.BlockSpec((D_MODEL, tn), lambda j: (0, j)),
                  pl.BlockSpec((1, tn), lambda j: (0, j))],
        out_specs=pl.BlockSpec((8, tn), lambda j: (0, j)),
        out_shape=jax.ShapeDtypeStruct((8, n), F32),
        compiler_params=_params(1),
        name="adaln_mods",
    )(cond, mod_w, mod_b.reshape(1, n))


def _ffn_kernel(x_ref, mods_ref, gain_ref, wg_ref, wu_ref, wo_ref, fin_ref, o_ref, *,
                mod_base, final_norm):
    x = x_ref[0]
    h = _normed_input(x, mods_ref, gain_ref, mod_base).astype(BF16)
    g = _dot(h, wg_ref[...])
    u = _dot(h, wu_ref[...])
    a = (g * jax.nn.sigmoid(g) * u).astype(BF16)
    y = _dot(a, wo_ref[...])
    gate = mods_ref[0, mod_base + 2:mod_base + 3, :]
    out = x + 0.5 * gate * y
    if final_norm:
        out = _rms(out, fin_ref[...])
    o_ref[0] = out


def _ffn(x, mods, gain, wg, wu, wo, fin, *, mod_base, final_norm, tm):
    nb, l, _ = x.shape
    tok = pl.BlockSpec((1, tm, D_MODEL), lambda b, i: (b, i, 0))
    return pl.pallas_call(
        functools.partial(_ffn_kernel, mod_base=mod_base, final_norm=final_norm),
        grid=(nb, l // tm),
        in_specs=[tok,
                  pl.BlockSpec((1, N_MOD, D_MODEL), lambda b, i: (b, 0, 0)),
                  _resident((1, D_MODEL)),
                  _resident(wg.shape), _resident(wu.shape), _resident(wo.shape),
                  _resident((1, D_MODEL))],
        out_specs=tok,
        out_shape=jax.ShapeDtypeStruct(x.shape, F32),
        compiler_params=_params(2),
        name="ffn_final" if final_norm else "ffn",
    )(x, mods, gain, wg, wu, wo, fin)


def _mixer_in_kernel(x_ref, mods_ref, gain_ref, wu_ref, wv_ref, wlat_ref, wga_ref,
                     vnorm_ref, ws_ref, bs_ref, qnorm_ref, wq_ref, wqsw_ref, kvnorm_ref,
                     wk_ref, wvup_ref, wa_ref, cos_ref, sin_ref,
                     a_ref, qt_ref, k_ref, vt_ref, ckv_ref, krope_ref, mixed_ref, *,
                     rope, q_scale):
    tm = x_ref.shape[1]
    x = x_ref[0]
    h = _normed_input(x, mods_ref, gain_ref, 3).astype(BF16)
    u = _dot(h, wu_ref[...])
    v = _dot(h, wv_ref[...])
    lat = _dot(h, wlat_ref[...])
    ga = jax.nn.sigmoid(_dot(h, wga_ref[...]))

    vv = _rms(v, vnorm_ref[...]).astype(BF16)
    n_chunks = tm // CHUNK
    for g in range(GMLP_GROUPS):
        cols = slice(g * GMLP_GROUP_DIM, (g + 1) * GMLP_GROUP_DIM)
        blk = jnp.concatenate(
            [vv[c * CHUNK:(c + 1) * CHUNK, cols] for c in range(n_chunks)], axis=1)
        mix = _dot(ws_ref[g], blk)
        for c in range(n_chunks):
            mixed_ref[c * CHUNK:(c + 1) * CHUNK, cols] = (
                mix[:, c * CHUNK:(c + 1) * CHUNK] + bs_ref[:, cols])
    out_a = (u * mixed_ref[...]).astype(BF16)
    a_ref[0] = ga * _dot(out_a, wa_ref[...])

    q_lat = lat[:, 0:Q_LORA_RANK]
    ckv = _rms(lat[:, Q_LORA_RANK:Q_LORA_RANK + KV_LORA_RANK], kvnorm_ref[...])
    kr_off = Q_LORA_RANK + KV_LORA_RANK
    kr = lat[:, kr_off:kr_off + ROPE_PAD]
    ckv_ref[0] = ckv
    krope_ref[0] = kr[:, 0:QK_ROPE_DIM]

    qn = _rms(q_lat, qnorm_ref[...]).astype(BF16)
    q_all = _dot(qn, wq_ref[...])
    if rope:
        cos = cos_ref[...]
        sin = sin_ref[...]
        q_sw = _dot(qn, wqsw_ref[...])
        kr = kr * cos + lat[:, kr_off + ROPE_PAD:kr_off + 2 * ROPE_PAD] * sin
    for hd in range(MLA_HEADS):
        nope = q_all[:, hd * QK_PAD:hd * QK_PAD + QK_NOPE_DIM]
        rot = q_all[:, hd * QK_PAD + QK_NOPE_DIM:(hd + 1) * QK_PAD]
        if rope:
            rot = rot * cos + q_sw[:, hd * ROPE_PAD:(hd + 1) * ROPE_PAD] * sin
        qt_ref[0, hd * QK_PAD:hd * QK_PAD + QK_NOPE_DIM, :] = (nope * q_scale).T.astype(BF16)
        qt_ref[0, hd * QK_PAD + QK_NOPE_DIM:(hd + 1) * QK_PAD, :] = (rot * q_scale).T.astype(BF16)

    ckv_b = ckv.astype(BF16)
    k_nope = _dot(ckv_b, wk_ref[...])
    kr_b = kr.astype(BF16)
    for hd in range(MLA_HEADS):
        k_ref[0, :, hd * QK_PAD:hd * QK_PAD + QK_NOPE_DIM] = (
            k_nope[:, hd * QK_NOPE_DIM:(hd + 1) * QK_NOPE_DIM].astype(BF16))
        k_ref[0, :, hd * QK_PAD + QK_NOPE_DIM:(hd + 1) * QK_PAD] = kr_b
    vt_ref[0] = _dot(ckv_b, wvup_ref[...]).T.astype(BF16)


def _mixer_in(x, mods, w, cos, sin, *, rope, tm):
    nb, l, _ = x.shape
    q_scale = (QK_NOPE_DIM + QK_ROPE_DIM) ** -0.5 * math.log2(math.e)
    tok = lambda width: pl.BlockSpec((1, tm, width), lambda b, i: (b, i, 0))
    tok_t = lambda height: pl.BlockSpec((1, height, tm), lambda b, i: (b, 0, i))
    table = pl.BlockSpec((tm, ROPE_PAD), (lambda b, i: (i, 0)) if rope else (lambda b, i: (0, 0)))
    weights = [w['norm_mix'], w['w_u'], w['w_v'], w['w_lat'], w['w_ga'], w['v_norm'],
               w['w_s'], w['b_s'], w['q_norm'], w['w_q'], w['w_q_sw'], w['kv_norm'],
               w['w_k'], w['w_vup'], w['w_a']]
    return pl.pallas_call(
        functools.partial(_mixer_in_kernel, rope=rope, q_scale=q_scale),
        grid=(nb, l // tm),
        in_specs=[tok(D_MODEL), pl.BlockSpec((1, N_MOD, D_MODEL), lambda b, i: (b, 0, 0))]
                 + [_resident(a.shape) for a in weights] + [table, table],
        out_specs=[tok(D_MODEL), tok_t(MLA_HEADS * QK_PAD), tok(MLA_HEADS * QK_PAD),
                   tok_t(MLA_HEADS * V_HEAD_DIM), tok(KV_LORA_RANK), tok(QK_ROPE_DIM)],
        out_shape=[jax.ShapeDtypeStruct((nb, l, D_MODEL), F32),
                   jax.ShapeDtypeStruct((nb, MLA_HEADS * QK_PAD, l), BF16),
                   jax.ShapeDtypeStruct((nb, l, MLA_HEADS * QK_PAD), BF16),
                   jax.ShapeDtypeStruct((nb, MLA_HEADS * V_HEAD_DIM, l), BF16),
                   jax.ShapeDtypeStruct((nb, l, KV_LORA_RANK), F32),
                   jax.ShapeDtypeStruct((nb, l, QK_ROPE_DIM), F32)],
        scratch_shapes=[pltpu.VMEM((tm, GMLP_DIM), F32)],
        compiler_params=_params(2),
        name="mixer_in_rope" if rope else "mixer_in",
    )(x, mods, *weights, cos, sin)


def _cache_kv_kernel(ckv_ref, kr_ref, wk_ref, wvup_ref, k_ref, vt_ref):
    ckv_b = ckv_ref[0].astype(BF16)
    k_nope = _dot(ckv_b, wk_ref[...])
    kr_b = kr_ref[0].astype(BF16)
    for hd in range(MLA_HEADS):
        k_ref[0, :, hd * QK_PAD:hd * QK_PAD + QK_NOPE_DIM] = (
            k_nope[:, hd * QK_NOPE_DIM:(hd + 1) * QK_NOPE_DIM].astype(BF16))
        k_ref[0, :, hd * QK_PAD + QK_NOPE_DIM:(hd + 1) * QK_PAD] = kr_b
    vt_ref[0] = _dot(ckv_b, wvup_ref[...]).T.astype(BF16)


def _cache_kv(ckv, kr_pad, wk, wvup):
    nb, l, _ = ckv.shape
    return pl.pallas_call(
        _cache_kv_kernel,
        grid=(nb,),
        in_specs=[pl.BlockSpec((1, l, KV_LORA_RANK), lambda b: (b, 0, 0)),
                  pl.BlockSpec((1, l, ROPE_PAD), lambda b: (b, 0, 0)),
                  _resident(wk.shape), _resident(wvup.shape)],
        out_specs=[pl.BlockSpec((1, l, MLA_HEADS * QK_PAD), lambda b: (b, 0, 0)),
                   pl.BlockSpec((1, MLA_HEADS * V_HEAD_DIM, l), lambda b: (b, 0, 0))],
        out_shape=[jax.ShapeDtypeStruct((nb, l, MLA_HEADS * QK_PAD), BF16),
                   jax.ShapeDtypeStruct((nb, MLA_HEADS * V_HEAD_DIM, l), BF16)],
        compiler_params=_params(1),
        name="cache_kv",
    )(ckv, kr_pad, wk, wvup)


def _attn_kernel(qt_ref, k_ref, vt_ref, o_ref, *, heads):
    for j in range(heads):
        qt = qt_ref[0, j * QK_PAD:(j + 1) * QK_PAD, :]
        k = k_ref[0, :, j * QK_PAD:(j + 1) * QK_PAD]
        s = _dot(k, qt)
        m = jnp.max(s, axis=0, keepdims=True)
        p = jnp.exp2(s - m)
        l = jnp.sum(p, axis=0, keepdims=True)
        vt = vt_ref[0, j * V_HEAD_DIM:(j + 1) * V_HEAD_DIM, :]
        ot = _dot(vt, p.astype(BF16)) * (1.0 / l)
        o_ref[0, :, j * V_HEAD_DIM:(j + 1) * V_HEAD_DIM] = ot.T.astype(BF16)


def _attention(qt, k, vt, *, n_seq, seq_q, seq_k, heads, tq):
    nb = qt.shape[0]
    lq_total = qt.shape[2]
    q_tiles = seq_q // tq
    hg = MLA_HEADS // heads
    return pl.pallas_call(
        functools.partial(_attn_kernel, heads=heads),
        grid=(nb, n_seq, hg, q_tiles),
        in_specs=[pl.BlockSpec((1, heads * QK_PAD, tq), lambda b, s, h, i: (b, h, s * q_tiles + i)),
                  pl.BlockSpec((1, seq_k, heads * QK_PAD), lambda b, s, h, i: (b, s, h)),
                  pl.BlockSpec((1, heads * V_HEAD_DIM, seq_k), lambda b, s, h, i: (b, h, s))],
        out_specs=pl.BlockSpec((1, tq, heads * V_HEAD_DIM), lambda b, s, h, i: (b, s * q_tiles + i, h)),
        out_shape=jax.ShapeDtypeStruct((nb, lq_total, MLA_HEADS * V_HEAD_DIM), BF16),
        compiler_params=_params(4),
        name="attention",
    )(qt, k, vt)


def _merge_kernel(x_ref, a_ref, ob_ref, mods_ref, gain_ref, wgb_ref, wb_ref, wo_ref, o_ref):
    x = x_ref[0]
    h = _normed_input(x, mods_ref, gain_ref, 3).astype(BF16)
    gb = jax.nn.sigmoid(_dot(h, wgb_ref[...]))
    merged = (a_ref[0] + gb * _dot(ob_ref[0], wb_ref[...])).astype(BF16)
    gate = mods_ref[0, 5:6, :]
    o_ref[0] = x + gate * _dot(merged, wo_ref[...])


def _merge(x, a_part, out_b, mods, gain, wgb, wb, wo, *, tm):
    nb, l, _ = x.shape
    tok = pl.BlockSpec((1, tm, D_MODEL), lambda b, i: (b, i, 0))
    return pl.pallas_call(
        _merge_kernel,
        grid=(nb, l // tm),
        in_specs=[tok, tok, tok,
                  pl.BlockSpec((1, N_MOD, D_MODEL), lambda b, i: (b, 0, 0)),
                  _resident((1, D_MODEL)),
                  _resident(wgb.shape), _resident(wb.shape), _resident(wo.shape)],
        out_specs=tok,
        out_shape=jax.ShapeDtypeStruct(x.shape, F32),
        compiler_params=_params(2),
        name="merge",
    )(x, a_part, out_b, mods, gain, wgb, wb, wo)


def _rope_tables(l):
    rows = l // GRID_W
    r = jnp.repeat(jnp.arange(rows, dtype=F32), GRID_W)
    col = jnp.tile(jnp.arange(GRID_W, dtype=F32), rows)
    half = QK_ROPE_DIM // 2
    inv = 1.0 / (ROPE_BASE ** (jnp.arange(0, half, 2, dtype=F32) / half))
    ang_r, ang_c = r[:, None] * inv, col[:, None] * inv
    cos = jnp.concatenate([jnp.cos(ang_r), jnp.cos(ang_r), jnp.cos(ang_c), jnp.cos(ang_c),
                           jnp.ones((l, ROPE_PAD - QK_ROPE_DIM), F32)], axis=1)
    sin = jnp.concatenate([-jnp.sin(ang_r), jnp.sin(ang_r), -jnp.sin(ang_c), jnp.sin(ang_c),
                           jnp.zeros((l, ROPE_PAD - QK_ROPE_DIM), F32)], axis=1)
    return cos, sin


def _layer_weights(l, norm_ffn1, ffn1_w_in, ffn1_w_out, norm_mix, w_in, gmlp_v_norm, gmlp_w_s,
                   gmlp_b_s, q_norm, w_q_up, kv_norm, w_kv_up, w_a_proj, w_b_proj, w_o,
                   norm_ffn2, ffn2_w_in, ffn2_w_out):
    bf = lambda a: a.astype(BF16)
    row = lambda a: a.reshape(1, -1)
    swap = np.arange(QK_ROPE_DIM) ^ (QK_ROPE_DIM // 4)
    win = w_in[l]
    o_v, o_q, o_kv, o_kr, o_ga = np.cumsum(
        [GMLP_DIM, GMLP_DIM, Q_LORA_RANK, KV_LORA_RANK, QK_ROPE_DIM]).tolist()
    o_gb = o_ga + D_MODEL
    kr_w = win[:, o_kr:o_ga]
    pad_kr = jnp.zeros((D_MODEL, ROPE_PAD - QK_ROPE_DIM), F32)
    w_lat = jnp.concatenate([win[:, o_q:o_kr], kr_w, pad_kr, kr_w[:, swap], pad_kr], axis=1)

    wq3 = w_q_up[l].reshape(Q_LORA_RANK, MLA_HEADS, QK_NOPE_DIM + QK_ROPE_DIM)
    pad_q = jnp.zeros((Q_LORA_RANK, MLA_HEADS, ROPE_PAD - QK_ROPE_DIM), F32)
    w_q = jnp.concatenate([wq3, pad_q], axis=2).reshape(Q_LORA_RANK, MLA_HEADS * QK_PAD)
    w_q_sw = jnp.concatenate([wq3[:, :, QK_NOPE_DIM:][:, :, swap], pad_q], axis=2).reshape(
        Q_LORA_RANK, MLA_HEADS * ROPE_PAD)
    wkv3 = w_kv_up[l].reshape(KV_LORA_RANK, MLA_HEADS, QK_NOPE_DIM + V_HEAD_DIM)
    return {
        'norm_ffn1': row(norm_ffn1[l]),
        'ffn1_g': bf(ffn1_w_in[l][:, :FFN_DIM]), 'ffn1_u': bf(ffn1_w_in[l][:, FFN_DIM:]),
        'ffn1_o': bf(ffn1_w_out[l]),
        'norm_mix': row(norm_mix[l]),
        'w_u': bf(win[:, :GMLP_DIM]), 'w_v': bf(win[:, o_v:o_q]), 'w_lat': bf(w_lat),
        'w_ga': bf(win[:, o_ga:o_gb]), 'w_gb': bf(win[:, o_gb:]),
        'v_norm': row(gmlp_v_norm[l]),
        'w_s': bf(gmlp_w_s[l]),
        'b_s': jnp.repeat(gmlp_b_s[l], GMLP_GROUP_DIM, axis=1),
        'q_norm': row(q_norm[l]), 'w_q': bf(w_q), 'w_q_sw': bf(w_q_sw),
        'kv_norm': row(kv_norm[l]),
        'w_k': bf(wkv3[:, :, :QK_NOPE_DIM].reshape(KV_LORA_RANK, -1)),
        'w_vup': bf(wkv3[:, :, QK_NOPE_DIM:].reshape(KV_LORA_RANK, -1)),
        'w_a': bf(w_a_proj[l]), 'w_b': bf(w_b_proj[l]), 'w_o': bf(w_o[l]),
        'norm_ffn2': row(norm_ffn2[l]),
        'ffn2_g': bf(ffn2_w_in[l][:, :FFN_DIM]), 'ffn2_u': bf(ffn2_w_in[l][:, FFN_DIM:]),
        'ffn2_o': bf(ffn2_w_out[l]),
    }


def _trunk_layer(x, mods, w, fin, cos, sin, cache, *, n_seq, seq, final_norm, tm, tm_mix, tq, heads):
    rope = cache is not None
    x = _ffn(x, mods, w['norm_ffn1'], w['ffn1_g'], w['ffn1_u'], w['ffn1_o'], fin,
             mod_base=0, final_norm=False, tm=tm)
    a_part, qt, k, vt, ckv, krope = _mixer_in(x, mods, w, cos, sin, rope=rope, tm=tm_mix)
    seq_k = seq
    if rope:
        k_c, vt_c = cache
        k = jnp.concatenate([k_c, k], axis=1)
        vt = jnp.concatenate([vt_c, vt], axis=2)
        seq_k = seq + k_c.shape[1]
    out_b = _attention(qt, k, vt, n_seq=n_seq, seq_q=seq, seq_k=seq_k, heads=heads, tq=tq)
    x = _merge(x, a_part, out_b, mods, w['norm_mix'], w['w_gb'], w['w_b'], w['w_o'], tm=tm)
    x = _ffn(x, mods, w['norm_ffn2'], w['ffn2_g'], w['ffn2_u'], w['ffn2_o'], fin,
             mod_base=6, final_norm=final_norm, tm=tm)
    return x, ckv, krope


def kernel(x_prompt, x_sample, c, cache_ckv, cache_krope, c_ctx, mod_w, mod_b, norm_ffn1, ffn1_w_in, ffn1_w_out, norm_mix, w_in, gmlp_v_norm, gmlp_w_s, gmlp_b_s, q_norm, w_q_up, kv_norm, w_kv_up, w_a_proj, w_b_proj, w_o, norm_ffn2, ffn2_w_in, ffn2_w_out, norm_final):
    batch, seq, _ = x_prompt.shape
    dec_batch, dec_seq, _ = x_sample.shape
    depth = mod_w.shape[0]
    fin = norm_final.reshape(1, D_MODEL)
    cos, sin = _rope_tables(dec_seq)
    cond = jnp.concatenate([c_ctx[None, :], c, jnp.zeros((8 - 1 - dec_batch, D_MODEL), F32)], axis=0)

    xp = x_prompt.reshape(1, batch * seq, D_MODEL)
    xs = x_sample
    ckv_list, krope_list = [], []
    for l in range(depth):
        w = _layer_weights(l, norm_ffn1, ffn1_w_in, ffn1_w_out, norm_mix, w_in, gmlp_v_norm,
                           gmlp_w_s, gmlp_b_s, q_norm, w_q_up, kv_norm, w_kv_up, w_a_proj,
                           w_b_proj, w_o, norm_ffn2, ffn2_w_in, ffn2_w_out)
        mods = _mods(cond, mod_w[l], mod_b[l]).reshape(8, N_MOD, D_MODEL)
        last = l == depth - 1
        xp, ckv_l, krope_l = _trunk_layer(
            xp, mods[0:1], w, fin, cos, sin, None, n_seq=batch, seq=seq,
            final_norm=last, tm=512, tm_mix=256, tq=seq, heads=MLA_HEADS)
        ckv_list.append(ckv_l.reshape(batch, seq, KV_LORA_RANK))
        krope_list.append(krope_l.reshape(batch, seq, QK_ROPE_DIM))
        kr_pad = jnp.pad(cache_krope[:, l], ((0, 0), (0, 0), (0, ROPE_PAD - QK_ROPE_DIM)))
        cache = _cache_kv(cache_ckv[:, l], kr_pad, w['w_k'], w['w_vup'])
        xs, _, _ = _trunk_layer(
            xs, mods[1:1 + dec_batch], w, fin, cos, sin, cache, n_seq=1, seq=dec_seq,
            final_norm=last, tm=512, tm_mix=256, tq=256, heads=1)
    y_prompt = xp.reshape(batch, seq, D_MODEL)
    new_ckv = jnp.stack(ckv_list, axis=1)
    new_krope = jnp.stack(krope_list, axis=1)
    return (y_prompt, xs, new_ckv, new_krope)
```

```python
import functools
import math

import numpy as np
import jax
import jax.numpy as jnp
from jax import lax
from jax.experimental import pallas as pl
from jax.experimental.pallas import tpu as pltpu

D_MODEL = 1024
GRID_W = 64
FFN_DIM = 2816
GMLP_GROUPS = 8
GMLP_GROUP_DIM = 128
GMLP_DIM = GMLP_GROUPS * GMLP_GROUP_DIM
CHUNK = 128
MLA_HEADS = 8
QK_NOPE_DIM = 128
QK_ROPE_DIM = 64
V_HEAD_DIM = 128
Q_LORA_RANK = 256
KV_LORA_RANK = 256
ROPE_BASE = 10000.0
N_MOD = 9
EPS = 1e-6

QK_PAD = 256
ROPE_PAD = QK_PAD - QK_NOPE_DIM
LAT_DIM = Q_LORA_RANK + KV_LORA_RANK + 2 * ROPE_PAD
V7X_VMEM_LIMIT = 56 * 1024 * 1024
Q_SCALE = (QK_NOPE_DIM + QK_ROPE_DIM) ** -0.5 * math.log2(math.e)

F32 = jnp.float32
BF16 = jnp.bfloat16


def _dot(a, b):
    return jnp.dot(a, b, preferred_element_type=F32)


def _rms(x, g):
    return x * lax.rsqrt(jnp.mean(x * x, axis=-1, keepdims=True) + EPS) * g


def _normed_input(x, mods_ref, gain_ref, base):
    shift = mods_ref[0, base:base + 1, :]
    scale = mods_ref[0, base + 1:base + 2, :]
    return _rms(x, gain_ref[...]) * (1.0 + scale) + shift


def _resident(shape):
    nd = len(shape)
    return pl.BlockSpec(shape, lambda *_: (0,) * nd, pipeline_mode=pl.Buffered(1))


def _params(n_grid):
    return pltpu.CompilerParams(dimension_semantics=("parallel",) * n_grid,
                                vmem_limit_bytes=V7X_VMEM_LIMIT)


def _mods_kernel(c_ref, w_ref, b_ref, o_ref):
    c = c_ref[...]
    s = (c * jax.nn.sigmoid(c)).astype(BF16)
    o_ref[...] = _dot(s, w_ref[...].astype(BF16)) + b_ref[...]


def _mods(cond, mod_w, mod_b):
    n = mod_w.shape[1]
    tn = D_MODEL
    return pl.pallas_call(
        _mods_kernel,
        grid=(n // tn,),
        in_specs=[pl.BlockSpec((8, D_MODEL), lambda j: (0, 0)),
                  pl.BlockSpec((D_MODEL, tn), lambda j: (0, j)),
                  pl.BlockSpec((1, tn), lambda j: (0, j))],
        out_specs=pl.BlockSpec((8, tn), lambda j: (0, j)),
        out_shape=jax.ShapeDtypeStruct((8, n), F32),
        compiler_params=_params(1),
        name="adaln_mods",
    )(cond, mod_w, mod_b.reshape(1, n))


def _ffn_kernel(x_ref, mods_ref, gain_ref, wg_ref, wu_ref, wo_ref, fin_ref, o_ref, *,
                mod_base, final_norm):
    x = x_ref[0]
    h = _normed_input(x, mods_ref, gain_ref, mod_base).astype(BF16)
    g = _dot(h, wg_ref[...])
    u = _dot(h, wu_ref[...])
    a = (g * jax.nn.sigmoid(g) * u).astype(BF16)
    y = _dot(a, wo_ref[...])
    gate = mods_ref[0, mod_base + 2:mod_base + 3, :]
    out = x + 0.5 * gate * y
    if final_norm:
        out = _rms(out, fin_ref[...])
    o_ref[0] = out


def _ffn(x, mods, gain, wg, wu, wo, fin, *, mod_base, final_norm, tm):
    nb, l, _ = x.shape
    tok = pl.BlockSpec((1, tm, D_MODEL), lambda b, i: (b, i, 0))
    return pl.pallas_call(
        functools.partial(_ffn_kernel, mod_base=mod_base, final_norm=final_norm),
        grid=(nb, l // tm),
        in_specs=[tok,
                  pl.BlockSpec((1, N_MOD, D_MODEL), lambda b, i: (b, 0, 0)),
                  _resident((1, D_MODEL)),
                  _resident(wg.shape), _resident(wu.shape), _resident(wo.shape),
                  _resident((1, D_MODEL))],
        out_specs=tok,
        out_shape=jax.ShapeDtypeStruct(x.shape, F32),
        compiler_params=_params(2),
        name="ffn_final" if final_norm else "ffn",
    )(x, mods, gain, wg, wu, wo, fin)


def _store_kv(ckv_b, kr_b, wk_ref, wvup_ref, k_ref, vt_ref):
    k_nope = _dot(ckv_b, wk_ref[...])
    for hd in range(MLA_HEADS):
        k_ref[0, :, hd * QK_PAD:hd * QK_PAD + QK_NOPE_DIM] = (
            k_nope[:, hd * QK_NOPE_DIM:(hd + 1) * QK_NOPE_DIM].astype(BF16))
        k_ref[0, :, hd * QK_PAD + QK_NOPE_DIM:(hd + 1) * QK_PAD] = kr_b
    vt_ref[0] = _dot(ckv_b, wvup_ref[...]).T.astype(BF16)


def _mixer_tokens(x_ref, mods_ref, gain_ref, wu_ref, wv_ref, wlat_ref, wga_ref, vnorm_ref,
                  ws_ref, bs_ref, qnorm_ref, wq_ref, kvnorm_ref, wk_ref, wvup_ref, wa_ref,
                  a_ref, qt_ref, k_ref, vt_ref, mixed_ref, *, rope_refs, latent_refs):
    tm = x_ref.shape[1]
    x = x_ref[0]
    h = _normed_input(x, mods_ref, gain_ref, 3).astype(BF16)
    u = _dot(h, wu_ref[...])
    v = _dot(h, wv_ref[...])
    lat = _dot(h, wlat_ref[...])
    ga = jax.nn.sigmoid(_dot(h, wga_ref[...]))

    vv = _rms(v, vnorm_ref[...]).astype(BF16)
    n_chunks = tm // CHUNK
    for g in range(GMLP_GROUPS):
        cols = slice(g * GMLP_GROUP_DIM, (g + 1) * GMLP_GROUP_DIM)
        blk = jnp.concatenate(
            [vv[c * CHUNK:(c + 1) * CHUNK, cols] for c in range(n_chunks)], axis=1)
        mix = _dot(ws_ref[g], blk)
        for c in range(n_chunks):
            mixed_ref[c * CHUNK:(c + 1) * CHUNK, cols] = (
                mix[:, c * CHUNK:(c + 1) * CHUNK] + bs_ref[:, cols])
    out_a = (u * mixed_ref[...]).astype(BF16)
    a_ref[0] = ga * _dot(out_a, wa_ref[...])

    q_lat = lat[:, 0:Q_LORA_RANK]
    ckv = _rms(lat[:, Q_LORA_RANK:Q_LORA_RANK + KV_LORA_RANK], kvnorm_ref[...])
    kr_off = Q_LORA_RANK + KV_LORA_RANK
    kr = lat[:, kr_off:kr_off + ROPE_PAD]
    if latent_refs is not None:
        ckv_ref, krope_ref = latent_refs
        ckv_ref[0] = ckv
        krope_ref[0] = kr[:, 0:QK_ROPE_DIM]

    qn = _rms(q_lat, qnorm_ref[...]).astype(BF16)
    q_all = _dot(qn, wq_ref[...])
    if rope_refs is not None:
        wqsw_ref, cos_ref, sin_ref = rope_refs
        cos = cos_ref[...]
        sin = sin_ref[...]
        q_sw = _dot(qn, wqsw_ref[...])
        kr = kr * cos + lat[:, kr_off + ROPE_PAD:kr_off + 2 * ROPE_PAD] * sin
    for hd in range(MLA_HEADS):
        nope = q_all[:, hd * QK_PAD:hd * QK_PAD + QK_NOPE_DIM]
        rot = q_all[:, hd * QK_PAD + QK_NOPE_DIM:(hd + 1) * QK_PAD]
        if rope_refs is not None:
            rot = rot * cos + q_sw[:, hd * ROPE_PAD:(hd + 1) * ROPE_PAD] * sin
        qt_ref[0, hd * QK_PAD:hd * QK_PAD + QK_NOPE_DIM, :] = (nope * Q_SCALE).T.astype(BF16)
        qt_ref[0, hd * QK_PAD + QK_NOPE_DIM:(hd + 1) * QK_PAD, :] = (rot * Q_SCALE).T.astype(BF16)
    _store_kv(ckv.astype(BF16), kr.astype(BF16), wk_ref, wvup_ref, k_ref, vt_ref)


def _mixer_ctx_kernel(x_ref, mods_ref, gain_ref, wu_ref, wv_ref, wlat_ref, wga_ref, vnorm_ref,
                      ws_ref, bs_ref, qnorm_ref, wq_ref, kvnorm_ref, wk_ref, wvup_ref, wa_ref,
                      a_ref, qt_ref, k_ref, vt_ref, ckv_ref, krope_ref, mixed_ref):
    _mixer_tokens(x_ref, mods_ref, gain_ref, wu_ref, wv_ref, wlat_ref, wga_ref, vnorm_ref,
                  ws_ref, bs_ref, qnorm_ref, wq_ref, kvnorm_ref, wk_ref, wvup_ref, wa_ref,
                  a_ref, qt_ref, k_ref, vt_ref, mixed_ref,
                  rope_refs=None, latent_refs=(ckv_ref, krope_ref))


def _mixer_lat_kernel(x_ref, mods_ref, gain_ref, wu_ref, wv_ref, wlat_ref, wga_ref, vnorm_ref,
                      ws_ref, bs_ref, qnorm_ref, wq_ref, kvnorm_ref, wk_ref, wvup_ref, wa_ref,
                      wqsw_ref, cos_ref, sin_ref, cckv_ref, ckr_ref,
                      a_ref, qt_ref, k_ref, vt_ref, mixed_ref):
    step = pl.program_id(1)

    @pl.when(step == 0)
    def _():
        _store_kv(cckv_ref[0].astype(BF16), ckr_ref[0].astype(BF16), wk_ref, wvup_ref, k_ref, vt_ref)

    @pl.when(step > 0)
    def _():
        _mixer_tokens(x_ref, mods_ref, gain_ref, wu_ref, wv_ref, wlat_ref, wga_ref, vnorm_ref,
                      ws_ref, bs_ref, qnorm_ref, wq_ref, kvnorm_ref, wk_ref, wvup_ref, wa_ref,
                      a_ref, qt_ref, k_ref, vt_ref, mixed_ref,
                      rope_refs=(wqsw_ref, cos_ref, sin_ref), latent_refs=None)


def _mixer_weights(w):
    return [w['norm_mix'], w['w_u'], w['w_v'], w['w_lat'], w['w_ga'], w['v_norm'], w['w_s'],
            w['b_s'], w['q_norm'], w['w_q'], w['kv_norm'], w['w_k'], w['w_vup'], w['w_a']]


def _mixer_ctx(x, mods, w, *, tm):
    nb, l, _ = x.shape
    tok = lambda width: pl.BlockSpec((1, tm, width), lambda b, i: (b, i, 0))
    tok_t = lambda height: pl.BlockSpec((1, height, tm), lambda b, i: (b, 0, i))
    weights = _mixer_weights(w)
    return pl.pallas_call(
        _mixer_ctx_kernel,
        grid=(nb, l // tm),
        in_specs=[tok(D_MODEL), pl.BlockSpec((1, N_MOD, D_MODEL), lambda b, i: (b, 0, 0))]
                 + [_resident(a.shape) for a in weights],
        out_specs=[tok(D_MODEL), tok_t(MLA_HEADS * QK_PAD), tok(MLA_HEADS * QK_PAD),
                   tok_t(MLA_HEADS * V_HEAD_DIM), tok(KV_LORA_RANK), tok(QK_ROPE_DIM)],
        out_shape=[jax.ShapeDtypeStruct((nb, l, D_MODEL), F32),
                   jax.ShapeDtypeStruct((nb, MLA_HEADS * QK_PAD, l), BF16),
                   jax.ShapeDtypeStruct((nb, l, MLA_HEADS * QK_PAD), BF16),
                   jax.ShapeDtypeStruct((nb, MLA_HEADS * V_HEAD_DIM, l), BF16),
                   jax.ShapeDtypeStruct((nb, l, KV_LORA_RANK), F32),
                   jax.ShapeDtypeStruct((nb, l, QK_ROPE_DIM), F32)],
        scratch_shapes=[pltpu.VMEM((tm, GMLP_DIM), F32)],
        compiler_params=_params(2),
        name="mixer_ctx",
    )(x, mods, *weights)


def _mixer_lat(x, mods, w, cos, sin, cache_ckv, cache_kr, *, tm):
    nb, l, _ = x.shape
    past = cache_ckv.shape[1]
    assert past == tm, "the cached context must fill exactly one key block"
    prev = lambda i: jnp.maximum(i - 1, 0)
    tok = pl.BlockSpec((1, tm, D_MODEL), lambda b, i: (b, prev(i), 0))
    table = pl.BlockSpec((tm, ROPE_PAD), lambda b, i: (prev(i), 0))
    weights = _mixer_weights(w) + [w['w_q_sw']]
    return pl.pallas_call(
        _mixer_lat_kernel,
        grid=(nb, 1 + l // tm),
        in_specs=[tok, pl.BlockSpec((1, N_MOD, D_MODEL), lambda b, i: (b, 0, 0))]
                 + [_resident(a.shape) for a in weights] + [table, table]
                 + [pl.BlockSpec((1, past, KV_LORA_RANK), lambda b, i: (b, 0, 0)),
                    pl.BlockSpec((1, past, ROPE_PAD), lambda b, i: (b, 0, 0))],
        out_specs=[tok,
                   pl.BlockSpec((1, MLA_HEADS * QK_PAD, tm), lambda b, i: (b, 0, prev(i))),
                   pl.BlockSpec((1, tm, MLA_HEADS * QK_PAD), lambda b, i: (b, i, 0)),
                   pl.BlockSpec((1, MLA_HEADS * V_HEAD_DIM, tm), lambda b, i: (b, 0, i))],
        out_shape=[jax.ShapeDtypeStruct((nb, l, D_MODEL), F32),
                   jax.ShapeDtypeStruct((nb, MLA_HEADS * QK_PAD, l), BF16),
                   jax.ShapeDtypeStruct((nb, past + l, MLA_HEADS * QK_PAD), BF16),
                   jax.ShapeDtypeStruct((nb, MLA_HEADS * V_HEAD_DIM, past + l), BF16)],
        scratch_shapes=[pltpu.VMEM((tm, GMLP_DIM), F32)],
        compiler_params=pltpu.CompilerParams(dimension_semantics=("parallel", "arbitrary"),
                                             vmem_limit_bytes=V7X_VMEM_LIMIT),
        name="mixer_lat",
    )(x, mods, *weights, cos, sin, cache_ckv, cache_kr)


def _attn_kernel(qt_ref, k_ref, vt_ref, o_ref, *, heads, tk):
    seq_k = k_ref.shape[1]
    for j in range(heads):
        qt = qt_ref[0, j * QK_PAD:(j + 1) * QK_PAD, :]
        scores = lambda c: _dot(k_ref[0, c * tk:(c + 1) * tk, j * QK_PAD:(j + 1) * QK_PAD], qt)
        n_chunks = seq_k // tk
        m = l = acc = None
        ahead = 2
        pending = [scores(c) for c in range(min(ahead, n_chunks))]
        for c in range(n_chunks):
            s = pending.pop(0)
            if c + ahead < n_chunks:
                pending.append(scores(c + ahead))
            vt = vt_ref[0, j * V_HEAD_DIM:(j + 1) * V_HEAD_DIM, c * tk:(c + 1) * tk]
            m_c = jnp.max(s, axis=0, keepdims=True)
            if c == 0:
                m = m_c
                p = jnp.exp2(s - m)
                l = jnp.sum(p, axis=0, keepdims=True)
                acc = _dot(vt, p.astype(BF16))
            else:
                m_new = jnp.maximum(m, m_c)
                alpha = jnp.exp2(m - m_new)
                p = jnp.exp2(s - m_new)
                l = alpha * l + jnp.sum(p, axis=0, keepdims=True)
                acc = alpha * acc + _dot(vt, p.astype(BF16))
                m = m_new
        ot = acc * (1.0 / l)
        o_ref[0, :, j * V_HEAD_DIM:(j + 1) * V_HEAD_DIM] = ot.T.astype(BF16)


def _attention(qt, k, vt, *, n_seq, seq_q, seq_k, heads, tq, tk):
    nb = qt.shape[0]
    lq_total = qt.shape[2]
    q_tiles = seq_q // tq
    hg = MLA_HEADS // heads
    return pl.pallas_call(
        functools.partial(_attn_kernel, heads=heads, tk=tk),
        grid=(nb, n_seq, hg, q_tiles),
        in_specs=[pl.BlockSpec((1, heads * QK_PAD, tq), lambda b, s, h, i: (b, h, s * q_tiles + i)),
                  pl.BlockSpec((1, seq_k, heads * QK_PAD), lambda b, s, h, i: (b, s, h)),
                  pl.BlockSpec((1, heads * V_HEAD_DIM, seq_k), lambda b, s, h, i: (b, h, s))],
        out_specs=pl.BlockSpec((1, tq, heads * V_HEAD_DIM), lambda b, s, h, i: (b, s * q_tiles + i, h)),
        out_shape=jax.ShapeDtypeStruct((nb, lq_total, MLA_HEADS * V_HEAD_DIM), BF16),
        compiler_params=_params(4),
        name="attention",
    )(qt, k, vt)


def _merge_kernel(x_ref, a_ref, ob_ref, mods_ref, gain_ref, wgb_ref, wb_ref, wo_ref, o_ref):
    x = x_ref[0]
    h = _normed_input(x, mods_ref, gain_ref, 3).astype(BF16)
    gb = jax.nn.sigmoid(_dot(h, wgb_ref[...]))
    merged = (a_ref[0] + gb * _dot(ob_ref[0], wb_ref[...])).astype(BF16)
    gate = mods_ref[0, 5:6, :]
    o_ref[0] = x + gate * _dot(merged, wo_ref[...])


def _merge(x, a_part, out_b, mods, gain, wgb, wb, wo, *, tm):
    nb, l, _ = x.shape
    tok = pl.BlockSpec((1, tm, D_MODEL), lambda b, i: (b, i, 0))
    return pl.pallas_call(
        _merge_kernel,
        grid=(nb, l // tm),
        in_specs=[tok, tok, tok,
                  pl.BlockSpec((1, N_MOD, D_MODEL), lambda b, i: (b, 0, 0)),
                  _resident((1, D_MODEL)),
                  _resident(wgb.shape), _resident(wb.shape), _resident(wo.shape)],
        out_specs=tok,
        out_shape=jax.ShapeDtypeStruct(x.shape, F32),
        compiler_params=_params(2),
        name="merge",
    )(x, a_part, out_b, mods, gain, wgb, wb, wo)


def _rope_tables(l):
    rows = l // GRID_W
    r = np.repeat(np.arange(rows, dtype=np.float32), GRID_W)
    col = np.tile(np.arange(GRID_W, dtype=np.float32), rows)
    half = QK_ROPE_DIM // 2
    inv = (1.0 / (np.float32(ROPE_BASE) ** (np.arange(0, half, 2, dtype=np.float32) / half))
           ).astype(np.float32)
    ang_r, ang_c = r[:, None] * inv, col[:, None] * inv
    pad = ROPE_PAD - QK_ROPE_DIM
    cos = np.concatenate([np.cos(ang_r), np.cos(ang_r), np.cos(ang_c), np.cos(ang_c),
                          np.ones((l, pad), np.float32)], axis=1)
    sin = np.concatenate([-np.sin(ang_r), np.sin(ang_r), -np.sin(ang_c), np.sin(ang_c),
                          np.zeros((l, pad), np.float32)], axis=1)
    return jnp.asarray(cos, F32), jnp.asarray(sin, F32)


def _layer_weights(l, norm_ffn1, ffn1_w_in, ffn1_w_out, norm_mix, w_in, gmlp_v_norm, gmlp_w_s,
                   gmlp_b_s, q_norm, w_q_up, kv_norm, w_kv_up, w_a_proj, w_b_proj, w_o,
                   norm_ffn2, ffn2_w_in, ffn2_w_out):
    bf = lambda a: a.astype(BF16)
    row = lambda a: a.reshape(1, -1)
    swap = np.arange(QK_ROPE_DIM) ^ (QK_ROPE_DIM // 4)
    win = w_in[l]
    o_v, o_q, o_kv, o_kr, o_ga = np.cumsum(
        [GMLP_DIM, GMLP_DIM, Q_LORA_RANK, KV_LORA_RANK, QK_ROPE_DIM]).tolist()
    o_gb = o_ga + D_MODEL
    kr_w = win[:, o_kr:o_ga]
    pad_kr = jnp.zeros((D_MODEL, ROPE_PAD - QK_ROPE_DIM), F32)
    w_lat = jnp.concatenate([win[:, o_q:o_kr], kr_w, pad_kr, kr_w[:, swap], pad_kr], axis=1)

    wq3 = w_q_up[l].reshape(Q_LORA_RANK, MLA_HEADS, QK_NOPE_DIM + QK_ROPE_DIM)
    pad_q = jnp.zeros((Q_LORA_RANK, MLA_HEADS, ROPE_PAD - QK_ROPE_DIM), F32)
    w_q = jnp.concatenate([wq3, pad_q], axis=2).reshape(Q_LORA_RANK, MLA_HEADS * QK_PAD)
    w_q_sw = jnp.concatenate([wq3[:, :, QK_NOPE_DIM:][:, :, swap], pad_q], axis=2).reshape(
        Q_LORA_RANK, MLA_HEADS * ROPE_PAD)
    wkv3 = w_kv_up[l].reshape(KV_LORA_RANK, MLA_HEADS, QK_NOPE_DIM + V_HEAD_DIM)
    return {
        'norm_ffn1': row(norm_ffn1[l]),
        'ffn1_g': bf(ffn1_w_in[l][:, :FFN_DIM]), 'ffn1_u': bf(ffn1_w_in[l][:, FFN_DIM:]),
        'ffn1_o': bf(ffn1_w_out[l]),
        'norm_mix': row(norm_mix[l]),
        'w_u': bf(win[:, :GMLP_DIM]), 'w_v': bf(win[:, o_v:o_q]), 'w_lat': bf(w_lat),
        'w_ga': bf(win[:, o_ga:o_gb]), 'w_gb': bf(win[:, o_gb:]),
        'v_norm': row(gmlp_v_norm[l]),
        'w_s': bf(gmlp_w_s[l]),
        'b_s': jnp.repeat(gmlp_b_s[l], GMLP_GROUP_DIM, axis=1),
        'q_norm': row(q_norm[l]), 'w_q': bf(w_q), 'w_q_sw': bf(w_q_sw),
        'kv_norm': row(kv_norm[l]),
        'w_k': bf(wkv3[:, :, :QK_NOPE_DIM].reshape(KV_LORA_RANK, -1)),
        'w_vup': bf(wkv3[:, :, QK_NOPE_DIM:].reshape(KV_LORA_RANK, -1)),
        'w_a': bf(w_a_proj[l]), 'w_b': bf(w_b_proj[l]), 'w_o': bf(w_o[l]),
        'norm_ffn2': row(norm_ffn2[l]),
        'ffn2_g': bf(ffn2_w_in[l][:, :FFN_DIM]), 'ffn2_u': bf(ffn2_w_in[l][:, FFN_DIM:]),
        'ffn2_o': bf(ffn2_w_out[l]),
    }


def _trunk_layer(x, mods, w, fin, mixer, *, n_seq, seq, seq_k, final_norm, tm, tq, tk, heads):
    x = _ffn(x, mods, w['norm_ffn1'], w['ffn1_g'], w['ffn1_u'], w['ffn1_o'], fin,
             mod_base=0, final_norm=False, tm=tm)
    a_part, qt, k, vt, *latents = mixer(x)
    out_b = _attention(qt, k, vt, n_seq=n_seq, seq_q=seq, seq_k=seq_k, heads=heads, tq=tq, tk=tk)
    x = _merge(x, a_part, out_b, mods, w['norm_mix'], w['w_gb'], w['w_b'], w['w_o'], tm=tm)
    x = _ffn(x, mods, w['norm_ffn2'], w['ffn2_g'], w['ffn2_u'], w['ffn2_o'], fin,
             mod_base=6, final_norm=final_norm, tm=tm)
    return x, latents


def kernel(x_prompt, x_sample, c, cache_ckv, cache_krope, c_ctx, mod_w, mod_b, norm_ffn1, ffn1_w_in, ffn1_w_out, norm_mix, w_in, gmlp_v_norm, gmlp_w_s, gmlp_b_s, q_norm, w_q_up, kv_norm, w_kv_up, w_a_proj, w_b_proj, w_o, norm_ffn2, ffn2_w_in, ffn2_w_out, norm_final):
    batch, seq, _ = x_prompt.shape
    dec_batch, dec_seq, _ = x_sample.shape
    past = cache_ckv.shape[2]
    depth = mod_w.shape[0]
    fin = norm_final.reshape(1, D_MODEL)
    cos, sin = _rope_tables(dec_seq)
    cond = jnp.concatenate([c_ctx[None, :], c, jnp.zeros((8 - 1 - dec_batch, D_MODEL), F32)], axis=0)

    xp = x_prompt.reshape(1, batch * seq, D_MODEL)
    xs = x_sample
    ckv_list, krope_list = [], []
    for l in range(depth):
        w = _layer_weights(l, norm_ffn1, ffn1_w_in, ffn1_w_out, norm_mix, w_in, gmlp_v_norm,
                           gmlp_w_s, gmlp_b_s, q_norm, w_q_up, kv_norm, w_kv_up, w_a_proj,
                           w_b_proj, w_o, norm_ffn2, ffn2_w_in, ffn2_w_out)
        mods = _mods(cond, mod_w[l], mod_b[l]).reshape(8, N_MOD, D_MODEL)
        mods_ctx, mods_lat = mods[0:1], mods[1:1 + dec_batch]
        last = l == depth - 1
        xp, (ckv_l, krope_l) = _trunk_layer(
            xp, mods_ctx, w, fin, lambda x: _mixer_ctx(x, mods_ctx, w, tm=256),
            n_seq=batch, seq=seq, seq_k=seq, final_norm=last, tm=512, tq=seq, tk=seq,
            heads=MLA_HEADS)
        ckv_list.append(ckv_l.reshape(batch, seq, KV_LORA_RANK))
        krope_list.append(krope_l.reshape(batch, seq, QK_ROPE_DIM))
        kr_pad = jnp.pad(cache_krope[:, l], ((0, 0), (0, 0), (0, ROPE_PAD - QK_ROPE_DIM)))
        xs, _ = _trunk_layer(
            xs, mods_lat, w, fin,
            lambda x: _mixer_lat(x, mods_lat, w, cos, sin, cache_ckv[:, l], kr_pad, tm=past),
            n_seq=1, seq=dec_seq, seq_k=past + dec_seq, final_norm=last, tm=512, tq=512, tk=256,
            heads=1)
    y_prompt = xp.reshape(batch, seq, D_MODEL)
    new_ckv = jnp.stack(ckv_list, axis=1)
    new_krope = jnp.stack(krope_list, axis=1)
    return (y_prompt, xs, new_ckv, new_krope)
```

```python
import functools
import math

import numpy as np
import jax
import jax.numpy as jnp
from jax import lax
from jax.experimental import pallas as pl
from jax.experimental.pallas import tpu as pltpu

D_MODEL = 1024
GRID_W = 64
FFN_DIM = 2816
GMLP_GROUPS = 8
GMLP_GROUP_DIM = 128
GMLP_DIM = GMLP_GROUPS * GMLP_GROUP_DIM
CHUNK = 128
MLA_HEADS = 8
QK_NOPE_DIM = 128
QK_ROPE_DIM = 64
V_HEAD_DIM = 128
Q_LORA_RANK = 256
KV_LORA_RANK = 256
ROPE_BASE = 10000.0
N_MOD = 9
EPS = 1e-6

QK_PAD = 256
ROPE_PAD = QK_PAD - QK_NOPE_DIM
LAT_DIM = Q_LORA_RANK + KV_LORA_RANK + 2 * ROPE_PAD
V7X_VMEM_LIMIT = 56 * 1024 * 1024
Q_SCALE = (QK_NOPE_DIM + QK_ROPE_DIM) ** -0.5 * math.log2(math.e)

F32 = jnp.float32
BF16 = jnp.bfloat16


def _dot(a, b):
    return jnp.dot(a, b, preferred_element_type=F32)


def _rms(x, g):
    return x * lax.rsqrt(jnp.mean(x * x, axis=-1, keepdims=True) + EPS) * g


def _normed_input(x, mods_ref, gain_ref, base):
    shift = mods_ref[0, base:base + 1, :]
    scale = mods_ref[0, base + 1:base + 2, :]
    return _rms(x, gain_ref[...]) * (1.0 + scale) + shift


def _resident(shape):
    nd = len(shape)
    return pl.BlockSpec(shape, lambda *_: (0,) * nd, pipeline_mode=pl.Buffered(1))


def _params(n_grid):
    return pltpu.CompilerParams(dimension_semantics=("parallel",) * n_grid,
                                vmem_limit_bytes=V7X_VMEM_LIMIT)


def _mods_kernel(c_ref, w_ref, b_ref, o_ref):
    c = c_ref[...]
    s = (c * jax.nn.sigmoid(c)).astype(BF16)
    o_ref[...] = _dot(s, w_ref[...].astype(BF16)) + b_ref[...]


def _mods(cond, mod_w, mod_b):
    n = mod_w.shape[1]
    tn = D_MODEL
    return pl.pallas_call(
        _mods_kernel,
        grid=(n // tn,),
        in_specs=[pl.BlockSpec((8, D_MODEL), lambda j: (0, 0)),
                  pl.BlockSpec((D_MODEL, tn), lambda j: (0, j)),
                  pl.BlockSpec((1, tn), lambda j: (0, j))],
        out_specs=pl.BlockSpec((8, tn), lambda j: (0, j)),
        out_shape=jax.ShapeDtypeStruct((8, n), F32),
        compiler_params=_params(1),
        name="adaln_mods",
    )(cond, mod_w, mod_b.reshape(1, n))


def _ffn_kernel(x_ref, mods_ref, gain_ref, wi_ref, wo_ref, fin_ref, o_ref, *,
                mod_base, final_norm):
    x = x_ref[0]
    h = _normed_input(x, mods_ref, gain_ref, mod_base).astype(BF16)
    g = _dot(h, wi_ref[:, :FFN_DIM])
    u = _dot(h, wi_ref[:, FFN_DIM:])
    a = (g * jax.nn.sigmoid(g) * u).astype(BF16)
    y = _dot(a, wo_ref[...])
    gate = mods_ref[0, mod_base + 2:mod_base + 3, :]
    out = x + 0.5 * gate * y
    if final_norm:
        out = _rms(out, fin_ref[...])
    o_ref[0] = out


def _ffn(x, mods, gain, wi, wo, fin, *, mod_base, final_norm, tm):
    nb, l, _ = x.shape
    tok = pl.BlockSpec((1, tm, D_MODEL), lambda b, i: (b, i, 0))
    return pl.pallas_call(
        functools.partial(_ffn_kernel, mod_base=mod_base, final_norm=final_norm),
        grid=(nb, l // tm),
        in_specs=[tok,
                  pl.BlockSpec((1, N_MOD, D_MODEL), lambda b, i: (b, 0, 0)),
                  _resident((1, D_MODEL)),
                  _resident(wi.shape), _resident(wo.shape),
                  _resident((1, D_MODEL))],
        out_specs=tok,
        out_shape=jax.ShapeDtypeStruct(x.shape, F32),
        compiler_params=_params(2),
        name="ffn_final" if final_norm else "ffn",
    )(x, mods, gain, wi, wo, fin)


def _store_kv(ckv_b, kr_b, wk_ref, wvup_ref, k_ref, vt_ref):
    k_nope = _dot(ckv_b, wk_ref[...])
    for hd in range(MLA_HEADS):
        k_ref[0, :, hd * QK_PAD:hd * QK_PAD + QK_NOPE_DIM] = (
            k_nope[:, hd * QK_NOPE_DIM:(hd + 1) * QK_NOPE_DIM].astype(BF16))
        k_ref[0, :, hd * QK_PAD + QK_NOPE_DIM:(hd + 1) * QK_PAD] = kr_b
    vt_ref[0] = _dot(ckv_b, wvup_ref[...]).T.astype(BF16)


def _mixer_tokens(x_ref, mods_ref, gain_ref, wu_ref, wv_ref, wlat_ref, wga_ref, vnorm_ref,
                  ws_ref, bs_ref, qnorm_ref, wq_ref, kvnorm_ref, wk_ref, wvup_ref, wa_ref,
                  a_ref, qt_ref, k_ref, vt_ref, mixed_ref, *, rope_refs, latent_refs):
    tm = x_ref.shape[1]
    x = x_ref[0]
    h = _normed_input(x, mods_ref, gain_ref, 3).astype(BF16)
    u = _dot(h, wu_ref[...])
    v = _dot(h, wv_ref[...])
    lat = _dot(h, wlat_ref[...])
    ga = jax.nn.sigmoid(_dot(h, wga_ref[...]))

    vv = _rms(v, vnorm_ref[...]).astype(BF16)
    n_chunks = tm // CHUNK
    for g in range(GMLP_GROUPS):
        cols = slice(g * GMLP_GROUP_DIM, (g + 1) * GMLP_GROUP_DIM)
        blk = jnp.concatenate(
            [vv[c * CHUNK:(c + 1) * CHUNK, cols] for c in range(n_chunks)], axis=1)
        mix = _dot(ws_ref[g], blk)
        for c in range(n_chunks):
            mixed_ref[c * CHUNK:(c + 1) * CHUNK, cols] = (
                mix[:, c * CHUNK:(c + 1) * CHUNK] + bs_ref[:, cols])
    out_a = (u * mixed_ref[...]).astype(BF16)
    a_ref[0] = ga * _dot(out_a, wa_ref[...])

    q_lat = lat[:, 0:Q_LORA_RANK]
    ckv = _rms(lat[:, Q_LORA_RANK:Q_LORA_RANK + KV_LORA_RANK], kvnorm_ref[...])
    kr_off = Q_LORA_RANK + KV_LORA_RANK
    kr = lat[:, kr_off:kr_off + ROPE_PAD]
    if latent_refs is not None:
        ckv_ref, krope_ref = latent_refs
        ckv_ref[0] = ckv
        krope_ref[0] = kr[:, 0:QK_ROPE_DIM]

    qn = _rms(q_lat, qnorm_ref[...]).astype(BF16)
    q_all = _dot(qn, wq_ref[...])
    if rope_refs is not None:
        wqsw_ref, cos_ref, sin_ref = rope_refs
        cos = cos_ref[...]
        sin = sin_ref[...]
        q_sw = _dot(qn, wqsw_ref[...])
        kr = kr * cos + lat[:, kr_off + ROPE_PAD:kr_off + 2 * ROPE_PAD] * sin
    for hd in range(MLA_HEADS):
        nope = q_all[:, hd * QK_PAD:hd * QK_PAD + QK_NOPE_DIM]
        rot = q_all[:, hd * QK_PAD + QK_NOPE_DIM:(hd + 1) * QK_PAD]
        if rope_refs is not None:
            rot = rot * cos + q_sw[:, hd * ROPE_PAD:(hd + 1) * ROPE_PAD] * sin
        qt_ref[0, hd * QK_PAD:hd * QK_PAD + QK_NOPE_DIM, :] = (nope * Q_SCALE).T.astype(BF16)
        qt_ref[0, hd * QK_PAD + QK_NOPE_DIM:(hd + 1) * QK_PAD, :] = (rot * Q_SCALE).T.astype(BF16)
    _store_kv(ckv.astype(BF16), kr.astype(BF16), wk_ref, wvup_ref, k_ref, vt_ref)


def _mixer_ctx_kernel(x_ref, mods_ref, gain_ref, wu_ref, wv_ref, wlat_ref, wga_ref, vnorm_ref,
                      ws_ref, bs_ref, qnorm_ref, wq_ref, kvnorm_ref, wk_ref, wvup_ref, wa_ref,
                      a_ref, qt_ref, k_ref, vt_ref, ckv_ref, krope_ref, mixed_ref):
    _mixer_tokens(x_ref, mods_ref, gain_ref, wu_ref, wv_ref, wlat_ref, wga_ref, vnorm_ref,
                  ws_ref, bs_ref, qnorm_ref, wq_ref, kvnorm_ref, wk_ref, wvup_ref, wa_ref,
                  a_ref, qt_ref, k_ref, vt_ref, mixed_ref,
                  rope_refs=None, latent_refs=(ckv_ref, krope_ref))


def _mixer_lat_kernel(x_ref, mods_ref, gain_ref, wu_ref, wv_ref, wlat_ref, wga_ref, vnorm_ref,
                      ws_ref, bs_ref, qnorm_ref, wq_ref, kvnorm_ref, wk_ref, wvup_ref, wa_ref,
                      wqsw_ref, cos_ref, sin_ref, cckv_ref, ckr_ref,
                      a_ref, qt_ref, k_ref, vt_ref, mixed_ref):
    step = pl.program_id(1)

    @pl.when(step == 0)
    def _():
        _store_kv(cckv_ref[0].astype(BF16), ckr_ref[0].astype(BF16), wk_ref, wvup_ref, k_ref, vt_ref)

    @pl.when(step > 0)
    def _():
        _mixer_tokens(x_ref, mods_ref, gain_ref, wu_ref, wv_ref, wlat_ref, wga_ref, vnorm_ref,
                      ws_ref, bs_ref, qnorm_ref, wq_ref, kvnorm_ref, wk_ref, wvup_ref, wa_ref,
                      a_ref, qt_ref, k_ref, vt_ref, mixed_ref,
                      rope_refs=(wqsw_ref, cos_ref, sin_ref), latent_refs=None)


def _mixer_weights(w):
    return [w['norm_mix'], w['w_u'], w['w_v'], w['w_lat'], w['w_ga'], w['v_norm'], w['w_s'],
            w['b_s'], w['q_norm'], w['w_q'], w['kv_norm'], w['w_k'], w['w_vup'], w['w_a']]


def _mixer_ctx(x, mods, w, *, tm):
    nb, l, _ = x.shape
    tok = lambda width: pl.BlockSpec((1, tm, width), lambda b, i: (b, i, 0))
    tok_t = lambda height: pl.BlockSpec((1, height, tm), lambda b, i: (b, 0, i))
    weights = _mixer_weights(w)
    return pl.pallas_call(
        _mixer_ctx_kernel,
        grid=(nb, l // tm),
        in_specs=[tok(D_MODEL), pl.BlockSpec((1, N_MOD, D_MODEL), lambda b, i: (b, 0, 0))]
                 + [_resident(a.shape) for a in weights],
        out_specs=[tok(D_MODEL), tok_t(MLA_HEADS * QK_PAD), tok(MLA_HEADS * QK_PAD),
                   tok_t(MLA_HEADS * V_HEAD_DIM), tok(KV_LORA_RANK), tok(QK_ROPE_DIM)],
        out_shape=[jax.ShapeDtypeStruct((nb, l, D_MODEL), F32),
                   jax.ShapeDtypeStruct((nb, MLA_HEADS * QK_PAD, l), BF16),
                   jax.ShapeDtypeStruct((nb, l, MLA_HEADS * QK_PAD), BF16),
                   jax.ShapeDtypeStruct((nb, MLA_HEADS * V_HEAD_DIM, l), BF16),
                   jax.ShapeDtypeStruct((nb, l, KV_LORA_RANK), F32),
                   jax.ShapeDtypeStruct((nb, l, QK_ROPE_DIM), F32)],
        scratch_shapes=[pltpu.VMEM((tm, GMLP_DIM), F32)],
        compiler_params=_params(2),
        name="mixer_ctx",
    )(x, mods, *weights)


def _mixer_lat(x, mods, w, cos, sin, cache_ckv, cache_kr, *, tm):
    nb, l, _ = x.shape
    past = cache_ckv.shape[1]
    assert past == tm, "the cached context must fill exactly one key block"
    prev = lambda i: jnp.maximum(i - 1, 0)
    tok = pl.BlockSpec((1, tm, D_MODEL), lambda b, i: (b, prev(i), 0))
    table = pl.BlockSpec((tm, ROPE_PAD), lambda b, i: (prev(i), 0))
    weights = _mixer_weights(w) + [w['w_q_sw']]
    return pl.pallas_call(
        _mixer_lat_kernel,
        grid=(nb, 1 + l // tm),
        in_specs=[tok, pl.BlockSpec((1, N_MOD, D_MODEL), lambda b, i: (b, 0, 0))]
                 + [_resident(a.shape) for a in weights] + [table, table]
                 + [pl.BlockSpec((1, past, KV_LORA_RANK), lambda b, i: (b, 0, 0)),
                    pl.BlockSpec((1, past, ROPE_PAD), lambda b, i: (b, 0, 0))],
        out_specs=[tok,
                   pl.BlockSpec((1, MLA_HEADS * QK_PAD, tm), lambda b, i: (b, 0, prev(i))),
                   pl.BlockSpec((1, tm, MLA_HEADS * QK_PAD), lambda b, i: (b, i, 0)),
                   pl.BlockSpec((1, MLA_HEADS * V_HEAD_DIM, tm), lambda b, i: (b, 0, i))],
        out_shape=[jax.ShapeDtypeStruct((nb, l, D_MODEL), F32),
                   jax.ShapeDtypeStruct((nb, MLA_HEADS * QK_PAD, l), BF16),
                   jax.ShapeDtypeStruct((nb, past + l, MLA_HEADS * QK_PAD), BF16),
                   jax.ShapeDtypeStruct((nb, MLA_HEADS * V_HEAD_DIM, past + l), BF16)],
        scratch_shapes=[pltpu.VMEM((tm, GMLP_DIM), F32)],
        compiler_params=pltpu.CompilerParams(dimension_semantics=("parallel", "arbitrary"),
                                             vmem_limit_bytes=V7X_VMEM_LIMIT),
        name="mixer_lat",
    )(x, mods, *weights, cos, sin, cache_ckv, cache_kr)


Q_LANES = 256
SCORES_AHEAD = 2


def _attn_kernel(qt_ref, k_ref, vt_ref, o_ref, *, heads, tk, tk_first):
    seq_k = k_ref.shape[1]
    tq = qt_ref.shape[2]
    bounds = [0] + list(range(tk_first, seq_k + 1, tk))
    chunks = [slice(lo, hi) for lo, hi in zip(bounds[:-1], bounds[1:])]
    n_chunks = len(chunks)
    streams = [(j, q0) for j in range(heads) for q0 in range(0, tq, Q_LANES)]
    qts = [qt_ref[0, j * QK_PAD:(j + 1) * QK_PAD, q0:q0 + Q_LANES] for j, q0 in streams]

    def scores(i, c):
        j = streams[i][0]
        return _dot(k_ref[0, chunks[c], j * QK_PAD:(j + 1) * QK_PAD], qts[i])

    pending = [[scores(i, c) for c in range(min(SCORES_AHEAD, n_chunks))]
               for i in range(len(streams))]
    state = [None] * len(streams)
    for c in range(n_chunks):
        for i in range(len(streams)):
            if c + SCORES_AHEAD < n_chunks:
                pending[i].append(scores(i, c + SCORES_AHEAD))
        for i, (j, _) in enumerate(streams):
            s = pending[i].pop(0)
            vt = vt_ref[0, j * V_HEAD_DIM:(j + 1) * V_HEAD_DIM, chunks[c]]
            m_c = jnp.max(s, axis=0, keepdims=True)
            if c == 0:
                p = jnp.exp2(s - m_c)
                state[i] = (m_c, jnp.sum(p, axis=0, keepdims=True), _dot(vt, p.astype(BF16)))
            else:
                m, l, acc = state[i]
                m_new = jnp.maximum(m, m_c)
                alpha = jnp.exp2(m - m_new)
                p = jnp.exp2(s - m_new)
                state[i] = (m_new, alpha * l + jnp.sum(p, axis=0, keepdims=True),
                            alpha * acc + _dot(vt, p.astype(BF16)))
    for i, (j, q0) in enumerate(streams):
        _, l, acc = state[i]
        ot = acc * (1.0 / l)
        o_ref[0, q0:q0 + Q_LANES, j * V_HEAD_DIM:(j + 1) * V_HEAD_DIM] = ot.T.astype(BF16)


def _attention(qt, k, vt, *, n_seq, seq_q, seq_k, heads, tq, tk, tk_first):
    nb = qt.shape[0]
    lq_total = qt.shape[2]
    q_tiles = seq_q // tq
    hg = MLA_HEADS // heads
    return pl.pallas_call(
        functools.partial(_attn_kernel, heads=heads, tk=tk, tk_first=tk_first),
        grid=(nb, n_seq, hg, q_tiles),
        in_specs=[pl.BlockSpec((1, heads * QK_PAD, tq), lambda b, s, h, i: (b, h, s * q_tiles + i)),
                  pl.BlockSpec((1, seq_k, heads * QK_PAD), lambda b, s, h, i: (b, s, h)),
                  pl.BlockSpec((1, heads * V_HEAD_DIM, seq_k), lambda b, s, h, i: (b, h, s))],
        out_specs=pl.BlockSpec((1, tq, heads * V_HEAD_DIM), lambda b, s, h, i: (b, s * q_tiles + i, h)),
        out_shape=jax.ShapeDtypeStruct((nb, lq_total, MLA_HEADS * V_HEAD_DIM), BF16),
        compiler_params=_params(4),
        name="attention",
    )(qt, k, vt)


def _merge_kernel(x_ref, a_ref, ob_ref, mods_ref, gain_ref, wgb_ref, wb_ref, wo_ref, o_ref):
    x = x_ref[0]
    h = _normed_input(x, mods_ref, gain_ref, 3).astype(BF16)
    gb = jax.nn.sigmoid(_dot(h, wgb_ref[...]))
    merged = (a_ref[0] + gb * _dot(ob_ref[0], wb_ref[...])).astype(BF16)
    gate = mods_ref[0, 5:6, :]
    o_ref[0] = x + gate * _dot(merged, wo_ref[...])


def _merge(x, a_part, out_b, mods, gain, wgb, wb, wo, *, tm):
    nb, l, _ = x.shape
    tok = pl.BlockSpec((1, tm, D_MODEL), lambda b, i: (b, i, 0))
    return pl.pallas_call(
        _merge_kernel,
        grid=(nb, l // tm),
        in_specs=[tok, tok, tok,
                  pl.BlockSpec((1, N_MOD, D_MODEL), lambda b, i: (b, 0, 0)),
                  _resident((1, D_MODEL)),
                  _resident(wgb.shape), _resident(wb.shape), _resident(wo.shape)],
        out_specs=tok,
        out_shape=jax.ShapeDtypeStruct(x.shape, F32),
        compiler_params=_params(2),
        name="merge",
    )(x, a_part, out_b, mods, gain, wgb, wb, wo)


def _rope_tables(l):
    rows = l // GRID_W
    r = np.repeat(np.arange(rows, dtype=np.float32), GRID_W)
    col = np.tile(np.arange(GRID_W, dtype=np.float32), rows)
    half = QK_ROPE_DIM // 2
    inv = (1.0 / (np.float32(ROPE_BASE) ** (np.arange(0, half, 2, dtype=np.float32) / half))
           ).astype(np.float32)
    ang_r, ang_c = r[:, None] * inv, col[:, None] * inv
    pad = ROPE_PAD - QK_ROPE_DIM
    cos = np.concatenate([np.cos(ang_r), np.cos(ang_r), np.cos(ang_c), np.cos(ang_c),
                          np.ones((l, pad), np.float32)], axis=1)
    sin = np.concatenate([-np.sin(ang_r), np.sin(ang_r), -np.sin(ang_c), np.sin(ang_c),
                          np.zeros((l, pad), np.float32)], axis=1)
    return jnp.asarray(cos, F32), jnp.asarray(sin, F32)


def _layer_weights(l, norm_ffn1, ffn1_w_in, ffn1_w_out, norm_mix, w_in, gmlp_v_norm, gmlp_w_s,
                   gmlp_b_s, q_norm, w_q_up, kv_norm, w_kv_up, w_a_proj, w_b_proj, w_o,
                   norm_ffn2, ffn2_w_in, ffn2_w_out):
    bf = lambda a: a.astype(BF16)
    row = lambda a: a.reshape(1, -1)
    swap = np.arange(QK_ROPE_DIM) ^ (QK_ROPE_DIM // 4)
    win = w_in[l]
    o_v, o_q, o_kv, o_kr, o_ga = np.cumsum(
        [GMLP_DIM, GMLP_DIM, Q_LORA_RANK, KV_LORA_RANK, QK_ROPE_DIM]).tolist()
    o_gb = o_ga + D_MODEL
    kr_w = win[:, o_kr:o_ga]
    pad_kr = jnp.zeros((D_MODEL, ROPE_PAD - QK_ROPE_DIM), F32)
    w_lat = jnp.concatenate([win[:, o_q:o_kr], kr_w, pad_kr, kr_w[:, swap], pad_kr], axis=1)

    wq3 = w_q_up[l].reshape(Q_LORA_RANK, MLA_HEADS, QK_NOPE_DIM + QK_ROPE_DIM)
    pad_q = jnp.zeros((Q_LORA_RANK, MLA_HEADS, ROPE_PAD - QK_ROPE_DIM), F32)
    w_q = jnp.concatenate([wq3, pad_q], axis=2).reshape(Q_LORA_RANK, MLA_HEADS * QK_PAD)
    w_q_sw = jnp.concatenate([wq3[:, :, QK_NOPE_DIM:][:, :, swap], pad_q], axis=2).reshape(
        Q_LORA_RANK, MLA_HEADS * ROPE_PAD)
    wkv3 = w_kv_up[l].reshape(KV_LORA_RANK, MLA_HEADS, QK_NOPE_DIM + V_HEAD_DIM)
    return {
        'norm_ffn1': row(norm_ffn1[l]),
        'ffn1_i': bf(ffn1_w_in[l]),
        'ffn1_o': bf(ffn1_w_out[l]),
        'norm_mix': row(norm_mix[l]),
        'w_u': bf(win[:, :GMLP_DIM]), 'w_v': bf(win[:, o_v:o_q]), 'w_lat': bf(w_lat),
        'w_ga': bf(win[:, o_ga:o_gb]), 'w_gb': bf(win[:, o_gb:]),
        'v_norm': row(gmlp_v_norm[l]),
        'w_s': bf(gmlp_w_s[l]),
        'b_s': jnp.repeat(gmlp_b_s[l], GMLP_GROUP_DIM, axis=1),
        'q_norm': row(q_norm[l]), 'w_q': bf(w_q), 'w_q_sw': bf(w_q_sw),
        'kv_norm': row(kv_norm[l]),
        'w_k': bf(wkv3[:, :, :QK_NOPE_DIM].reshape(KV_LORA_RANK, -1)),
        'w_vup': bf(wkv3[:, :, QK_NOPE_DIM:].reshape(KV_LORA_RANK, -1)),
        'w_a': bf(w_a_proj[l]), 'w_b': bf(w_b_proj[l]), 'w_o': bf(w_o[l]),
        'norm_ffn2': row(norm_ffn2[l]),
        'ffn2_i': bf(ffn2_w_in[l]),
        'ffn2_o': bf(ffn2_w_out[l]),
    }


def _trunk_layer(x, mods, w, fin, mixer, *, n_seq, seq, seq_k, final_norm, tm, tq, tk, tk_first,
                 heads):
    x = _ffn(x, mods, w['norm_ffn1'], w['ffn1_i'], w['ffn1_o'], fin,
             mod_base=0, final_norm=False, tm=tm)
    a_part, qt, k, vt, *latents = mixer(x)
    out_b = _attention(qt, k, vt, n_seq=n_seq, seq_q=seq, seq_k=seq_k, heads=heads, tq=tq, tk=tk,
                       tk_first=tk_first)
    x = _merge(x, a_part, out_b, mods, w['norm_mix'], w['w_gb'], w['w_b'], w['w_o'], tm=tm)
    x = _ffn(x, mods, w['norm_ffn2'], w['ffn2_i'], w['ffn2_o'], fin,
             mod_base=6, final_norm=final_norm, tm=tm)
    return x, latents


def kernel(x_prompt, x_sample, c, cache_ckv, cache_krope, c_ctx, mod_w, mod_b, norm_ffn1, ffn1_w_in, ffn1_w_out, norm_mix, w_in, gmlp_v_norm, gmlp_w_s, gmlp_b_s, q_norm, w_q_up, kv_norm, w_kv_up, w_a_proj, w_b_proj, w_o, norm_ffn2, ffn2_w_in, ffn2_w_out, norm_final):
    batch, seq, _ = x_prompt.shape
    dec_batch, dec_seq, _ = x_sample.shape
    past = cache_ckv.shape[2]
    depth = mod_w.shape[0]
    fin = norm_final.reshape(1, D_MODEL)
    cos, sin = _rope_tables(dec_seq)
    cond = jnp.concatenate([c_ctx[None, :], c, jnp.zeros((8 - 1 - dec_batch, D_MODEL), F32)], axis=0)

    xp = x_prompt.reshape(1, batch * seq, D_MODEL)
    xs = x_sample
    ckv_list, krope_list = [], []
    for l in range(depth):
        w = _layer_weights(l, norm_ffn1, ffn1_w_in, ffn1_w_out, norm_mix, w_in, gmlp_v_norm,
                           gmlp_w_s, gmlp_b_s, q_norm, w_q_up, kv_norm, w_kv_up, w_a_proj,
                           w_b_proj, w_o, norm_ffn2, ffn2_w_in, ffn2_w_out)
        mods = _mods(cond, mod_w[l], mod_b[l]).reshape(8, N_MOD, D_MODEL)
        mods_ctx, mods_lat = mods[0:1], mods[1:1 + dec_batch]
        last = l == depth - 1
        xp, (ckv_l, krope_l) = _trunk_layer(
            xp, mods_ctx, w, fin, lambda x: _mixer_ctx(x, mods_ctx, w, tm=256),
            n_seq=batch, seq=seq, seq_k=seq, final_norm=last, tm=512, tq=seq, tk=seq, tk_first=seq,
            heads=MLA_HEADS)
        ckv_list.append(ckv_l.reshape(batch, seq, KV_LORA_RANK))
        krope_list.append(krope_l.reshape(batch, seq, QK_ROPE_DIM))
        kr_pad = jnp.pad(cache_krope[:, l], ((0, 0), (0, 0), (0, ROPE_PAD - QK_ROPE_DIM)))
        xs, _ = _trunk_layer(
            xs, mods_lat, w, fin,
            lambda x: _mixer_lat(x, mods_lat, w, cos, sin, cache_ckv[:, l], kr_pad, tm=past),
            n_seq=1, seq=dec_seq, seq_k=past + dec_seq, final_norm=last, tm=512, tq=1024, tk=512, tk_first=past,
            heads=1)
    y_prompt = xp.reshape(batch, seq, D_MODEL)
    new_ckv = jnp.stack(ckv_list, axis=1)
    new_krope = jnp.stack(krope_list, axis=1)
    return (y_prompt, xs, new_ckv, new_krope)
```

```python
import functools
import math

import numpy as np
import jax
import jax.numpy as jnp
from jax import lax
from jax.experimental import pallas as pl
from jax.experimental.pallas import tpu as pltpu

D_MODEL = 1024
GRID_W = 64
FFN_DIM = 2816
GMLP_GROUPS = 8
GMLP_GROUP_DIM = 128
GMLP_DIM = GMLP_GROUPS * GMLP_GROUP_DIM
CHUNK = 128
MLA_HEADS = 8
QK_NOPE_DIM = 128
QK_ROPE_DIM = 64
V_HEAD_DIM = 128
Q_LORA_RANK = 256
KV_LORA_RANK = 256
ROPE_BASE = 10000.0
N_MOD = 9
EPS = 1e-6

V_ROWS = 144
QK_PAD = 256
ROPE_PAD = QK_PAD - QK_NOPE_DIM
V7X_VMEM_LIMIT = 56 * 1024 * 1024
Q_SCALE = (QK_NOPE_DIM + QK_ROPE_DIM) ** -0.5 * math.log2(math.e)

F32 = jnp.float32
BF16 = jnp.bfloat16


def _dot(a, b):
    return jnp.dot(a, b, preferred_element_type=F32)


def _rms(x, g):
    return x * lax.rsqrt(jnp.mean(x * x, axis=-1, keepdims=True) + EPS) * g


def _normed_input(x, mods_ref, gain_ref, base):
    shift = mods_ref[0, base:base + 1, :]
    scale = mods_ref[0, base + 1:base + 2, :]
    return _rms(x, gain_ref[...]) * (1.0 + scale) + shift


def _resident(shape):
    nd = len(shape)
    return pl.BlockSpec(shape, lambda *_: (0,) * nd, pipeline_mode=pl.Buffered(1))


def _params(n_grid):
    return pltpu.CompilerParams(dimension_semantics=("parallel",) * n_grid,
                                vmem_limit_bytes=V7X_VMEM_LIMIT)


def _mods_kernel(c_ref, w_ref, b_ref, o_ref):
    c = c_ref[...]
    s = (c * jax.nn.sigmoid(c)).astype(BF16)
    o_ref[...] = _dot(s, w_ref[...].astype(BF16)) + b_ref[...]


def _mods(cond, mod_w, mod_b):
    n = mod_w.shape[1]
    tn = D_MODEL
    return pl.pallas_call(
        _mods_kernel,
        grid=(n // tn,),
        in_specs=[pl.BlockSpec((8, D_MODEL), lambda j: (0, 0)),
                  pl.BlockSpec((D_MODEL, tn), lambda j: (0, j)),
                  pl.BlockSpec((1, tn), lambda j: (0, j))],
        out_specs=pl.BlockSpec((8, tn), lambda j: (0, j)),
        out_shape=jax.ShapeDtypeStruct((8, n), F32),
        compiler_params=_params(1),
        name="adaln_mods",
    )(cond, mod_w, mod_b.reshape(1, n))


def _ffn_kernel(x_ref, mods_ref, gain_ref, wi_ref, wo_ref, fin_ref, o_ref, *,
                mod_base, final_norm):
    x = x_ref[0]
    h = _normed_input(x, mods_ref, gain_ref, mod_base).astype(BF16)
    g = _dot(h, wi_ref[:, :FFN_DIM])
    u = _dot(h, wi_ref[:, FFN_DIM:])
    a = (g * jax.nn.sigmoid(g) * u).astype(BF16)
    y = _dot(a, wo_ref[...])
    gate = mods_ref[0, mod_base + 2:mod_base + 3, :]
    out = x + 0.5 * gate * y
    if final_norm:
        out = _rms(out, fin_ref[...])
    o_ref[0] = out


def _ffn(x, mods, gain, wi, wo, fin, *, mod_base, final_norm, tm):
    nb, l, _ = x.shape
    tok = pl.BlockSpec((1, tm, D_MODEL), lambda b, i: (b, i, 0))
    return pl.pallas_call(
        functools.partial(_ffn_kernel, mod_base=mod_base, final_norm=final_norm),
        grid=(nb, l // tm),
        in_specs=[tok,
                  pl.BlockSpec((1, N_MOD, D_MODEL), lambda b, i: (b, 0, 0)),
                  _resident((1, D_MODEL)),
                  _resident(wi.shape), _resident(wo.shape),
                  _resident((1, D_MODEL))],
        out_specs=tok,
        out_shape=jax.ShapeDtypeStruct(x.shape, F32),
        compiler_params=_params(2),
        name="ffn_final" if final_norm else "ffn",
    )(x, mods, gain, wi, wo, fin)


def _store_kv(ckv_b, kr_b, wk_ref, wvup_ref, k_ref, vt_ref):
    k_nope = _dot(ckv_b, wk_ref[...])
    for hd in range(MLA_HEADS):
        k_ref[0, :, hd * QK_PAD:hd * QK_PAD + QK_NOPE_DIM] = (
            k_nope[:, hd * QK_NOPE_DIM:(hd + 1) * QK_NOPE_DIM].astype(BF16))
        k_ref[0, :, hd * QK_PAD + QK_NOPE_DIM:(hd + 1) * QK_PAD] = kr_b
    vals = _dot(ckv_b, wvup_ref[...])
    ones = jnp.ones((V_ROWS - V_HEAD_DIM, vals.shape[0]), BF16)
    for hd in range(MLA_HEADS):
        vt_ref[0, hd * V_ROWS:hd * V_ROWS + V_HEAD_DIM, :] = (
            vals[:, hd * V_HEAD_DIM:(hd + 1) * V_HEAD_DIM].T.astype(BF16))
        vt_ref[0, hd * V_ROWS + V_HEAD_DIM:(hd + 1) * V_ROWS, :] = ones


def _mixer_tokens(x_ref, mods_ref, gain_ref, wu_ref, wv_ref, wlat_ref, wga_ref, vnorm_ref,
                  ws_ref, bs_ref, qnorm_ref, wq_ref, kvnorm_ref, wk_ref, wvup_ref, wa_ref,
                  a_ref, qt_ref, k_ref, vt_ref, mixed_ref, *, rope_refs, latent_refs):
    tm = x_ref.shape[1]
    x = x_ref[0]
    h = _normed_input(x, mods_ref, gain_ref, 3).astype(BF16)
    u = _dot(h, wu_ref[...])
    v = _dot(h, wv_ref[...])
    lat = _dot(h, wlat_ref[...])
    ga = jax.nn.sigmoid(_dot(h, wga_ref[...]))

    vv = _rms(v, vnorm_ref[...]).astype(BF16)
    n_chunks = tm // CHUNK
    for g in range(GMLP_GROUPS):
        cols = slice(g * GMLP_GROUP_DIM, (g + 1) * GMLP_GROUP_DIM)
        blk = jnp.concatenate(
            [vv[c * CHUNK:(c + 1) * CHUNK, cols] for c in range(n_chunks)], axis=1)
        mix = _dot(ws_ref[g], blk)
        for c in range(n_chunks):
            mixed_ref[c * CHUNK:(c + 1) * CHUNK, cols] = (
                mix[:, c * CHUNK:(c + 1) * CHUNK] + bs_ref[:, cols])
    out_a = (u * mixed_ref[...]).astype(BF16)
    a_ref[0] = ga * _dot(out_a, wa_ref[...])

    q_lat = lat[:, 0:Q_LORA_RANK]
    ckv = _rms(lat[:, Q_LORA_RANK:Q_LORA_RANK + KV_LORA_RANK], kvnorm_ref[...])
    kr_off = Q_LORA_RANK + KV_LORA_RANK
    kr = lat[:, kr_off:kr_off + ROPE_PAD]
    if latent_refs is not None:
        ckv_ref, krope_ref = latent_refs
        ckv_ref[0] = ckv
        krope_ref[0] = kr[:, 0:QK_ROPE_DIM]

    qn = _rms(q_lat, qnorm_ref[...]).astype(BF16)
    q_all = _dot(qn, wq_ref[...])
    if rope_refs is not None:
        wqsw_ref, cos_ref, sin_ref = rope_refs
        cos = cos_ref[...]
        sin = sin_ref[...]
        q_sw = _dot(qn, wqsw_ref[...])
        kr = kr * cos + lat[:, kr_off + ROPE_PAD:kr_off + 2 * ROPE_PAD] * sin
    for hd in range(MLA_HEADS):
        nope = q_all[:, hd * QK_PAD:hd * QK_PAD + QK_NOPE_DIM]
        rot = q_all[:, hd * QK_PAD + QK_NOPE_DIM:(hd + 1) * QK_PAD]
        if rope_refs is not None:
            rot = rot * cos + q_sw[:, hd * ROPE_PAD:(hd + 1) * ROPE_PAD] * sin
        qt_ref[0, hd * QK_PAD:hd * QK_PAD + QK_NOPE_DIM, :] = (nope * Q_SCALE).T.astype(BF16)
        qt_ref[0, hd * QK_PAD + QK_NOPE_DIM:(hd + 1) * QK_PAD, :] = (rot * Q_SCALE).T.astype(BF16)
    _store_kv(ckv.astype(BF16), kr.astype(BF16), wk_ref, wvup_ref, k_ref, vt_ref)


def _mixer_ctx_kernel(x_ref, mods_ref, gain_ref, wu_ref, wv_ref, wlat_ref, wga_ref, vnorm_ref,
                      ws_ref, bs_ref, qnorm_ref, wq_ref, kvnorm_ref, wk_ref, wvup_ref, wa_ref,
                      a_ref, qt_ref, k_ref, vt_ref, ckv_ref, krope_ref, mixed_ref):
    _mixer_tokens(x_ref, mods_ref, gain_ref, wu_ref, wv_ref, wlat_ref, wga_ref, vnorm_ref,
                  ws_ref, bs_ref, qnorm_ref, wq_ref, kvnorm_ref, wk_ref, wvup_ref, wa_ref,
                  a_ref, qt_ref, k_ref, vt_ref, mixed_ref,
                  rope_refs=None, latent_refs=(ckv_ref, krope_ref))


def _mixer_lat_kernel(x_ref, mods_ref, gain_ref, wu_ref, wv_ref, wlat_ref, wga_ref, vnorm_ref,
                      ws_ref, bs_ref, qnorm_ref, wq_ref, kvnorm_ref, wk_ref, wvup_ref, wa_ref,
                      wqsw_ref, cos_ref, sin_ref, cckv_ref, ckr_ref,
                      a_ref, qt_ref, k_ref, vt_ref, mixed_ref):
    step = pl.program_id(1)

    @pl.when(step == 0)
    def _():
        _store_kv(cckv_ref[0].astype(BF16), ckr_ref[0].astype(BF16), wk_ref, wvup_ref, k_ref, vt_ref)

    @pl.when(step > 0)
    def _():
        _mixer_tokens(x_ref, mods_ref, gain_ref, wu_ref, wv_ref, wlat_ref, wga_ref, vnorm_ref,
                      ws_ref, bs_ref, qnorm_ref, wq_ref, kvnorm_ref, wk_ref, wvup_ref, wa_ref,
                      a_ref, qt_ref, k_ref, vt_ref, mixed_ref,
                      rope_refs=(wqsw_ref, cos_ref, sin_ref), latent_refs=None)


def _mixer_weights(w):
    return [w['norm_mix'], w['w_u'], w['w_v'], w['w_lat'], w['w_ga'], w['v_norm'], w['w_s'],
            w['b_s'], w['q_norm'], w['w_q'], w['kv_norm'], w['w_k'], w['w_vup'], w['w_a']]


def _mixer_ctx(x, mods, w, *, tm):
    nb, l, _ = x.shape
    tok = lambda width: pl.BlockSpec((1, tm, width), lambda b, i: (b, i, 0))
    tok_t = lambda height: pl.BlockSpec((1, height, tm), lambda b, i: (b, 0, i))
    weights = _mixer_weights(w)
    return pl.pallas_call(
        _mixer_ctx_kernel,
        grid=(nb, l // tm),
        in_specs=[tok(D_MODEL), pl.BlockSpec((1, N_MOD, D_MODEL), lambda b, i: (b, 0, 0))]
                 + [_resident(a.shape) for a in weights],
        out_specs=[tok(D_MODEL), tok_t(MLA_HEADS * QK_PAD), tok(MLA_HEADS * QK_PAD),
                   tok_t(MLA_HEADS * V_ROWS), tok(KV_LORA_RANK), tok(QK_ROPE_DIM)],
        out_shape=[jax.ShapeDtypeStruct((nb, l, D_MODEL), F32),
                   jax.ShapeDtypeStruct((nb, MLA_HEADS * QK_PAD, l), BF16),
                   jax.ShapeDtypeStruct((nb, l, MLA_HEADS * QK_PAD), BF16),
                   jax.ShapeDtypeStruct((nb, MLA_HEADS * V_ROWS, l), BF16),
                   jax.ShapeDtypeStruct((nb, l, KV_LORA_RANK), F32),
                   jax.ShapeDtypeStruct((nb, l, QK_ROPE_DIM), F32)],
        scratch_shapes=[pltpu.VMEM((tm, GMLP_DIM), F32)],
        compiler_params=_params(2),
        name="mixer_ctx",
    )(x, mods, *weights)


def _mixer_lat(x, mods, w, cos, sin, cache_ckv, cache_kr, *, tm):
    nb, l, _ = x.shape
    past = cache_ckv.shape[1]
    assert past == tm, "the cached context must fill exactly one key block"
    prev = lambda i: jnp.maximum(i - 1, 0)
    tok = pl.BlockSpec((1, tm, D_MODEL), lambda b, i: (b, prev(i), 0))
    table = pl.BlockSpec((tm, ROPE_PAD), lambda b, i: (prev(i), 0))
    weights = _mixer_weights(w) + [w['w_q_sw']]
    return pl.pallas_call(
        _mixer_lat_kernel,
        grid=(nb, 1 + l // tm),
        in_specs=[tok, pl.BlockSpec((1, N_MOD, D_MODEL), lambda b, i: (b, 0, 0))]
                 + [_resident(a.shape) for a in weights] + [table, table]
                 + [pl.BlockSpec((1, past, KV_LORA_RANK), lambda b, i: (b, 0, 0)),
                    pl.BlockSpec((1, past, ROPE_PAD), lambda b, i: (b, 0, 0))],
        out_specs=[tok,
                   pl.BlockSpec((1, MLA_HEADS * QK_PAD, tm), lambda b, i: (b, 0, prev(i))),
                   pl.BlockSpec((1, tm, MLA_HEADS * QK_PAD), lambda b, i: (b, i, 0)),
                   pl.BlockSpec((1, MLA_HEADS * V_ROWS, tm), lambda b, i: (b, 0, i))],
        out_shape=[jax.ShapeDtypeStruct((nb, l, D_MODEL), F32),
                   jax.ShapeDtypeStruct((nb, MLA_HEADS * QK_PAD, l), BF16),
                   jax.ShapeDtypeStruct((nb, past + l, MLA_HEADS * QK_PAD), BF16),
                   jax.ShapeDtypeStruct((nb, MLA_HEADS * V_ROWS, past + l), BF16)],
        scratch_shapes=[pltpu.VMEM((tm, GMLP_DIM), F32)],
        compiler_params=pltpu.CompilerParams(dimension_semantics=("parallel", "arbitrary"),
                                             vmem_limit_bytes=V7X_VMEM_LIMIT),
        name="mixer_lat",
    )(x, mods, *weights, cos, sin, cache_ckv, cache_kr)


Q_LANES = 256
SCORES_AHEAD = 2


def _attn_kernel(qt_ref, k_ref, vt_ref, o_ref, *, heads, tk, tk_first):
    seq_k = k_ref.shape[1]
    tq = qt_ref.shape[2]
    bounds = [0] + list(range(tk_first, seq_k + 1, tk))
    chunks = [slice(lo, hi) for lo, hi in zip(bounds[:-1], bounds[1:])]
    n_chunks = len(chunks)
    streams = [(j, q0) for j in range(heads) for q0 in range(0, tq, Q_LANES)]
    qts = [qt_ref[0, j * QK_PAD:(j + 1) * QK_PAD, q0:q0 + Q_LANES] for j, q0 in streams]

    def scores(i, c):
        j = streams[i][0]
        return _dot(k_ref[0, chunks[c], j * QK_PAD:(j + 1) * QK_PAD], qts[i])

    pending = [[scores(i, c) for c in range(min(SCORES_AHEAD, n_chunks))]
               for i in range(len(streams))]
    state = [None] * len(streams)
    for c in range(n_chunks):
        for i in range(len(streams)):
            if c + SCORES_AHEAD < n_chunks:
                pending[i].append(scores(i, c + SCORES_AHEAD))
        for i, (j, _) in enumerate(streams):
            s = pending[i].pop(0)
            vt = vt_ref[0, j * V_ROWS:(j + 1) * V_ROWS, chunks[c]]
            m_c = jnp.max(s, axis=0, keepdims=True)
            if c == 0:
                p = jnp.exp2(s - m_c)
                state[i] = (m_c, _dot(vt, p.astype(BF16)))
            else:
                m, acc = state[i]
                m_new = jnp.maximum(m, m_c)
                alpha = jnp.exp2(m - m_new)
                p = jnp.exp2(s - m_new)
                state[i] = (m_new, alpha * acc + _dot(vt, p.astype(BF16)))
    for i, (j, q0) in enumerate(streams):
        _, acc = state[i]
        ot = acc[:V_HEAD_DIM] * (1.0 / acc[V_HEAD_DIM:V_HEAD_DIM + 1])
        o_ref[0, q0:q0 + Q_LANES, j * V_HEAD_DIM:(j + 1) * V_HEAD_DIM] = ot.T.astype(BF16)


def _attention(qt, k, vt, *, n_seq, seq_q, seq_k, heads, tq, tk, tk_first):
    nb = qt.shape[0]
    lq_total = qt.shape[2]
    q_tiles = seq_q // tq
    hg = MLA_HEADS // heads
    return pl.pallas_call(
        functools.partial(_attn_kernel, heads=heads, tk=tk, tk_first=tk_first),
        grid=(nb, n_seq, hg, q_tiles),
        in_specs=[pl.BlockSpec((1, heads * QK_PAD, tq), lambda b, s, h, i: (b, h, s * q_tiles + i)),
                  pl.BlockSpec((1, seq_k, heads * QK_PAD), lambda b, s, h, i: (b, s, h)),
                  pl.BlockSpec((1, heads * V_ROWS, seq_k), lambda b, s, h, i: (b, h, s))],
        out_specs=pl.BlockSpec((1, tq, heads * V_HEAD_DIM), lambda b, s, h, i: (b, s * q_tiles + i, h)),
        out_shape=jax.ShapeDtypeStruct((nb, lq_total, MLA_HEADS * V_HEAD_DIM), BF16),
        compiler_params=_params(4),
        name="attention",
    )(qt, k, vt)


def _merge_kernel(x_ref, a_ref, ob_ref, mods_ref, gain_ref, wgb_ref, wb_ref, wo_ref, o_ref):
    x = x_ref[0]
    h = _normed_input(x, mods_ref, gain_ref, 3).astype(BF16)
    gb = jax.nn.sigmoid(_dot(h, wgb_ref[...]))
    merged = (a_ref[0] + gb * _dot(ob_ref[0], wb_ref[...])).astype(BF16)
    gate = mods_ref[0, 5:6, :]
    o_ref[0] = x + gate * _dot(merged, wo_ref[...])


def _merge(x, a_part, out_b, mods, gain, wgb, wb, wo, *, tm):
    nb, l, _ = x.shape
    tok = pl.BlockSpec((1, tm, D_MODEL), lambda b, i: (b, i, 0))
    return pl.pallas_call(
        _merge_kernel,
        grid=(nb, l // tm),
        in_specs=[tok, tok, tok,
                  pl.BlockSpec((1, N_MOD, D_MODEL), lambda b, i: (b, 0, 0)),
                  _resident((1, D_MODEL)),
                  _resident(wgb.shape), _resident(wb.shape), _resident(wo.shape)],
        out_specs=tok,
        out_shape=jax.ShapeDtypeStruct(x.shape, F32),
        compiler_params=_params(2),
        name="merge",
    )(x, a_part, out_b, mods, gain, wgb, wb, wo)


def _rope_tables(l):
    rows = l // GRID_W
    r = np.repeat(np.arange(rows, dtype=np.float32), GRID_W)
    col = np.tile(np.arange(GRID_W, dtype=np.float32), rows)
    half = QK_ROPE_DIM // 2
    inv = (1.0 / (np.float32(ROPE_BASE) ** (np.arange(0, half, 2, dtype=np.float32) / half))
           ).astype(np.float32)
    ang_r, ang_c = r[:, None] * inv, col[:, None] * inv
    pad = ROPE_PAD - QK_ROPE_DIM
    cos = np.concatenate([np.cos(ang_r), np.cos(ang_r), np.cos(ang_c), np.cos(ang_c),
                          np.ones((l, pad), np.float32)], axis=1)
    sin = np.concatenate([-np.sin(ang_r), np.sin(ang_r), -np.sin(ang_c), np.sin(ang_c),
                          np.zeros((l, pad), np.float32)], axis=1)
    return jnp.asarray(cos, F32), jnp.asarray(sin, F32)


def _layer_weights(l, norm_ffn1, ffn1_w_in, ffn1_w_out, norm_mix, w_in, gmlp_v_norm, gmlp_w_s,
                   gmlp_b_s, q_norm, w_q_up, kv_norm, w_kv_up, w_a_proj, w_b_proj, w_o,
                   norm_ffn2, ffn2_w_in, ffn2_w_out):
    bf = lambda a: a.astype(BF16)
    row = lambda a: a.reshape(1, -1)
    swap = np.arange(QK_ROPE_DIM) ^ (QK_ROPE_DIM // 4)
    win = w_in[l]
    o_v, o_q, o_kv, o_kr, o_ga = np.cumsum(
        [GMLP_DIM, GMLP_DIM, Q_LORA_RANK, KV_LORA_RANK, QK_ROPE_DIM]).tolist()
    o_gb = o_ga + D_MODEL
    kr_w = win[:, o_kr:o_ga]
    pad_kr = jnp.zeros((D_MODEL, ROPE_PAD - QK_ROPE_DIM), F32)
    w_lat = jnp.concatenate([win[:, o_q:o_kr], kr_w, pad_kr, kr_w[:, swap], pad_kr], axis=1)

    wq3 = w_q_up[l].reshape(Q_LORA_RANK, MLA_HEADS, QK_NOPE_DIM + QK_ROPE_DIM)
    pad_q = jnp.zeros((Q_LORA_RANK, MLA_HEADS, ROPE_PAD - QK_ROPE_DIM), F32)
    w_q = jnp.concatenate([wq3, pad_q], axis=2).reshape(Q_LORA_RANK, MLA_HEADS * QK_PAD)
    w_q_sw = jnp.concatenate([wq3[:, :, QK_NOPE_DIM:][:, :, swap], pad_q], axis=2).reshape(
        Q_LORA_RANK, MLA_HEADS * ROPE_PAD)
    wkv3 = w_kv_up[l].reshape(KV_LORA_RANK, MLA_HEADS, QK_NOPE_DIM + V_HEAD_DIM)
    return {
        'norm_ffn1': row(norm_ffn1[l]),
        'ffn1_i': bf(ffn1_w_in[l]),
        'ffn1_o': bf(ffn1_w_out[l]),
        'norm_mix': row(norm_mix[l]),
        'w_u': bf(win[:, :GMLP_DIM]), 'w_v': bf(win[:, o_v:o_q]), 'w_lat': bf(w_lat),
        'w_ga': bf(win[:, o_ga:o_gb]), 'w_gb': bf(win[:, o_gb:]),
        'v_norm': row(gmlp_v_norm[l]),
        'w_s': bf(gmlp_w_s[l]),
        'b_s': jnp.repeat(gmlp_b_s[l], GMLP_GROUP_DIM, axis=1),
        'q_norm': row(q_norm[l]), 'w_q': bf(w_q), 'w_q_sw': bf(w_q_sw),
        'kv_norm': row(kv_norm[l]),
        'w_k': bf(wkv3[:, :, :QK_NOPE_DIM].reshape(KV_LORA_RANK, -1)),
        'w_vup': bf(wkv3[:, :, QK_NOPE_DIM:].reshape(KV_LORA_RANK, -1)),
        'w_a': bf(w_a_proj[l]), 'w_b': bf(w_b_proj[l]), 'w_o': bf(w_o[l]),
        'norm_ffn2': row(norm_ffn2[l]),
        'ffn2_i': bf(ffn2_w_in[l]),
        'ffn2_o': bf(ffn2_w_out[l]),
    }


def _trunk_layer(x, mods, w, fin, mixer, *, n_seq, seq, seq_k, final_norm, tm, tq, tk, tk_first,
                 heads):
    x = _ffn(x, mods, w['norm_ffn1'], w['ffn1_i'], w['ffn1_o'], fin,
             mod_base=0, final_norm=False, tm=tm)
    a_part, qt, k, vt, *latents = mixer(x)
    out_b = _attention(qt, k, vt, n_seq=n_seq, seq_q=seq, seq_k=seq_k, heads=heads, tq=tq, tk=tk,
                       tk_first=tk_first)
    x = _merge(x, a_part, out_b, mods, w['norm_mix'], w['w_gb'], w['w_b'], w['w_o'], tm=tm)
    x = _ffn(x, mods, w['norm_ffn2'], w['ffn2_i'], w['ffn2_o'], fin,
             mod_base=6, final_norm=final_norm, tm=tm)
    return x, latents


def kernel(x_prompt, x_sample, c, cache_ckv, cache_krope, c_ctx, mod_w, mod_b, norm_ffn1, ffn1_w_in, ffn1_w_out, norm_mix, w_in, gmlp_v_norm, gmlp_w_s, gmlp_b_s, q_norm, w_q_up, kv_norm, w_kv_up, w_a_proj, w_b_proj, w_o, norm_ffn2, ffn2_w_in, ffn2_w_out, norm_final):
    batch, seq, _ = x_prompt.shape
    dec_batch, dec_seq, _ = x_sample.shape
    past = cache_ckv.shape[2]
    depth = mod_w.shape[0]
    fin = norm_final.reshape(1, D_MODEL)
    cos, sin = _rope_tables(dec_seq)
    cond = jnp.concatenate([c_ctx[None, :], c, jnp.zeros((8 - 1 - dec_batch, D_MODEL), F32)], axis=0)

    xp = x_prompt.reshape(1, batch * seq, D_MODEL)
    xs = x_sample
    ckv_list, krope_list = [], []
    for l in range(depth):
        w = _layer_weights(l, norm_ffn1, ffn1_w_in, ffn1_w_out, norm_mix, w_in, gmlp_v_norm,
                           gmlp_w_s, gmlp_b_s, q_norm, w_q_up, kv_norm, w_kv_up, w_a_proj,
                           w_b_proj, w_o, norm_ffn2, ffn2_w_in, ffn2_w_out)
        mods = _mods(cond, mod_w[l], mod_b[l]).reshape(8, N_MOD, D_MODEL)
        mods_ctx, mods_lat = mods[0:1], mods[1:1 + dec_batch]
        last = l == depth - 1
        xp, (ckv_l, krope_l) = _trunk_layer(
            xp, mods_ctx, w, fin, lambda x: _mixer_ctx(x, mods_ctx, w, tm=256),
            n_seq=batch, seq=seq, seq_k=seq, final_norm=last, tm=512, tq=seq, tk=seq, tk_first=seq,
            heads=MLA_HEADS)
        ckv_list.append(ckv_l.reshape(batch, seq, KV_LORA_RANK))
        krope_list.append(krope_l.reshape(batch, seq, QK_ROPE_DIM))
        kr_pad = jnp.pad(cache_krope[:, l], ((0, 0), (0, 0), (0, ROPE_PAD - QK_ROPE_DIM)))
        xs, _ = _trunk_layer(
            xs, mods_lat, w, fin,
            lambda x: _mixer_lat(x, mods_lat, w, cos, sin, cache_ckv[:, l], kr_pad, tm=past),
            n_seq=1, seq=dec_seq, seq_k=past + dec_seq, final_norm=last, tm=512, tq=2048, tk=512, tk_first=past,
            heads=1)
    y_prompt = xp.reshape(batch, seq, D_MODEL)
    new_ckv = jnp.stack(ckv_list, axis=1)
    new_krope = jnp.stack(krope_list, axis=1)
    return (y_prompt, xs, new_ckv, new_krope)
```

```python
import functools
import math

import numpy as np
import jax
import jax.numpy as jnp
from jax import lax
from jax.experimental import pallas as pl
from jax.experimental.pallas import tpu as pltpu

D_MODEL = 1024
GRID_W = 64
FFN_DIM = 2816
GMLP_GROUPS = 8
GMLP_GROUP_DIM = 128
GMLP_DIM = GMLP_GROUPS * GMLP_GROUP_DIM
CHUNK = 128
MLA_HEADS = 8
QK_NOPE_DIM = 128
QK_ROPE_DIM = 64
V_HEAD_DIM = 128
Q_LORA_RANK = 256
KV_LORA_RANK = 256
ROPE_BASE = 10000.0
N_MOD = 9
EPS = 1e-6

V_ROWS = 144
QK_PAD = 256
ROPE_PAD = QK_PAD - QK_NOPE_DIM
V7X_VMEM_LIMIT = 56 * 1024 * 1024
Q_SCALE = (QK_NOPE_DIM + QK_ROPE_DIM) ** -0.5 * math.log2(math.e)

F32 = jnp.float32
BF16 = jnp.bfloat16


def _dot(a, b):
    return jnp.dot(a, b, preferred_element_type=F32)


def _rms(x, g):
    return x * lax.rsqrt(jnp.mean(x * x, axis=-1, keepdims=True) + EPS) * g


def _sigmoid(x):
    return 0.5 * jnp.tanh(0.5 * x) + 0.5


def _normed_input(x, mods_ref, gain_ref, base):
    shift = mods_ref[0, base:base + 1, :]
    scale = mods_ref[0, base + 1:base + 2, :]
    return _rms(x, gain_ref[...]) * (1.0 + scale) + shift


def _resident(shape):
    nd = len(shape)
    return pl.BlockSpec(shape, lambda *_: (0,) * nd, pipeline_mode=pl.Buffered(1))


def _params(n_grid):
    return pltpu.CompilerParams(dimension_semantics=("parallel",) * n_grid,
                                vmem_limit_bytes=V7X_VMEM_LIMIT)


def _mods_kernel(c_ref, w_ref, b_ref, o_ref):
    c = c_ref[...]
    s = (c * jax.nn.sigmoid(c)).astype(BF16)
    o_ref[...] = _dot(s, w_ref[...].astype(BF16)) + b_ref[...]


def _mods(cond, mod_w, mod_b):
    n = mod_w.shape[1]
    tn = D_MODEL
    return pl.pallas_call(
        _mods_kernel,
        grid=(n // tn,),
        in_specs=[pl.BlockSpec((8, D_MODEL), lambda j: (0, 0)),
                  pl.BlockSpec((D_MODEL, tn), lambda j: (0, j)),
                  pl.BlockSpec((1, tn), lambda j: (0, j))],
        out_specs=pl.BlockSpec((8, tn), lambda j: (0, j)),
        out_shape=jax.ShapeDtypeStruct((8, n), F32),
        compiler_params=_params(1),
        name="adaln_mods",
    )(cond, mod_w, mod_b.reshape(1, n))


def _ffn_kernel(x_ref, mods_ref, gain_ref, wi_ref, wo_ref, fin_ref, o_ref, *,
                mod_base, final_norm):
    x = x_ref[0]
    h = _normed_input(x, mods_ref, gain_ref, mod_base).astype(BF16)
    g = _dot(h, wi_ref[:, :FFN_DIM])
    u = _dot(h, wi_ref[:, FFN_DIM:])
    a = (g * jax.nn.sigmoid(g) * u).astype(BF16)
    y = _dot(a, wo_ref[...])
    gate = mods_ref[0, mod_base + 2:mod_base + 3, :]
    out = x + 0.5 * gate * y
    if final_norm:
        out = _rms(out, fin_ref[...])
    o_ref[0] = out


def _ffn(x, mods, gain, wi, wo, fin, *, mod_base, final_norm, tm):
    nb, l, _ = x.shape
    tok = pl.BlockSpec((1, tm, D_MODEL), lambda b, i: (b, i, 0))
    return pl.pallas_call(
        functools.partial(_ffn_kernel, mod_base=mod_base, final_norm=final_norm),
        grid=(nb, l // tm),
        in_specs=[tok,
                  pl.BlockSpec((1, N_MOD, D_MODEL), lambda b, i: (b, 0, 0)),
                  _resident((1, D_MODEL)),
                  _resident(wi.shape), _resident(wo.shape),
                  _resident((1, D_MODEL))],
        out_specs=tok,
        out_shape=jax.ShapeDtypeStruct(x.shape, F32),
        compiler_params=_params(2),
        name="ffn_final" if final_norm else "ffn",
    )(x, mods, gain, wi, wo, fin)


def _store_kv(k_nope, vals, kr_b, k_ref, vt_ref):
    ones = jnp.ones((V_ROWS - V_HEAD_DIM, vals.shape[0]), BF16)
    for hd in range(MLA_HEADS):
        cols = slice(hd * V_HEAD_DIM, (hd + 1) * V_HEAD_DIM)
        k_ref[0, :, hd * QK_PAD:hd * QK_PAD + QK_NOPE_DIM] = k_nope[:, cols].astype(BF16)
        k_ref[0, :, hd * QK_PAD + QK_NOPE_DIM:(hd + 1) * QK_PAD] = kr_b
        vt_ref[0, hd * V_ROWS:hd * V_ROWS + V_HEAD_DIM, :] = vals[:, cols].T.astype(BF16)
        vt_ref[0, hd * V_ROWS + V_HEAD_DIM:(hd + 1) * V_ROWS, :] = ones


def _attend_tile(q_t, k_nope, vals, kr_b, ob_ref):
    ones = jnp.ones((V_ROWS - V_HEAD_DIM, vals.shape[0]), BF16)
    head_cols = [slice(hd * V_HEAD_DIM, (hd + 1) * V_HEAD_DIM) for hd in range(MLA_HEADS)]
    scores = [_dot(jnp.concatenate([k_nope[:, cols].astype(BF16), kr_b], axis=1), q_t[hd])
              for hd, cols in enumerate(head_cols)]
    for hd, cols in enumerate(head_cols):
        vt_h = jnp.concatenate([vals[:, cols].T.astype(BF16), ones], axis=0)
        s = scores[hd]
        p = jnp.exp2(s - jnp.max(s, axis=0, keepdims=True))
        acc = _dot(vt_h, p.astype(BF16))
        ot = acc[:V_HEAD_DIM] * (1.0 / acc[V_HEAD_DIM:V_HEAD_DIM + 1])
        ob_ref[0, :, cols] = ot.T.astype(BF16)


def _mixer_tokens(x_ref, mods_ref, gain_ref, wu_ref, wv_ref, wlat_ref, wga_ref, vnorm_ref,
                  ws_ref, bs_ref, qnorm_ref, wq_ref, kvnorm_ref, wk_ref, wvup_ref, wa_ref,
                  a_ref, mixed_ref, *, rope_refs, latent_refs, qkv_refs, ob_ref):
    tm = x_ref.shape[1]
    x = x_ref[0]
    h = _normed_input(x, mods_ref, gain_ref, 3).astype(BF16)
    lat = _dot(h, wlat_ref[...])
    v = _dot(h, wv_ref[...])
    u = _dot(h, wu_ref[...])
    ga_logit = _dot(h, wga_ref[...])

    q_lat = lat[:, 0:Q_LORA_RANK]
    ckv = _rms(lat[:, Q_LORA_RANK:Q_LORA_RANK + KV_LORA_RANK], kvnorm_ref[...])
    kr_off = Q_LORA_RANK + KV_LORA_RANK
    kr = lat[:, kr_off:kr_off + ROPE_PAD]
    if latent_refs is not None:
        ckv_ref, krope_ref = latent_refs
        ckv_ref[0] = ckv
        krope_ref[0] = kr[:, 0:QK_ROPE_DIM]
    qn = _rms(q_lat, qnorm_ref[...]).astype(BF16)
    ckv_b = ckv.astype(BF16)
    q_all = _dot(qn, wq_ref[...])
    if rope_refs is not None:
        wqsw_ref, cos_ref, sin_ref = rope_refs
        q_sw = _dot(qn, wqsw_ref[...])
    k_nope = _dot(ckv_b, wk_ref[...])
    vals = _dot(ckv_b, wvup_ref[...])

    vv = _rms(v, vnorm_ref[...]).astype(BF16)
    n_chunks = tm // CHUNK
    for g in range(GMLP_GROUPS):
        cols = slice(g * GMLP_GROUP_DIM, (g + 1) * GMLP_GROUP_DIM)
        blk = jnp.concatenate(
            [vv[c * CHUNK:(c + 1) * CHUNK, cols] for c in range(n_chunks)], axis=1)
        mix = _dot(ws_ref[g], blk)
        for c in range(n_chunks):
            mixed_ref[c * CHUNK:(c + 1) * CHUNK, cols] = (
                mix[:, c * CHUNK:(c + 1) * CHUNK] + bs_ref[:, cols])
    out_a = (u * mixed_ref[...]).astype(BF16)
    a_proj = _dot(out_a, wa_ref[...])

    if rope_refs is not None:
        cos = cos_ref[...]
        sin = sin_ref[...]
        kr = kr * cos + lat[:, kr_off + ROPE_PAD:kr_off + 2 * ROPE_PAD] * sin
    q_t = []
    for hd in range(MLA_HEADS):
        nope = q_all[:, hd * QK_PAD:hd * QK_PAD + QK_NOPE_DIM]
        rot = q_all[:, hd * QK_PAD + QK_NOPE_DIM:(hd + 1) * QK_PAD]
        if rope_refs is not None:
            rot = rot * cos + q_sw[:, hd * ROPE_PAD:(hd + 1) * ROPE_PAD] * sin
        nope_t = (nope * Q_SCALE).T.astype(BF16)
        rot_t = (rot * Q_SCALE).T.astype(BF16)
        if qkv_refs is not None:
            qkv_refs[0][0, hd * QK_PAD:hd * QK_PAD + QK_NOPE_DIM, :] = nope_t
            qkv_refs[0][0, hd * QK_PAD + QK_NOPE_DIM:(hd + 1) * QK_PAD, :] = rot_t
        else:
            q_t.append(jnp.concatenate([nope_t, rot_t], axis=0))
    if qkv_refs is not None:
        _store_kv(k_nope, vals, kr.astype(BF16), qkv_refs[1], qkv_refs[2])
    else:
        _attend_tile(q_t, k_nope, vals, kr.astype(BF16), ob_ref)
    a_ref[0] = _sigmoid(ga_logit) * a_proj


def _mixer_ctx_kernel(x_ref, mods_ref, gain_ref, wu_ref, wv_ref, wlat_ref, wga_ref, vnorm_ref,
                      ws_ref, bs_ref, qnorm_ref, wq_ref, kvnorm_ref, wk_ref, wvup_ref, wa_ref,
                      a_ref, ob_ref, ckv_ref, krope_ref, mixed_ref):
    _mixer_tokens(x_ref, mods_ref, gain_ref, wu_ref, wv_ref, wlat_ref, wga_ref, vnorm_ref,
                  ws_ref, bs_ref, qnorm_ref, wq_ref, kvnorm_ref, wk_ref, wvup_ref, wa_ref,
                  a_ref, mixed_ref, rope_refs=None, latent_refs=(ckv_ref, krope_ref),
                  qkv_refs=None, ob_ref=ob_ref)


def _mixer_lat_kernel(x_ref, mods_ref, gain_ref, wu_ref, wv_ref, wlat_ref, wga_ref, vnorm_ref,
                      ws_ref, bs_ref, qnorm_ref, wq_ref, kvnorm_ref, wk_ref, wvup_ref, wa_ref,
                      wqsw_ref, cos_ref, sin_ref, cckv_ref, ckr_ref,
                      a_ref, qt_ref, k_ref, vt_ref, mixed_ref):
    step = pl.program_id(1)

    @pl.when(step == 0)
    def _():
        ckv_b = cckv_ref[0].astype(BF16)
        _store_kv(_dot(ckv_b, wk_ref[...]), _dot(ckv_b, wvup_ref[...]), ckr_ref[0].astype(BF16),
                  k_ref, vt_ref)

    @pl.when(step > 0)
    def _():
        _mixer_tokens(x_ref, mods_ref, gain_ref, wu_ref, wv_ref, wlat_ref, wga_ref, vnorm_ref,
                      ws_ref, bs_ref, qnorm_ref, wq_ref, kvnorm_ref, wk_ref, wvup_ref, wa_ref,
                      a_ref, mixed_ref, rope_refs=(wqsw_ref, cos_ref, sin_ref), latent_refs=None,
                      qkv_refs=(qt_ref, k_ref, vt_ref), ob_ref=None)


def _mixer_weights(w):
    return [w['norm_mix'], w['w_u'], w['w_v'], w['w_lat'], w['w_ga'], w['v_norm'], w['w_s'],
            w['b_s'], w['q_norm'], w['w_q'], w['kv_norm'], w['w_k'], w['w_vup'], w['w_a']]


def _mixer_ctx(x, mods, w, *, tm):
    nb, l, _ = x.shape
    tok = lambda width: pl.BlockSpec((1, tm, width), lambda b, i: (b, i, 0))
    weights = _mixer_weights(w)
    return pl.pallas_call(
        _mixer_ctx_kernel,
        grid=(nb, l // tm),
        in_specs=[tok(D_MODEL), pl.BlockSpec((1, N_MOD, D_MODEL), lambda b, i: (b, 0, 0))]
                 + [_resident(a.shape) for a in weights],
        out_specs=[tok(D_MODEL), tok(MLA_HEADS * V_HEAD_DIM), tok(KV_LORA_RANK), tok(QK_ROPE_DIM)],
        out_shape=[jax.ShapeDtypeStruct((nb, l, D_MODEL), F32),
                   jax.ShapeDtypeStruct((nb, l, MLA_HEADS * V_HEAD_DIM), BF16),
                   jax.ShapeDtypeStruct((nb, l, KV_LORA_RANK), F32),
                   jax.ShapeDtypeStruct((nb, l, QK_ROPE_DIM), F32)],
        scratch_shapes=[pltpu.VMEM((tm, GMLP_DIM), F32)],
        compiler_params=_params(2),
        name="mixer_ctx",
    )(x, mods, *weights)


def _mixer_lat(x, mods, w, cos, sin, cache_ckv, cache_kr, *, tm):
    nb, l, _ = x.shape
    past = cache_ckv.shape[1]
    assert past == tm, "the cached context must fill exactly one key block"
    prev = lambda i: jnp.maximum(i - 1, 0)
    tok = pl.BlockSpec((1, tm, D_MODEL), lambda b, i: (b, prev(i), 0))
    table = pl.BlockSpec((tm, ROPE_PAD), lambda b, i: (prev(i), 0))
    weights = _mixer_weights(w) + [w['w_q_sw']]
    return pl.pallas_call(
        _mixer_lat_kernel,
        grid=(nb, 1 + l // tm),
        in_specs=[tok, pl.BlockSpec((1, N_MOD, D_MODEL), lambda b, i: (b, 0, 0))]
                 + [_resident(a.shape) for a in weights] + [table, table]
                 + [pl.BlockSpec((1, past, KV_LORA_RANK), lambda b, i: (b, 0, 0)),
                    pl.BlockSpec((1, past, ROPE_PAD), lambda b, i: (b, 0, 0))],
        out_specs=[tok,
                   pl.BlockSpec((1, MLA_HEADS * QK_PAD, tm), lambda b, i: (b, 0, prev(i))),
                   pl.BlockSpec((1, tm, MLA_HEADS * QK_PAD), lambda b, i: (b, i, 0)),
                   pl.BlockSpec((1, MLA_HEADS * V_ROWS, tm), lambda b, i: (b, 0, i))],
        out_shape=[jax.ShapeDtypeStruct((nb, l, D_MODEL), F32),
                   jax.ShapeDtypeStruct((nb, MLA_HEADS * QK_PAD, l), BF16),
                   jax.ShapeDtypeStruct((nb, past + l, MLA_HEADS * QK_PAD), BF16),
                   jax.ShapeDtypeStruct((nb, MLA_HEADS * V_ROWS, past + l), BF16)],
        scratch_shapes=[pltpu.VMEM((tm, GMLP_DIM), F32)],
        compiler_params=pltpu.CompilerParams(dimension_semantics=("parallel", "arbitrary"),
                                             vmem_limit_bytes=V7X_VMEM_LIMIT),
        name="mixer_lat",
    )(x, mods, *weights, cos, sin, cache_ckv, cache_kr)


Q_LANES = 256
SCORES_AHEAD = 2


def _attn_kernel(qt_ref, k_ref, vt_ref, o_ref, *, heads, tk, tk_first):
    seq_k = k_ref.shape[1]
    tq = qt_ref.shape[2]
    bounds = [0] + list(range(tk_first, seq_k + 1, tk))
    chunks = [slice(lo, hi) for lo, hi in zip(bounds[:-1], bounds[1:])]
    n_chunks = len(chunks)
    streams = [(j, q0) for j in range(heads) for q0 in range(0, tq, Q_LANES)]
    qts = [qt_ref[0, j * QK_PAD:(j + 1) * QK_PAD, q0:q0 + Q_LANES] for j, q0 in streams]

    def scores(i, c):
        j = streams[i][0]
        return _dot(k_ref[0, chunks[c], j * QK_PAD:(j + 1) * QK_PAD], qts[i])

    pending = [[scores(i, c) for c in range(min(SCORES_AHEAD, n_chunks))]
               for i in range(len(streams))]
    state = [None] * len(streams)
    for c in range(n_chunks):
        for i, (j, _) in enumerate(streams):
            if c + SCORES_AHEAD < n_chunks:
                pending[i].append(scores(i, c + SCORES_AHEAD))
            s = pending[i].pop(0)
            vt = vt_ref[0, j * V_ROWS:(j + 1) * V_ROWS, chunks[c]]
            m_c = jnp.max(s, axis=0, keepdims=True)
            if c == 0:
                p = jnp.exp2(s - m_c)
                state[i] = (m_c, _dot(vt, p.astype(BF16)))
            else:
                m, acc = state[i]
                m_new = jnp.maximum(m, m_c)
                alpha = jnp.exp2(m - m_new)
                p = jnp.exp2(s - m_new)
                state[i] = (m_new, alpha * acc + _dot(vt, p.astype(BF16)))
    for i, (j, q0) in enumerate(streams):
        _, acc = state[i]
        ot = acc[:V_HEAD_DIM] * (1.0 / acc[V_HEAD_DIM:V_HEAD_DIM + 1])
        o_ref[0, q0:q0 + Q_LANES, j * V_HEAD_DIM:(j + 1) * V_HEAD_DIM] = ot.T.astype(BF16)


def _attention(qt, k, vt, *, seq_q, seq_k, heads, tq, tk, tk_first):
    nb = qt.shape[0]
    return pl.pallas_call(
        functools.partial(_attn_kernel, heads=heads, tk=tk, tk_first=tk_first),
        grid=(nb, MLA_HEADS // heads, seq_q // tq),
        in_specs=[pl.BlockSpec((1, heads * QK_PAD, tq), lambda b, h, i: (b, h, i)),
                  pl.BlockSpec((1, seq_k, heads * QK_PAD), lambda b, h, i: (b, 0, h)),
                  pl.BlockSpec((1, heads * V_ROWS, seq_k), lambda b, h, i: (b, h, 0))],
        out_specs=pl.BlockSpec((1, tq, heads * V_HEAD_DIM), lambda b, h, i: (b, i, h)),
        out_shape=jax.ShapeDtypeStruct((nb, seq_q, MLA_HEADS * V_HEAD_DIM), BF16),
        compiler_params=_params(3),
        name="attention",
    )(qt, k, vt)


def _merge_kernel(x_ref, a_ref, ob_ref, mods_ref, gain_ref, wgb_ref, wb_ref, wo_ref, o_ref):
    x = x_ref[0]
    h = _normed_input(x, mods_ref, gain_ref, 3).astype(BF16)
    gb = _sigmoid(_dot(h, wgb_ref[...]))
    merged = (a_ref[0] + gb * _dot(ob_ref[0], wb_ref[...])).astype(BF16)
    gate = mods_ref[0, 5:6, :]
    o_ref[0] = x + gate * _dot(merged, wo_ref[...])


def _merge(x, a_part, out_b, mods, gain, wgb, wb, wo, *, tm):
    nb, l, _ = x.shape
    tok = pl.BlockSpec((1, tm, D_MODEL), lambda b, i: (b, i, 0))
    return pl.pallas_call(
        _merge_kernel,
        grid=(nb, l // tm),
        in_specs=[tok, tok, tok,
                  pl.BlockSpec((1, N_MOD, D_MODEL), lambda b, i: (b, 0, 0)),
                  _resident((1, D_MODEL)),
                  _resident(wgb.shape), _resident(wb.shape), _resident(wo.shape)],
        out_specs=tok,
        out_shape=jax.ShapeDtypeStruct(x.shape, F32),
        compiler_params=_params(2),
        name="merge",
    )(x, a_part, out_b, mods, gain, wgb, wb, wo)


def _rope_tables(l):
    rows = l // GRID_W
    r = np.repeat(np.arange(rows, dtype=np.float32), GRID_W)
    col = np.tile(np.arange(GRID_W, dtype=np.float32), rows)
    half = QK_ROPE_DIM // 2
    inv = (1.0 / (np.float32(ROPE_BASE) ** (np.arange(0, half, 2, dtype=np.float32) / half))
           ).astype(np.float32)
    ang_r, ang_c = r[:, None] * inv, col[:, None] * inv
    pad = ROPE_PAD - QK_ROPE_DIM
    cos = np.concatenate([np.cos(ang_r), np.cos(ang_r), np.cos(ang_c), np.cos(ang_c),
                          np.ones((l, pad), np.float32)], axis=1)
    sin = np.concatenate([-np.sin(ang_r), np.sin(ang_r), -np.sin(ang_c), np.sin(ang_c),
                          np.zeros((l, pad), np.float32)], axis=1)
    return jnp.asarray(cos, F32), jnp.asarray(sin, F32)


def _layer_weights(l, norm_ffn1, ffn1_w_in, ffn1_w_out, norm_mix, w_in, gmlp_v_norm, gmlp_w_s,
                   gmlp_b_s, q_norm, w_q_up, kv_norm, w_kv_up, w_a_proj, w_b_proj, w_o,
                   norm_ffn2, ffn2_w_in, ffn2_w_out):
    bf = lambda a: a.astype(BF16)
    row = lambda a: a.reshape(1, -1)
    swap = np.arange(QK_ROPE_DIM) ^ (QK_ROPE_DIM // 4)
    win = w_in[l]
    o_v, o_q, o_kv, o_kr, o_ga = np.cumsum(
        [GMLP_DIM, GMLP_DIM, Q_LORA_RANK, KV_LORA_RANK, QK_ROPE_DIM]).tolist()
    o_gb = o_ga + D_MODEL
    kr_w = win[:, o_kr:o_ga]
    pad_kr = jnp.zeros((D_MODEL, ROPE_PAD - QK_ROPE_DIM), F32)
    w_lat = jnp.concatenate([win[:, o_q:o_kr], kr_w, pad_kr, kr_w[:, swap], pad_kr], axis=1)

    wq3 = w_q_up[l].reshape(Q_LORA_RANK, MLA_HEADS, QK_NOPE_DIM + QK_ROPE_DIM)
    pad_q = jnp.zeros((Q_LORA_RANK, MLA_HEADS, ROPE_PAD - QK_ROPE_DIM), F32)
    w_q = jnp.concatenate([wq3, pad_q], axis=2).reshape(Q_LORA_RANK, MLA_HEADS * QK_PAD)
    w_q_sw = jnp.concatenate([wq3[:, :, QK_NOPE_DIM:][:, :, swap], pad_q], axis=2).reshape(
        Q_LORA_RANK, MLA_HEADS * ROPE_PAD)
    wkv3 = w_kv_up[l].reshape(KV_LORA_RANK, MLA_HEADS, QK_NOPE_DIM + V_HEAD_DIM)
    return {
        'norm_ffn1': row(norm_ffn1[l]),
        'ffn1_i': bf(ffn1_w_in[l]),
        'ffn1_o': bf(ffn1_w_out[l]),
        'norm_mix': row(norm_mix[l]),
        'w_u': bf(win[:, :GMLP_DIM]), 'w_v': bf(win[:, o_v:o_q]), 'w_lat': bf(w_lat),
        'w_ga': bf(win[:, o_ga:o_gb]), 'w_gb': bf(win[:, o_gb:]),
        'v_norm': row(gmlp_v_norm[l]),
        'w_s': bf(gmlp_w_s[l]),
        'b_s': jnp.repeat(gmlp_b_s[l], GMLP_GROUP_DIM, axis=1),
        'q_norm': row(q_norm[l]), 'w_q': bf(w_q), 'w_q_sw': bf(w_q_sw),
        'kv_norm': row(kv_norm[l]),
        'w_k': bf(wkv3[:, :, :QK_NOPE_DIM].reshape(KV_LORA_RANK, -1)),
        'w_vup': bf(wkv3[:, :, QK_NOPE_DIM:].reshape(KV_LORA_RANK, -1)),
        'w_a': bf(w_a_proj[l]), 'w_b': bf(w_b_proj[l]), 'w_o': bf(w_o[l]),
        'norm_ffn2': row(norm_ffn2[l]),
        'ffn2_i': bf(ffn2_w_in[l]),
        'ffn2_o': bf(ffn2_w_out[l]),
    }


def _trunk_layer(x, mods, w, fin, mix_attend, *, final_norm, tm):
    x = _ffn(x, mods, w['norm_ffn1'], w['ffn1_i'], w['ffn1_o'], fin,
             mod_base=0, final_norm=False, tm=tm)
    a_part, out_b, extras = mix_attend(x)
    x = _merge(x, a_part, out_b, mods, w['norm_mix'], w['w_gb'], w['w_b'], w['w_o'], tm=tm)
    x = _ffn(x, mods, w['norm_ffn2'], w['ffn2_i'], w['ffn2_o'], fin,
             mod_base=6, final_norm=final_norm, tm=tm)
    return x, extras


def kernel(x_prompt, x_sample, c, cache_ckv, cache_krope, c_ctx, mod_w, mod_b, norm_ffn1, ffn1_w_in, ffn1_w_out, norm_mix, w_in, gmlp_v_norm, gmlp_w_s, gmlp_b_s, q_norm, w_q_up, kv_norm, w_kv_up, w_a_proj, w_b_proj, w_o, norm_ffn2, ffn2_w_in, ffn2_w_out, norm_final):
    batch, seq, _ = x_prompt.shape
    dec_batch, dec_seq, _ = x_sample.shape
    past = cache_ckv.shape[2]
    depth = mod_w.shape[0]
    fin = norm_final.reshape(1, D_MODEL)
    cos, sin = _rope_tables(dec_seq)
    cond = jnp.concatenate([c_ctx[None, :], c, jnp.zeros((8 - 1 - dec_batch, D_MODEL), F32)], axis=0)

    xp = x_prompt.reshape(1, batch * seq, D_MODEL)
    xs = x_sample
    ckv_list, krope_list = [], []
    for l in range(depth):
        w = _layer_weights(l, norm_ffn1, ffn1_w_in, ffn1_w_out, norm_mix, w_in, gmlp_v_norm,
                           gmlp_w_s, gmlp_b_s, q_norm, w_q_up, kv_norm, w_kv_up, w_a_proj,
                           w_b_proj, w_o, norm_ffn2, ffn2_w_in, ffn2_w_out)
        mods = _mods(cond, mod_w[l], mod_b[l]).reshape(8, N_MOD, D_MODEL)
        mods_ctx, mods_lat = mods[0:1], mods[1:1 + dec_batch]
        last = l == depth - 1
        kr_pad = jnp.pad(cache_krope[:, l], ((0, 0), (0, 0), (0, ROPE_PAD - QK_ROPE_DIM)))

        def ctx_mix(x):
            a_part, out_b, ckv, krope = _mixer_ctx(x, mods_ctx, w, tm=seq)
            return a_part, out_b, (ckv, krope)

        def lat_mix(x):
            a_part, qt, k, vt = _mixer_lat(x, mods_lat, w, cos, sin, cache_ckv[:, l], kr_pad, tm=past)
            out_b = _attention(qt, k, vt, seq_q=dec_seq, seq_k=past + dec_seq, heads=1,
                               tq=2048, tk=512, tk_first=past)
            return a_part, out_b, ()

        xp, (ckv_l, krope_l) = _trunk_layer(xp, mods_ctx, w, fin, ctx_mix, final_norm=last, tm=512)
        ckv_list.append(ckv_l.reshape(batch, seq, KV_LORA_RANK))
        krope_list.append(krope_l.reshape(batch, seq, QK_ROPE_DIM))
        xs, _ = _trunk_layer(xs, mods_lat, w, fin, lat_mix, final_norm=last, tm=512)
    y_prompt = xp.reshape(batch, seq, D_MODEL)
    new_ckv = jnp.stack(ckv_list, axis=1)
    new_krope = jnp.stack(krope_list, axis=1)
    return (y_prompt, xs, new_ckv, new_krope)
```

```python
import functools
import math

import numpy as np
import jax
import jax.numpy as jnp
from jax import lax
from jax.experimental import pallas as pl
from jax.experimental.pallas import tpu as pltpu

D_MODEL = 1024
GRID_W = 64
FFN_DIM = 2816
GMLP_GROUPS = 8
GMLP_GROUP_DIM = 128
GMLP_DIM = GMLP_GROUPS * GMLP_GROUP_DIM
CHUNK = 128
MLA_HEADS = 8
QK_NOPE_DIM = 128
QK_ROPE_DIM = 64
V_HEAD_DIM = 128
Q_LORA_RANK = 256
KV_LORA_RANK = 256
ROPE_BASE = 10000.0
N_MOD = 9
EPS = 1e-6

V_ROWS = 144
QK_PAD = 256
ROPE_PAD = QK_PAD - QK_NOPE_DIM
V7X_VMEM_LIMIT = 56 * 1024 * 1024
Q_SCALE = (QK_NOPE_DIM + QK_ROPE_DIM) ** -0.5 * math.log2(math.e)

F32 = jnp.float32
BF16 = jnp.bfloat16


def _dot(a, b):
    return jnp.dot(a, b, preferred_element_type=F32)


def _rms(x, g):
    return x * lax.rsqrt(jnp.mean(x * x, axis=-1, keepdims=True) + EPS) * g


def _sigmoid(x):
    return 0.5 * jnp.tanh(0.5 * x) + 0.5


def _normed_input(x, mods_ref, gain_ref, base):
    shift = mods_ref[0, base:base + 1, :]
    scale = mods_ref[0, base + 1:base + 2, :]
    return _rms(x, gain_ref[...]) * (1.0 + scale) + shift


def _resident(shape):
    nd = len(shape)
    return pl.BlockSpec(shape, lambda *_: (0,) * nd, pipeline_mode=pl.Buffered(1))


def _params(n_grid):
    return pltpu.CompilerParams(dimension_semantics=("parallel",) * n_grid,
                                vmem_limit_bytes=V7X_VMEM_LIMIT)


def _mods_kernel(c_ref, w_ref, b_ref, o_ref):
    c = c_ref[...]
    s = (c * jax.nn.sigmoid(c)).astype(BF16)
    o_ref[...] = _dot(s, w_ref[...].astype(BF16)) + b_ref[...]


def _mods(cond, mod_w, mod_b):
    n = mod_w.shape[1]
    tn = D_MODEL
    return pl.pallas_call(
        _mods_kernel,
        grid=(n // tn,),
        in_specs=[pl.BlockSpec((8, D_MODEL), lambda j: (0, 0)),
                  pl.BlockSpec((D_MODEL, tn), lambda j: (0, j)),
                  pl.BlockSpec((1, tn), lambda j: (0, j))],
        out_specs=pl.BlockSpec((8, tn), lambda j: (0, j)),
        out_shape=jax.ShapeDtypeStruct((8, n), F32),
        compiler_params=_params(1),
        name="adaln_mods",
    )(cond, mod_w, mod_b.reshape(1, n))


def _ffn_kernel(x_ref, mods_ref, gain_ref, wi_ref, wo_ref, fin_ref, o_ref, *,
                mod_base, final_norm):
    x = x_ref[0]
    h = _normed_input(x, mods_ref, gain_ref, mod_base).astype(BF16)
    g = _dot(h, wi_ref[:, :FFN_DIM])
    u = _dot(h, wi_ref[:, FFN_DIM:])
    a = (g * jax.nn.sigmoid(g) * u).astype(BF16)
    y = _dot(a, wo_ref[...])
    gate = mods_ref[0, mod_base + 2:mod_base + 3, :]
    out = x + 0.5 * gate * y
    if final_norm:
        out = _rms(out, fin_ref[...])
    o_ref[0] = out


def _ffn(x, mods, gain, wi, wo, fin, *, mod_base, final_norm, tm):
    nb, l, _ = x.shape
    tok = pl.BlockSpec((1, tm, D_MODEL), lambda b, i: (b, i, 0))
    return pl.pallas_call(
        functools.partial(_ffn_kernel, mod_base=mod_base, final_norm=final_norm),
        grid=(nb, l // tm),
        in_specs=[tok,
                  pl.BlockSpec((1, N_MOD, D_MODEL), lambda b, i: (b, 0, 0)),
                  _resident((1, D_MODEL)),
                  _resident(wi.shape), _resident(wo.shape),
                  _resident((1, D_MODEL))],
        out_specs=tok,
        out_shape=jax.ShapeDtypeStruct(x.shape, F32),
        compiler_params=_params(2),
        name="ffn_final" if final_norm else "ffn",
    )(x, mods, gain, wi, wo, fin)


def _store_kv(k_nope, vals, kr_b, k_ref, vt_ref):
    ones = jnp.ones((V_ROWS - V_HEAD_DIM, vals.shape[0]), BF16)
    for hd in range(MLA_HEADS):
        cols = slice(hd * V_HEAD_DIM, (hd + 1) * V_HEAD_DIM)
        k_ref[0, :, hd * QK_PAD:hd * QK_PAD + QK_NOPE_DIM] = k_nope[:, cols].astype(BF16)
        k_ref[0, :, hd * QK_PAD + QK_NOPE_DIM:(hd + 1) * QK_PAD] = kr_b
        vt_ref[0, hd * V_ROWS:hd * V_ROWS + V_HEAD_DIM, :] = vals[:, cols].T.astype(BF16)
        vt_ref[0, hd * V_ROWS + V_HEAD_DIM:(hd + 1) * V_ROWS, :] = ones


def _attend_tile(q_t, k_nope, vals, kr_b, ob_ref):
    ones = jnp.ones((V_ROWS - V_HEAD_DIM, vals.shape[0]), BF16)
    head_cols = [slice(hd * V_HEAD_DIM, (hd + 1) * V_HEAD_DIM) for hd in range(MLA_HEADS)]
    scores = [_dot(jnp.concatenate([k_nope[:, cols].astype(BF16), kr_b], axis=1), q_t[hd])
              for hd, cols in enumerate(head_cols)]
    for hd, cols in enumerate(head_cols):
        vt_h = jnp.concatenate([vals[:, cols].T.astype(BF16), ones], axis=0)
        s = scores[hd]
        p = jnp.exp2(s - jnp.max(s, axis=0, keepdims=True))
        acc = _dot(vt_h, p.astype(BF16))
        ot = acc[:V_HEAD_DIM] * (1.0 / acc[V_HEAD_DIM:V_HEAD_DIM + 1])
        ob_ref[0, :, cols] = ot.T.astype(BF16)


def _mixer_tokens(x_ref, mods_ref, gain_ref, wu_ref, wv_ref, wlat_ref, wga_ref, vnorm_ref,
                  ws_ref, bs_ref, qnorm_ref, wq_ref, kvnorm_ref, wk_ref, wvup_ref, wa_ref,
                  a_ref, mixed_ref, *, rope_refs, latent_refs, qkv_refs, ob_ref):
    tm = x_ref.shape[1]
    x = x_ref[0]
    h = _normed_input(x, mods_ref, gain_ref, 3).astype(BF16)
    lat = _dot(h, wlat_ref[...])
    v = _dot(h, wv_ref[...])
    u = _dot(h, wu_ref[...])
    ga_logit = _dot(h, wga_ref[...])

    q_lat = lat[:, 0:Q_LORA_RANK]
    ckv = _rms(lat[:, Q_LORA_RANK:Q_LORA_RANK + KV_LORA_RANK], kvnorm_ref[...])
    kr_off = Q_LORA_RANK + KV_LORA_RANK
    kr = lat[:, kr_off:kr_off + ROPE_PAD]
    if latent_refs is not None:
        ckv_ref, krope_ref = latent_refs
        ckv_ref[0] = ckv
        krope_ref[0] = kr[:, 0:QK_ROPE_DIM]
    qn = _rms(q_lat, qnorm_ref[...]).astype(BF16)
    ckv_b = ckv.astype(BF16)
    q_all = _dot(qn, wq_ref[...])
    if rope_refs is not None:
        wqsw_ref, cos_ref, sin_ref = rope_refs
        q_sw = _dot(qn, wqsw_ref[...])
    k_nope = _dot(ckv_b, wk_ref[...])
    vals = _dot(ckv_b, wvup_ref[...])

    vv = _rms(v, vnorm_ref[...]).astype(BF16)
    n_chunks = tm // CHUNK
    for g in range(GMLP_GROUPS):
        cols = slice(g * GMLP_GROUP_DIM, (g + 1) * GMLP_GROUP_DIM)
        blk = jnp.concatenate(
            [vv[c * CHUNK:(c + 1) * CHUNK, cols] for c in range(n_chunks)], axis=1)
        mix = _dot(ws_ref[g], blk)
        for c in range(n_chunks):
            mixed_ref[c * CHUNK:(c + 1) * CHUNK, cols] = (
                mix[:, c * CHUNK:(c + 1) * CHUNK] + bs_ref[:, cols])
    out_a = (u * mixed_ref[...]).astype(BF16)
    a_proj = _dot(out_a, wa_ref[...])

    if rope_refs is not None:
        cos = cos_ref[...]
        sin = sin_ref[...]
        kr = kr * cos + lat[:, kr_off + ROPE_PAD:kr_off + 2 * ROPE_PAD] * sin
    q_t = []
    for hd in range(MLA_HEADS):
        nope = q_all[:, hd * QK_PAD:hd * QK_PAD + QK_NOPE_DIM]
        rot = q_all[:, hd * QK_PAD + QK_NOPE_DIM:(hd + 1) * QK_PAD]
        if rope_refs is not None:
            rot = rot * cos + q_sw[:, hd * ROPE_PAD:(hd + 1) * ROPE_PAD] * sin
        nope_t = (nope * Q_SCALE).T.astype(BF16)
        rot_t = (rot * Q_SCALE).T.astype(BF16)
        if qkv_refs is not None:
            qkv_refs[0][0, hd * QK_PAD:hd * QK_PAD + QK_NOPE_DIM, :] = nope_t
            qkv_refs[0][0, hd * QK_PAD + QK_NOPE_DIM:(hd + 1) * QK_PAD, :] = rot_t
        else:
            q_t.append(jnp.concatenate([nope_t, rot_t], axis=0))
    if qkv_refs is not None:
        _store_kv(k_nope, vals, kr.astype(BF16), qkv_refs[1], qkv_refs[2])
    else:
        _attend_tile(q_t, k_nope, vals, kr.astype(BF16), ob_ref)
    a_ref[0] = _sigmoid(ga_logit) * a_proj


def _mixer_ctx_kernel(x_ref, mods_ref, gain_ref, wu_ref, wv_ref, wlat_ref, wga_ref, vnorm_ref,
                      ws_ref, bs_ref, qnorm_ref, wq_ref, kvnorm_ref, wk_ref, wvup_ref, wa_ref,
                      a_ref, ob_ref, ckv_ref, krope_ref, mixed_ref):
    _mixer_tokens(x_ref, mods_ref, gain_ref, wu_ref, wv_ref, wlat_ref, wga_ref, vnorm_ref,
                  ws_ref, bs_ref, qnorm_ref, wq_ref, kvnorm_ref, wk_ref, wvup_ref, wa_ref,
                  a_ref, mixed_ref, rope_refs=None, latent_refs=(ckv_ref, krope_ref),
                  qkv_refs=None, ob_ref=ob_ref)


def _mixer_lat_kernel(x_ref, mods_ref, gain_ref, wu_ref, wv_ref, wlat_ref, wga_ref, vnorm_ref,
                      ws_ref, bs_ref, qnorm_ref, wq_ref, kvnorm_ref, wk_ref, wvup_ref, wa_ref,
                      wqsw_ref, cos_ref, sin_ref, cckv_ref, ckr_ref,
                      a_ref, qt_ref, k_ref, vt_ref, mixed_ref):
    step = pl.program_id(1)

    @pl.when(step == 0)
    def _():
        ckv_b = cckv_ref[0].astype(BF16)
        _store_kv(_dot(ckv_b, wk_ref[...]), _dot(ckv_b, wvup_ref[...]), ckr_ref[0].astype(BF16),
                  k_ref, vt_ref.at[0])

    @pl.when(step > 0)
    def _():
        _mixer_tokens(x_ref, mods_ref, gain_ref, wu_ref, wv_ref, wlat_ref, wga_ref, vnorm_ref,
                      ws_ref, bs_ref, qnorm_ref, wq_ref, kvnorm_ref, wk_ref, wvup_ref, wa_ref,
                      a_ref, mixed_ref, rope_refs=(wqsw_ref, cos_ref, sin_ref), latent_refs=None,
                      qkv_refs=(qt_ref.at[0], k_ref, vt_ref.at[0]), ob_ref=None)


def _mixer_weights(w):
    return [w['norm_mix'], w['w_u'], w['w_v'], w['w_lat'], w['w_ga'], w['v_norm'], w['w_s'],
            w['b_s'], w['q_norm'], w['w_q'], w['kv_norm'], w['w_k'], w['w_vup'], w['w_a']]


def _mixer_ctx(x, mods, w, *, tm):
    nb, l, _ = x.shape
    tok = lambda width: pl.BlockSpec((1, tm, width), lambda b, i: (b, i, 0))
    weights = _mixer_weights(w)
    return pl.pallas_call(
        _mixer_ctx_kernel,
        grid=(nb, l // tm),
        in_specs=[tok(D_MODEL), pl.BlockSpec((1, N_MOD, D_MODEL), lambda b, i: (b, 0, 0))]
                 + [_resident(a.shape) for a in weights],
        out_specs=[tok(D_MODEL), tok(MLA_HEADS * V_HEAD_DIM), tok(KV_LORA_RANK), tok(QK_ROPE_DIM)],
        out_shape=[jax.ShapeDtypeStruct((nb, l, D_MODEL), F32),
                   jax.ShapeDtypeStruct((nb, l, MLA_HEADS * V_HEAD_DIM), BF16),
                   jax.ShapeDtypeStruct((nb, l, KV_LORA_RANK), F32),
                   jax.ShapeDtypeStruct((nb, l, QK_ROPE_DIM), F32)],
        scratch_shapes=[pltpu.VMEM((tm, GMLP_DIM), F32)],
        compiler_params=_params(2),
        name="mixer_ctx",
    )(x, mods, *weights)


def _mixer_lat(x, mods, w, cos, sin, cache_ckv, cache_kr, *, tm):
    nb, l, _ = x.shape
    past = cache_ckv.shape[1]
    assert past == tm, "the cached context must fill exactly one key block"
    prev = lambda i: jnp.maximum(i - 1, 0)
    tok = pl.BlockSpec((1, tm, D_MODEL), lambda b, i: (b, prev(i), 0))
    table = pl.BlockSpec((tm, ROPE_PAD), lambda b, i: (prev(i), 0))
    weights = _mixer_weights(w) + [w['w_q_sw']]
    return pl.pallas_call(
        _mixer_lat_kernel,
        grid=(nb, 1 + l // tm),
        in_specs=[tok, pl.BlockSpec((1, N_MOD, D_MODEL), lambda b, i: (b, 0, 0))]
                 + [_resident(a.shape) for a in weights] + [table, table]
                 + [pl.BlockSpec((1, past, KV_LORA_RANK), lambda b, i: (b, 0, 0)),
                    pl.BlockSpec((1, past, ROPE_PAD), lambda b, i: (b, 0, 0))],
        out_specs=[tok,
                   pl.BlockSpec((1, 1, MLA_HEADS * QK_PAD, tm), lambda b, i: (b, prev(i), 0, 0)),
                   pl.BlockSpec((1, tm, MLA_HEADS * QK_PAD), lambda b, i: (b, i, 0)),
                   pl.BlockSpec((1, 1, MLA_HEADS * V_ROWS, tm), lambda b, i: (b, i, 0, 0))],
        out_shape=[jax.ShapeDtypeStruct((nb, l, D_MODEL), F32),
                   jax.ShapeDtypeStruct((nb, l // tm, MLA_HEADS * QK_PAD, tm), BF16),
                   jax.ShapeDtypeStruct((nb, past + l, MLA_HEADS * QK_PAD), BF16),
                   jax.ShapeDtypeStruct((nb, (past + l) // tm, MLA_HEADS * V_ROWS, tm), BF16)],
        scratch_shapes=[pltpu.VMEM((tm, GMLP_DIM), F32)],
        compiler_params=pltpu.CompilerParams(dimension_semantics=("parallel", "arbitrary"),
                                             vmem_limit_bytes=V7X_VMEM_LIMIT),
        name="mixer_lat",
    )(x, mods, *weights, cos, sin, cache_ckv, cache_kr)


Q_LANES = 256
SCORES_AHEAD = 2


def _attn_kernel(qt_ref, k_ref, vt_ref, o_ref, *, heads, tk, tk_first):
    seq_k = k_ref.shape[1]
    bounds = [0] + list(range(tk_first, seq_k + 1, tk))
    chunks = [slice(lo, hi) for lo, hi in zip(bounds[:-1], bounds[1:])]
    n_chunks = len(chunks)
    streams = [(j, t) for j in range(heads) for t in range(qt_ref.shape[1])]
    qts = [qt_ref[0, t, j * QK_PAD:(j + 1) * QK_PAD, :] for j, t in streams]

    def values(j, c, p):
        lo, hi = chunks[c].start, chunks[c].stop
        acc = None
        for t in range(lo // Q_LANES, hi // Q_LANES):
            part = _dot(vt_ref[0, t, j * V_ROWS:(j + 1) * V_ROWS, :],
                        p[t * Q_LANES - lo:(t + 1) * Q_LANES - lo])
            acc = part if acc is None else acc + part
        return acc

    def scores(i, c):
        j = streams[i][0]
        return _dot(k_ref[0, chunks[c], j * QK_PAD:(j + 1) * QK_PAD], qts[i])

    pending = [[scores(i, c) for c in range(min(SCORES_AHEAD, n_chunks))]
               for i in range(len(streams))]
    state = [None] * len(streams)
    for c in range(n_chunks):
        for i, (j, _) in enumerate(streams):
            if c + SCORES_AHEAD < n_chunks:
                pending[i].append(scores(i, c + SCORES_AHEAD))
            s = pending[i].pop(0)
            m_c = jnp.max(s, axis=0, keepdims=True)
            if c == 0:
                p = jnp.exp2(s - m_c)
                state[i] = (m_c, values(j, c, p.astype(BF16)))
            else:
                m, acc = state[i]
                m_new = jnp.maximum(m, m_c)
                alpha = jnp.exp2(m - m_new)
                p = jnp.exp2(s - m_new)
                state[i] = (m_new, alpha * acc + values(j, c, p.astype(BF16)))
    for i, (j, t) in enumerate(streams):
        q0 = t * Q_LANES
        _, acc = state[i]
        ot = acc[:V_HEAD_DIM] * (1.0 / acc[V_HEAD_DIM:V_HEAD_DIM + 1])
        o_ref[0, q0:q0 + Q_LANES, j * V_HEAD_DIM:(j + 1) * V_HEAD_DIM] = ot.T.astype(BF16)


def _attention(qt, k, vt, *, seq_q, seq_k, heads, tq, tk, tk_first):
    nb = qt.shape[0]
    return pl.pallas_call(
        functools.partial(_attn_kernel, heads=heads, tk=tk, tk_first=tk_first),
        grid=(nb, MLA_HEADS // heads, seq_q // tq),
        in_specs=[pl.BlockSpec((1, tq // Q_LANES, heads * QK_PAD, Q_LANES),
                               lambda b, h, i: (b, i, h, 0)),
                  pl.BlockSpec((1, seq_k, heads * QK_PAD), lambda b, h, i: (b, 0, h)),
                  pl.BlockSpec((1, seq_k // Q_LANES, heads * V_ROWS, Q_LANES),
                               lambda b, h, i: (b, 0, h, 0))],
        out_specs=pl.BlockSpec((1, tq, heads * V_HEAD_DIM), lambda b, h, i: (b, i, h)),
        out_shape=jax.ShapeDtypeStruct((nb, seq_q, MLA_HEADS * V_HEAD_DIM), BF16),
        compiler_params=_params(3),
        name="attention",
    )(qt, k, vt)


def _merge_kernel(x_ref, a_ref, ob_ref, mods_ref, gain_ref, wgb_ref, wb_ref, wo_ref, o_ref):
    x = x_ref[0]
    h = _normed_input(x, mods_ref, gain_ref, 3).astype(BF16)
    gb = _sigmoid(_dot(h, wgb_ref[...]))
    merged = (a_ref[0] + gb * _dot(ob_ref[0], wb_ref[...])).astype(BF16)
    gate = mods_ref[0, 5:6, :]
    o_ref[0] = x + gate * _dot(merged, wo_ref[...])


def _merge(x, a_part, out_b, mods, gain, wgb, wb, wo, *, tm):
    nb, l, _ = x.shape
    tok = pl.BlockSpec((1, tm, D_MODEL), lambda b, i: (b, i, 0))
    return pl.pallas_call(
        _merge_kernel,
        grid=(nb, l // tm),
        in_specs=[tok, tok, tok,
                  pl.BlockSpec((1, N_MOD, D_MODEL), lambda b, i: (b, 0, 0)),
                  _resident((1, D_MODEL)),
                  _resident(wgb.shape), _resident(wb.shape), _resident(wo.shape)],
        out_specs=tok,
        out_shape=jax.ShapeDtypeStruct(x.shape, F32),
        compiler_params=_params(2),
        name="merge",
    )(x, a_part, out_b, mods, gain, wgb, wb, wo)


def _rope_tables(l):
    rows = l // GRID_W
    r = np.repeat(np.arange(rows, dtype=np.float32), GRID_W)
    col = np.tile(np.arange(GRID_W, dtype=np.float32), rows)
    half = QK_ROPE_DIM // 2
    inv = (1.0 / (np.float32(ROPE_BASE) ** (np.arange(0, half, 2, dtype=np.float32) / half))
           ).astype(np.float32)
    ang_r, ang_c = r[:, None] * inv, col[:, None] * inv
    pad = ROPE_PAD - QK_ROPE_DIM
    cos = np.concatenate([np.cos(ang_r), np.cos(ang_r), np.cos(ang_c), np.cos(ang_c),
                          np.ones((l, pad), np.float32)], axis=1)
    sin = np.concatenate([-np.sin(ang_r), np.sin(ang_r), -np.sin(ang_c), np.sin(ang_c),
                          np.zeros((l, pad), np.float32)], axis=1)
    return jnp.asarray(cos, F32), jnp.asarray(sin, F32)


def _layer_weights(l, norm_ffn1, ffn1_w_in, ffn1_w_out, norm_mix, w_in, gmlp_v_norm, gmlp_w_s,
                   gmlp_b_s, q_norm, w_q_up, kv_norm, w_kv_up, w_a_proj, w_b_proj, w_o,
                   norm_ffn2, ffn2_w_in, ffn2_w_out):
    bf = lambda a: a.astype(BF16)
    row = lambda a: a.reshape(1, -1)
    swap = np.arange(QK_ROPE_DIM) ^ (QK_ROPE_DIM // 4)
    win = w_in[l]
    o_v, o_q, o_kv, o_kr, o_ga = np.cumsum(
        [GMLP_DIM, GMLP_DIM, Q_LORA_RANK, KV_LORA_RANK, QK_ROPE_DIM]).tolist()
    o_gb = o_ga + D_MODEL
    kr_w = win[:, o_kr:o_ga]
    pad_kr = jnp.zeros((D_MODEL, ROPE_PAD - QK_ROPE_DIM), F32)
    w_lat = jnp.concatenate([win[:, o_q:o_kr], kr_w, pad_kr, kr_w[:, swap], pad_kr], axis=1)

    wq3 = w_q_up[l].reshape(Q_LORA_RANK, MLA_HEADS, QK_NOPE_DIM + QK_ROPE_DIM)
    pad_q = jnp.zeros((Q_LORA_RANK, MLA_HEADS, ROPE_PAD - QK_ROPE_DIM), F32)
    w_q = jnp.concatenate([wq3, pad_q], axis=2).reshape(Q_LORA_RANK, MLA_HEADS * QK_PAD)
    w_q_sw = jnp.concatenate([wq3[:, :, QK_NOPE_DIM:][:, :, swap], pad_q], axis=2).reshape(
        Q_LORA_RANK, MLA_HEADS * ROPE_PAD)
    wkv3 = w_kv_up[l].reshape(KV_LORA_RANK, MLA_HEADS, QK_NOPE_DIM + V_HEAD_DIM)
    return {
        'norm_ffn1': row(norm_ffn1[l]),
        'ffn1_i': bf(ffn1_w_in[l]),
        'ffn1_o': bf(ffn1_w_out[l]),
        'norm_mix': row(norm_mix[l]),
        'w_u': bf(win[:, :GMLP_DIM]), 'w_v': bf(win[:, o_v:o_q]), 'w_lat': bf(w_lat),
        'w_ga': bf(win[:, o_ga:o_gb]), 'w_gb': bf(win[:, o_gb:]),
        'v_norm': row(gmlp_v_norm[l]),
        'w_s': bf(gmlp_w_s[l]),
        'b_s': jnp.repeat(gmlp_b_s[l], GMLP_GROUP_DIM, axis=1),
        'q_norm': row(q_norm[l]), 'w_q': bf(w_q), 'w_q_sw': bf(w_q_sw),
        'kv_norm': row(kv_norm[l]),
        'w_k': bf(wkv3[:, :, :QK_NOPE_DIM].reshape(KV_LORA_RANK, -1)),
        'w_vup': bf(wkv3[:, :, QK_NOPE_DIM:].reshape(KV_LORA_RANK, -1)),
        'w_a': bf(w_a_proj[l]), 'w_b': bf(w_b_proj[l]), 'w_o': bf(w_o[l]),
        'norm_ffn2': row(norm_ffn2[l]),
        'ffn2_i': bf(ffn2_w_in[l]),
        'ffn2_o': bf(ffn2_w_out[l]),
    }


def _trunk_layer(x, mods, w, fin, mix_attend, *, final_norm, tm):
    x = _ffn(x, mods, w['norm_ffn1'], w['ffn1_i'], w['ffn1_o'], fin,
             mod_base=0, final_norm=False, tm=tm)
    a_part, out_b, extras = mix_attend(x)
    x = _merge(x, a_part, out_b, mods, w['norm_mix'], w['w_gb'], w['w_b'], w['w_o'], tm=tm)
    x = _ffn(x, mods, w['norm_ffn2'], w['ffn2_i'], w['ffn2_o'], fin,
             mod_base=6, final_norm=final_norm, tm=tm)
    return x, extras


def kernel(x_prompt, x_sample, c, cache_ckv, cache_krope, c_ctx, mod_w, mod_b, norm_ffn1, ffn1_w_in, ffn1_w_out, norm_mix, w_in, gmlp_v_norm, gmlp_w_s, gmlp_b_s, q_norm, w_q_up, kv_norm, w_kv_up, w_a_proj, w_b_proj, w_o, norm_ffn2, ffn2_w_in, ffn2_w_out, norm_final):
    batch, seq, _ = x_prompt.shape
    dec_batch, dec_seq, _ = x_sample.shape
    past = cache_ckv.shape[2]
    depth = mod_w.shape[0]
    fin = norm_final.reshape(1, D_MODEL)
    cos, sin = _rope_tables(dec_seq)
    cond = jnp.concatenate([c_ctx[None, :], c, jnp.zeros((8 - 1 - dec_batch, D_MODEL), F32)], axis=0)

    xp = x_prompt.reshape(1, batch * seq, D_MODEL)
    xs = x_sample
    ckv_list, krope_list = [], []
    for l in range(depth):
        w = _layer_weights(l, norm_ffn1, ffn1_w_in, ffn1_w_out, norm_mix, w_in, gmlp_v_norm,
                           gmlp_w_s, gmlp_b_s, q_norm, w_q_up, kv_norm, w_kv_up, w_a_proj,
                           w_b_proj, w_o, norm_ffn2, ffn2_w_in, ffn2_w_out)
        mods = _mods(cond, mod_w[l], mod_b[l]).reshape(8, N_MOD, D_MODEL)
        mods_ctx, mods_lat = mods[0:1], mods[1:1 + dec_batch]
        last = l == depth - 1
        kr_pad = jnp.pad(cache_krope[:, l], ((0, 0), (0, 0), (0, ROPE_PAD - QK_ROPE_DIM)))

        def ctx_mix(x):
            a_part, out_b, ckv, krope = _mixer_ctx(x, mods_ctx, w, tm=seq)
            return a_part, out_b, (ckv, krope)

        def lat_mix(x):
            a_part, qt, k, vt = _mixer_lat(x, mods_lat, w, cos, sin, cache_ckv[:, l], kr_pad, tm=past)
            out_b = _attention(qt, k, vt, seq_q=dec_seq, seq_k=past + dec_seq, heads=1,
                               tq=2048, tk=512, tk_first=past)
            return a_part, out_b, ()

        xp, (ckv_l, krope_l) = _trunk_layer(xp, mods_ctx, w, fin, ctx_mix, final_norm=last, tm=512)
        ckv_list.append(ckv_l.reshape(batch, seq, KV_LORA_RANK))
        krope_list.append(krope_l.reshape(batch, seq, QK_ROPE_DIM))
        xs, _ = _trunk_layer(xs, mods_lat, w, fin, lat_mix, final_norm=last, tm=512)
    y_prompt = xp.reshape(batch, seq, D_MODEL)
    new_ckv = jnp.stack(ckv_list, axis=1)
    new_krope = jnp.stack(krope_list, axis=1)
    return (y_prompt, xs, new_ckv, new_krope)
```

```python
import functools
import math

import numpy as np
import jax
import jax.numpy as jnp
from jax import lax
from jax.experimental import pallas as pl
from jax.experimental.pallas import tpu as pltpu

D_MODEL = 1024
GRID_W = 64
FFN_DIM = 2816
GMLP_GROUPS = 8
GMLP_GROUP_DIM = 128
GMLP_DIM = GMLP_GROUPS * GMLP_GROUP_DIM
CHUNK = 128
MLA_HEADS = 8
QK_NOPE_DIM = 128
QK_ROPE_DIM = 64
V_HEAD_DIM = 128
Q_LORA_RANK = 256
KV_LORA_RANK = 256
ROPE_BASE = 10000.0
N_MOD = 9
EPS = 1e-6

V_ROWS = 144
QK_PAD = 256
ROPE_PAD = QK_PAD - QK_NOPE_DIM
V7X_VMEM_LIMIT = 56 * 1024 * 1024
Q_SCALE = (QK_NOPE_DIM + QK_ROPE_DIM) ** -0.5 * math.log2(math.e)

F32 = jnp.float32
BF16 = jnp.bfloat16


def _dot(a, b):
    return jnp.dot(a, b, preferred_element_type=F32)


def _rms(x, g):
    return x * lax.rsqrt(jnp.mean(x * x, axis=-1, keepdims=True) + EPS) * g


def _sigmoid(x):
    return 0.5 * jnp.tanh(0.5 * x) + 0.5


def _normed_input(x, mods_ref, gain_ref, base):
    shift = mods_ref[0, base:base + 1, :]
    scale = mods_ref[0, base + 1:base + 2, :]
    return _rms(x, gain_ref[...]) * (1.0 + scale) + shift


def _resident(shape):
    nd = len(shape)
    return pl.BlockSpec(shape, lambda *_: (0,) * nd, pipeline_mode=pl.Buffered(1))


def _params(n_grid):
    return pltpu.CompilerParams(dimension_semantics=("parallel",) * n_grid,
                                vmem_limit_bytes=V7X_VMEM_LIMIT)


def _mods_kernel(c_ref, w_ref, b_ref, o_ref):
    c = c_ref[...]
    s = (c * jax.nn.sigmoid(c)).astype(BF16)
    o_ref[...] = _dot(s, w_ref[...].astype(BF16)) + b_ref[...]


def _mods(cond, mod_w, mod_b):
    n = mod_w.shape[1]
    tn = D_MODEL
    return pl.pallas_call(
        _mods_kernel,
        grid=(n // tn,),
        in_specs=[pl.BlockSpec((8, D_MODEL), lambda j: (0, 0)),
                  pl.BlockSpec((D_MODEL, tn), lambda j: (0, j)),
                  pl.BlockSpec((1, tn), lambda j: (0, j))],
        out_specs=pl.BlockSpec((8, tn), lambda j: (0, j)),
        out_shape=jax.ShapeDtypeStruct((8, n), F32),
        compiler_params=_params(1),
        name="adaln_mods",
    )(cond, mod_w, mod_b.reshape(1, n))


def _swiglu_residual(x, mods_ref, gain_ref, wi_ref, wo_ref, mod_base):
    h = _normed_input(x, mods_ref, gain_ref, mod_base).astype(BF16)
    g = _dot(h, wi_ref[:, :FFN_DIM])
    u = _dot(h, wi_ref[:, FFN_DIM:])
    a = (g * jax.nn.sigmoid(g) * u).astype(BF16)
    gate = mods_ref[0, mod_base + 2:mod_base + 3, :]
    return x + 0.5 * gate * _dot(a, wo_ref[...])


def _ffn_kernel(x_ref, mods_ref, gain_ref, wi_ref, wo_ref, o_ref):
    o_ref[0] = _swiglu_residual(x_ref[0], mods_ref, gain_ref, wi_ref, wo_ref, 0)


def _ffn(x, mods, gain, wi, wo, *, tm):
    nb, l, _ = x.shape
    tok = pl.BlockSpec((1, tm, D_MODEL), lambda b, i: (b, i, 0))
    return pl.pallas_call(
        _ffn_kernel,
        grid=(nb, l // tm),
        in_specs=[tok,
                  pl.BlockSpec((1, N_MOD, D_MODEL), lambda b, i: (b, 0, 0)),
                  _resident((1, D_MODEL)),
                  _resident(wi.shape), _resident(wo.shape)],
        out_specs=tok,
        out_shape=jax.ShapeDtypeStruct(x.shape, F32),
        compiler_params=_params(2),
        name="ffn",
    )(x, mods, gain, wi, wo)


def _store_kv(k_nope, vals, kr_b, k_ref, vt_ref):
    ones = jnp.ones((V_ROWS - V_HEAD_DIM, vals.shape[0]), BF16)
    for hd in range(MLA_HEADS):
        cols = slice(hd * V_HEAD_DIM, (hd + 1) * V_HEAD_DIM)
        k_ref[0, :, hd * QK_PAD:hd * QK_PAD + QK_NOPE_DIM] = k_nope[:, cols].astype(BF16)
        k_ref[0, :, hd * QK_PAD + QK_NOPE_DIM:(hd + 1) * QK_PAD] = kr_b
        vt_ref[0, hd * V_ROWS:hd * V_ROWS + V_HEAD_DIM, :] = vals[:, cols].T.astype(BF16)
        vt_ref[0, hd * V_ROWS + V_HEAD_DIM:(hd + 1) * V_ROWS, :] = ones


def _attend_tile(q_t, k_nope, vals, kr_b, ob_ref):
    ones = jnp.ones((V_ROWS - V_HEAD_DIM, vals.shape[0]), BF16)
    head_cols = [slice(hd * V_HEAD_DIM, (hd + 1) * V_HEAD_DIM) for hd in range(MLA_HEADS)]
    scores = [_dot(jnp.concatenate([k_nope[:, cols].astype(BF16), kr_b], axis=1), q_t[hd])
              for hd, cols in enumerate(head_cols)]
    for hd, cols in enumerate(head_cols):
        vt_h = jnp.concatenate([vals[:, cols].T.astype(BF16), ones], axis=0)
        s = scores[hd]
        p = jnp.exp2(s - jnp.max(s, axis=0, keepdims=True))
        acc = _dot(vt_h, p.astype(BF16))
        ot = acc[:V_HEAD_DIM] * (1.0 / acc[V_HEAD_DIM:V_HEAD_DIM + 1])
        ob_ref[0, :, cols] = ot.T.astype(BF16)


def _mixer_tokens(x_ref, mods_ref, gain_ref, wu_ref, wv_ref, wlat_ref, wga_ref, vnorm_ref,
                  ws_ref, bs_ref, qnorm_ref, wq_ref, kvnorm_ref, wk_ref, wvup_ref, wa_ref,
                  a_ref, mixed_ref, *, rope_refs, latent_refs, qkv_refs, ob_ref):
    tm = x_ref.shape[1]
    x = x_ref[0]
    h = _normed_input(x, mods_ref, gain_ref, 3).astype(BF16)
    lat = _dot(h, wlat_ref[...])
    v = _dot(h, wv_ref[...])
    u = _dot(h, wu_ref[...])
    ga_logit = _dot(h, wga_ref[...])

    q_lat = lat[:, 0:Q_LORA_RANK]
    ckv = _rms(lat[:, Q_LORA_RANK:Q_LORA_RANK + KV_LORA_RANK], kvnorm_ref[...])
    kr_off = Q_LORA_RANK + KV_LORA_RANK
    kr = lat[:, kr_off:kr_off + ROPE_PAD]
    if latent_refs is not None:
        ckv_ref, krope_ref = latent_refs
        ckv_ref[0] = ckv
        krope_ref[0] = kr[:, 0:QK_ROPE_DIM]
    qn = _rms(q_lat, qnorm_ref[...]).astype(BF16)
    ckv_b = ckv.astype(BF16)
    q_all = _dot(qn, wq_ref[...])
    if rope_refs is not None:
        wqsw_ref, cos_ref, sin_ref = rope_refs
        q_sw = _dot(qn, wqsw_ref[...])
    k_nope = _dot(ckv_b, wk_ref[...])
    vals = _dot(ckv_b, wvup_ref[...])

    vv = _rms(v, vnorm_ref[...]).astype(BF16)
    n_chunks = tm // CHUNK
    for g in range(GMLP_GROUPS):
        cols = slice(g * GMLP_GROUP_DIM, (g + 1) * GMLP_GROUP_DIM)
        blk = jnp.concatenate(
            [vv[c * CHUNK:(c + 1) * CHUNK, cols] for c in range(n_chunks)], axis=1)
        mix = _dot(ws_ref[g], blk)
        for c in range(n_chunks):
            mixed_ref[c * CHUNK:(c + 1) * CHUNK, cols] = (
                mix[:, c * CHUNK:(c + 1) * CHUNK] + bs_ref[:, cols])
    out_a = (u * mixed_ref[...]).astype(BF16)
    a_proj = _dot(out_a, wa_ref[...])

    if rope_refs is not None:
        cos = cos_ref[...]
        sin = sin_ref[...]
        kr = kr * cos + lat[:, kr_off + ROPE_PAD:kr_off + 2 * ROPE_PAD] * sin
    q_t = []
    for hd in range(MLA_HEADS):
        nope = q_all[:, hd * QK_PAD:hd * QK_PAD + QK_NOPE_DIM]
        rot = q_all[:, hd * QK_PAD + QK_NOPE_DIM:(hd + 1) * QK_PAD]
        if rope_refs is not None:
            rot = rot * cos + q_sw[:, hd * ROPE_PAD:(hd + 1) * ROPE_PAD] * sin
        nope_t = (nope * Q_SCALE).T.astype(BF16)
        rot_t = (rot * Q_SCALE).T.astype(BF16)
        if qkv_refs is not None:
            qkv_refs[0][0, hd * QK_PAD:hd * QK_PAD + QK_NOPE_DIM, :] = nope_t
            qkv_refs[0][0, hd * QK_PAD + QK_NOPE_DIM:(hd + 1) * QK_PAD, :] = rot_t
        else:
            q_t.append(jnp.concatenate([nope_t, rot_t], axis=0))
    if qkv_refs is not None:
        _store_kv(k_nope, vals, kr.astype(BF16), qkv_refs[1], qkv_refs[2])
    else:
        _attend_tile(q_t, k_nope, vals, kr.astype(BF16), ob_ref)
    a_ref[0] = _sigmoid(ga_logit) * a_proj


def _mixer_ctx_kernel(x_ref, mods_ref, gain_ref, wu_ref, wv_ref, wlat_ref, wga_ref, vnorm_ref,
                      ws_ref, bs_ref, qnorm_ref, wq_ref, kvnorm_ref, wk_ref, wvup_ref, wa_ref,
                      a_ref, ob_ref, ckv_ref, krope_ref, mixed_ref):
    _mixer_tokens(x_ref, mods_ref, gain_ref, wu_ref, wv_ref, wlat_ref, wga_ref, vnorm_ref,
                  ws_ref, bs_ref, qnorm_ref, wq_ref, kvnorm_ref, wk_ref, wvup_ref, wa_ref,
                  a_ref, mixed_ref, rope_refs=None, latent_refs=(ckv_ref, krope_ref),
                  qkv_refs=None, ob_ref=ob_ref)


def _mixer_lat_kernel(x_ref, mods_ref, gain_ref, wu_ref, wv_ref, wlat_ref, wga_ref, vnorm_ref,
                      ws_ref, bs_ref, qnorm_ref, wq_ref, kvnorm_ref, wk_ref, wvup_ref, wa_ref,
                      wqsw_ref, cos_ref, sin_ref, cckv_ref, ckr_ref,
                      a_ref, qt_ref, k_ref, vt_ref, mixed_ref):
    step = pl.program_id(1)

    @pl.when(step == 0)
    def _():
        ckv_b = cckv_ref[0].astype(BF16)
        _store_kv(_dot(ckv_b, wk_ref[...]), _dot(ckv_b, wvup_ref[...]), ckr_ref[0].astype(BF16),
                  k_ref, vt_ref.at[0])

    @pl.when(step > 0)
    def _():
        _mixer_tokens(x_ref, mods_ref, gain_ref, wu_ref, wv_ref, wlat_ref, wga_ref, vnorm_ref,
                      ws_ref, bs_ref, qnorm_ref, wq_ref, kvnorm_ref, wk_ref, wvup_ref, wa_ref,
                      a_ref, mixed_ref, rope_refs=(wqsw_ref, cos_ref, sin_ref), latent_refs=None,
                      qkv_refs=(qt_ref.at[0], k_ref, vt_ref.at[0]), ob_ref=None)


def _mixer_weights(w):
    return [w['norm_mix'], w['w_u'], w['w_v'], w['w_lat'], w['w_ga'], w['v_norm'], w['w_s'],
            w['b_s'], w['q_norm'], w['w_q'], w['kv_norm'], w['w_k'], w['w_vup'], w['w_a']]


def _mixer_ctx(x, mods, w, *, tm):
    nb, l, _ = x.shape
    tok = lambda width: pl.BlockSpec((1, tm, width), lambda b, i: (b, i, 0))
    weights = _mixer_weights(w)
    return pl.pallas_call(
        _mixer_ctx_kernel,
        grid=(nb, l // tm),
        in_specs=[tok(D_MODEL), pl.BlockSpec((1, N_MOD, D_MODEL), lambda b, i: (b, 0, 0))]
                 + [_resident(a.shape) for a in weights],
        out_specs=[tok(D_MODEL), tok(MLA_HEADS * V_HEAD_DIM), tok(KV_LORA_RANK), tok(QK_ROPE_DIM)],
        out_shape=[jax.ShapeDtypeStruct((nb, l, D_MODEL), F32),
                   jax.ShapeDtypeStruct((nb, l, MLA_HEADS * V_HEAD_DIM), BF16),
                   jax.ShapeDtypeStruct((nb, l, KV_LORA_RANK), F32),
                   jax.ShapeDtypeStruct((nb, l, QK_ROPE_DIM), F32)],
        scratch_shapes=[pltpu.VMEM((tm, GMLP_DIM), F32)],
        compiler_params=_params(2),
        name="mixer_ctx",
    )(x, mods, *weights)


def _mixer_lat(x, mods, w, cos, sin, cache_ckv, cache_kr, *, tm):
    nb, l, _ = x.shape
    past = cache_ckv.shape[1]
    assert past == tm, "the cached context must fill exactly one key block"
    prev = lambda i: jnp.maximum(i - 1, 0)
    tok = pl.BlockSpec((1, tm, D_MODEL), lambda b, i: (b, prev(i), 0))
    table = pl.BlockSpec((tm, ROPE_PAD), lambda b, i: (prev(i), 0))
    weights = _mixer_weights(w) + [w['w_q_sw']]
    return pl.pallas_call(
        _mixer_lat_kernel,
        grid=(nb, 1 + l // tm),
        in_specs=[tok, pl.BlockSpec((1, N_MOD, D_MODEL), lambda b, i: (b, 0, 0))]
                 + [_resident(a.shape) for a in weights] + [table, table]
                 + [pl.BlockSpec((1, past, KV_LORA_RANK), lambda b, i: (b, 0, 0)),
                    pl.BlockSpec((1, past, ROPE_PAD), lambda b, i: (b, 0, 0))],
        out_specs=[tok,
                   pl.BlockSpec((1, 1, MLA_HEADS * QK_PAD, tm), lambda b, i: (b, prev(i), 0, 0)),
                   pl.BlockSpec((1, tm, MLA_HEADS * QK_PAD), lambda b, i: (b, i, 0)),
                   pl.BlockSpec((1, 1, MLA_HEADS * V_ROWS, tm), lambda b, i: (b, i, 0, 0))],
        out_shape=[jax.ShapeDtypeStruct((nb, l, D_MODEL), F32),
                   jax.ShapeDtypeStruct((nb, l // tm, MLA_HEADS * QK_PAD, tm), BF16),
                   jax.ShapeDtypeStruct((nb, past + l, MLA_HEADS * QK_PAD), BF16),
                   jax.ShapeDtypeStruct((nb, (past + l) // tm, MLA_HEADS * V_ROWS, tm), BF16)],
        scratch_shapes=[pltpu.VMEM((tm, GMLP_DIM), F32)],
        compiler_params=pltpu.CompilerParams(dimension_semantics=("parallel", "arbitrary"),
                                             vmem_limit_bytes=V7X_VMEM_LIMIT),
        name="mixer_lat",
    )(x, mods, *weights, cos, sin, cache_ckv, cache_kr)


Q_LANES = 256
SCORES_AHEAD = 2


def _attn_kernel(qt_ref, k_ref, vt_ref, o_ref, *, heads, tk, tk_first):
    seq_k = k_ref.shape[1]
    bounds = [0] + list(range(tk_first, seq_k + 1, tk))
    chunks = [slice(lo, hi) for lo, hi in zip(bounds[:-1], bounds[1:])]
    n_chunks = len(chunks)
    streams = [(j, t) for j in range(heads) for t in range(qt_ref.shape[1])]
    qts = [qt_ref[0, t, j * QK_PAD:(j + 1) * QK_PAD, :] for j, t in streams]

    def values(j, c, p):
        lo, hi = chunks[c].start, chunks[c].stop
        acc = None
        for t in range(lo // Q_LANES, hi // Q_LANES):
            part = _dot(vt_ref[0, t, j * V_ROWS:(j + 1) * V_ROWS, :],
                        p[t * Q_LANES - lo:(t + 1) * Q_LANES - lo])
            acc = part if acc is None else acc + part
        return acc

    def scores(i, c):
        j = streams[i][0]
        return _dot(k_ref[0, chunks[c], j * QK_PAD:(j + 1) * QK_PAD], qts[i])

    pending = [[scores(i, c) for c in range(min(SCORES_AHEAD, n_chunks))]
               for i in range(len(streams))]
    state = [None] * len(streams)
    for c in range(n_chunks):
        for i, (j, _) in enumerate(streams):
            if c + SCORES_AHEAD < n_chunks:
                pending[i].append(scores(i, c + SCORES_AHEAD))
            s = pending[i].pop(0)
            m_c = jnp.max(s, axis=0, keepdims=True)
            if c == 0:
                p = jnp.exp2(s - m_c)
                state[i] = (m_c, values(j, c, p.astype(BF16)))
            else:
                m, acc = state[i]
                m_new = jnp.maximum(m, m_c)
                alpha = jnp.exp2(m - m_new)
                p = jnp.exp2(s - m_new)
                state[i] = (m_new, alpha * acc + values(j, c, p.astype(BF16)))
    for i, (j, t) in enumerate(streams):
        q0 = t * Q_LANES
        _, acc = state[i]
        ot = acc[:V_HEAD_DIM] * (1.0 / acc[V_HEAD_DIM:V_HEAD_DIM + 1])
        o_ref[0, q0:q0 + Q_LANES, j * V_HEAD_DIM:(j + 1) * V_HEAD_DIM] = ot.T.astype(BF16)


def _attention(qt, k, vt, *, seq_q, seq_k, heads, tq, tk, tk_first):
    nb = qt.shape[0]
    return pl.pallas_call(
        functools.partial(_attn_kernel, heads=heads, tk=tk, tk_first=tk_first),
        grid=(nb, MLA_HEADS // heads, seq_q // tq),
        in_specs=[pl.BlockSpec((1, tq // Q_LANES, heads * QK_PAD, Q_LANES),
                               lambda b, h, i: (b, i, h, 0)),
                  pl.BlockSpec((1, seq_k, heads * QK_PAD), lambda b, h, i: (b, 0, h)),
                  pl.BlockSpec((1, seq_k // Q_LANES, heads * V_ROWS, Q_LANES),
                               lambda b, h, i: (b, 0, h, 0))],
        out_specs=pl.BlockSpec((1, tq, heads * V_HEAD_DIM), lambda b, h, i: (b, i, h)),
        out_shape=jax.ShapeDtypeStruct((nb, seq_q, MLA_HEADS * V_HEAD_DIM), BF16),
        compiler_params=_params(3),
        name="attention",
    )(qt, k, vt)


def _merge_ffn_kernel(x_ref, a_ref, ob_ref, mods_ref, gmix_ref, wgb_ref, wb_ref, wo_ref,
                      gffn_ref, wi_ref, wo2_ref, fin_ref, o_ref, *, final_norm):
    x = x_ref[0]
    h = _normed_input(x, mods_ref, gmix_ref, 3).astype(BF16)
    gb = _sigmoid(_dot(h, wgb_ref[...]))
    merged = (a_ref[0] + gb * _dot(ob_ref[0], wb_ref[...])).astype(BF16)
    x = x + mods_ref[0, 5:6, :] * _dot(merged, wo_ref[...])
    out = _swiglu_residual(x, mods_ref, gffn_ref, wi_ref, wo2_ref, 6)
    if final_norm:
        out = _rms(out, fin_ref[...])
    o_ref[0] = out


def _merge_ffn(x, a_part, out_b, mods, gmix, wgb, wb, wo, gffn, wi, wo2, fin, *, final_norm, tm):
    nb, l, _ = x.shape
    tok = pl.BlockSpec((1, tm, D_MODEL), lambda b, i: (b, i, 0))
    return pl.pallas_call(
        functools.partial(_merge_ffn_kernel, final_norm=final_norm),
        grid=(nb, l // tm),
        in_specs=[tok, tok, tok,
                  pl.BlockSpec((1, N_MOD, D_MODEL), lambda b, i: (b, 0, 0)),
                  _resident((1, D_MODEL)),
                  _resident(wgb.shape), _resident(wb.shape), _resident(wo.shape),
                  _resident((1, D_MODEL)), _resident(wi.shape), _resident(wo2.shape),
                  _resident((1, D_MODEL))],
        out_specs=tok,
        out_shape=jax.ShapeDtypeStruct(x.shape, F32),
        compiler_params=_params(2),
        name="merge_ffn",
    )(x, a_part, out_b, mods, gmix, wgb, wb, wo, gffn, wi, wo2, fin)


def _rope_tables(l):
    rows = l // GRID_W
    r = np.repeat(np.arange(rows, dtype=np.float32), GRID_W)
    col = np.tile(np.arange(GRID_W, dtype=np.float32), rows)
    half = QK_ROPE_DIM // 2
    inv = (1.0 / (np.float32(ROPE_BASE) ** (np.arange(0, half, 2, dtype=np.float32) / half))
           ).astype(np.float32)
    ang_r, ang_c = r[:, None] * inv, col[:, None] * inv
    pad = ROPE_PAD - QK_ROPE_DIM
    cos = np.concatenate([np.cos(ang_r), np.cos(ang_r), np.cos(ang_c), np.cos(ang_c),
                          np.ones((l, pad), np.float32)], axis=1)
    sin = np.concatenate([-np.sin(ang_r), np.sin(ang_r), -np.sin(ang_c), np.sin(ang_c),
                          np.zeros((l, pad), np.float32)], axis=1)
    return jnp.asarray(cos, F32), jnp.asarray(sin, F32)


def _layer_weights(l, norm_ffn1, ffn1_w_in, ffn1_w_out, norm_mix, w_in, gmlp_v_norm, gmlp_w_s,
                   gmlp_b_s, q_norm, w_q_up, kv_norm, w_kv_up, w_a_proj, w_b_proj, w_o,
                   norm_ffn2, ffn2_w_in, ffn2_w_out):
    bf = lambda a: a.astype(BF16)
    row = lambda a: a.reshape(1, -1)
    swap = np.arange(QK_ROPE_DIM) ^ (QK_ROPE_DIM // 4)
    win = w_in[l]
    o_v, o_q, o_kv, o_kr, o_ga = np.cumsum(
        [GMLP_DIM, GMLP_DIM, Q_LORA_RANK, KV_LORA_RANK, QK_ROPE_DIM]).tolist()
    o_gb = o_ga + D_MODEL
    kr_w = win[:, o_kr:o_ga]
    pad_kr = jnp.zeros((D_MODEL, ROPE_PAD - QK_ROPE_DIM), F32)
    w_lat = jnp.concatenate([win[:, o_q:o_kr], kr_w, pad_kr, kr_w[:, swap], pad_kr], axis=1)

    wq3 = w_q_up[l].reshape(Q_LORA_RANK, MLA_HEADS, QK_NOPE_DIM + QK_ROPE_DIM)
    pad_q = jnp.zeros((Q_LORA_RANK, MLA_HEADS, ROPE_PAD - QK_ROPE_DIM), F32)
    w_q = jnp.concatenate([wq3, pad_q], axis=2).reshape(Q_LORA_RANK, MLA_HEADS * QK_PAD)
    w_q_sw = jnp.concatenate([wq3[:, :, QK_NOPE_DIM:][:, :, swap], pad_q], axis=2).reshape(
        Q_LORA_RANK, MLA_HEADS * ROPE_PAD)
    wkv3 = w_kv_up[l].reshape(KV_LORA_RANK, MLA_HEADS, QK_NOPE_DIM + V_HEAD_DIM)
    return {
        'norm_ffn1': row(norm_ffn1[l]),
        'ffn1_i': bf(ffn1_w_in[l]),
        'ffn1_o': bf(ffn1_w_out[l]),
        'norm_mix': row(norm_mix[l]),
        'w_u': bf(win[:, :GMLP_DIM]), 'w_v': bf(win[:, o_v:o_q]), 'w_lat': bf(w_lat),
        'w_ga': bf(win[:, o_ga:o_gb]), 'w_gb': bf(win[:, o_gb:]),
        'v_norm': row(gmlp_v_norm[l]),
        'w_s': bf(gmlp_w_s[l]),
        'b_s': jnp.repeat(gmlp_b_s[l], GMLP_GROUP_DIM, axis=1),
        'q_norm': row(q_norm[l]), 'w_q': bf(w_q), 'w_q_sw': bf(w_q_sw),
        'kv_norm': row(kv_norm[l]),
        'w_k': bf(wkv3[:, :, :QK_NOPE_DIM].reshape(KV_LORA_RANK, -1)),
        'w_vup': bf(wkv3[:, :, QK_NOPE_DIM:].reshape(KV_LORA_RANK, -1)),
        'w_a': bf(w_a_proj[l]), 'w_b': bf(w_b_proj[l]), 'w_o': bf(w_o[l]),
        'norm_ffn2': row(norm_ffn2[l]),
        'ffn2_i': bf(ffn2_w_in[l]),
        'ffn2_o': bf(ffn2_w_out[l]),
    }


def _trunk_layer(x, mods, w, fin, mix_attend, *, final_norm, tm_ffn, tm_merge):
    x = _ffn(x, mods, w['norm_ffn1'], w['ffn1_i'], w['ffn1_o'], tm=tm_ffn)
    a_part, out_b, extras = mix_attend(x)
    x = _merge_ffn(x, a_part, out_b, mods, w['norm_mix'], w['w_gb'], w['w_b'], w['w_o'],
                   w['norm_ffn2'], w['ffn2_i'], w['ffn2_o'], fin, final_norm=final_norm,
                   tm=tm_merge)
    return x, extras


def kernel(x_prompt, x_sample, c, cache_ckv, cache_krope, c_ctx, mod_w, mod_b, norm_ffn1, ffn1_w_in, ffn1_w_out, norm_mix, w_in, gmlp_v_norm, gmlp_w_s, gmlp_b_s, q_norm, w_q_up, kv_norm, w_kv_up, w_a_proj, w_b_proj, w_o, norm_ffn2, ffn2_w_in, ffn2_w_out, norm_final):
    batch, seq, _ = x_prompt.shape
    dec_batch, dec_seq, _ = x_sample.shape
    past = cache_ckv.shape[2]
    depth = mod_w.shape[0]
    fin = norm_final.reshape(1, D_MODEL)
    cos, sin = _rope_tables(dec_seq)
    cond = jnp.concatenate([c_ctx[None, :], c, jnp.zeros((8 - 1 - dec_batch, D_MODEL), F32)], axis=0)

    xp = x_prompt.reshape(1, batch * seq, D_MODEL)
    xs = x_sample
    ckv_list, krope_list = [], []
    for l in range(depth):
        w = _layer_weights(l, norm_ffn1, ffn1_w_in, ffn1_w_out, norm_mix, w_in, gmlp_v_norm,
                           gmlp_w_s, gmlp_b_s, q_norm, w_q_up, kv_norm, w_kv_up, w_a_proj,
                           w_b_proj, w_o, norm_ffn2, ffn2_w_in, ffn2_w_out)
        mods = _mods(cond, mod_w[l], mod_b[l]).reshape(8, N_MOD, D_MODEL)
        mods_ctx, mods_lat = mods[0:1], mods[1:1 + dec_batch]
        last = l == depth - 1
        kr_pad = jnp.pad(cache_krope[:, l], ((0, 0), (0, 0), (0, ROPE_PAD - QK_ROPE_DIM)))

        def ctx_mix(x):
            a_part, out_b, ckv, krope = _mixer_ctx(x, mods_ctx, w, tm=seq)
            return a_part, out_b, (ckv, krope)

        def lat_mix(x):
            a_part, qt, k, vt = _mixer_lat(x, mods_lat, w, cos, sin, cache_ckv[:, l], kr_pad, tm=past)
            out_b = _attention(qt, k, vt, seq_q=dec_seq, seq_k=past + dec_seq, heads=1,
                               tq=2048, tk=512, tk_first=past)
            return a_part, out_b, ()

        xp, (ckv_l, krope_l) = _trunk_layer(xp, mods_ctx, w, fin, ctx_mix, final_norm=last,
                                            tm_ffn=1024, tm_merge=512)
        ckv_list.append(ckv_l.reshape(batch, seq, KV_LORA_RANK))
        krope_list.append(krope_l.reshape(batch, seq, QK_ROPE_DIM))
        xs, _ = _trunk_layer(xs, mods_lat, w, fin, lat_mix, final_norm=last, tm_ffn=1024, tm_merge=512)
    y_prompt = xp.reshape(batch, seq, D_MODEL)
    new_ckv = jnp.stack(ckv_list, axis=1)
    new_krope = jnp.stack(krope_list, axis=1)
    return (y_prompt, xs, new_ckv, new_krope)
```

```python
import functools
import math

import numpy as np
import jax
import jax.numpy as jnp
from jax import lax
from jax.experimental import pallas as pl
from jax.experimental.pallas import tpu as pltpu

D_MODEL = 1024
GRID_W = 64
FFN_DIM = 2816
GMLP_GROUPS = 8
GMLP_GROUP_DIM = 128
GMLP_DIM = GMLP_GROUPS * GMLP_GROUP_DIM
CHUNK = 128
MLA_HEADS = 8
QK_NOPE_DIM = 128
QK_ROPE_DIM = 64
V_HEAD_DIM = 128
Q_LORA_RANK = 256
KV_LORA_RANK = 256
ROPE_BASE = 10000.0
N_MOD = 9
EPS = 1e-6

V_ROWS = 144
QK_PAD = 256
ROPE_PAD = QK_PAD - QK_NOPE_DIM
V7X_VMEM_LIMIT = 56 * 1024 * 1024
FFN_TM = 1024
MERGE_TM = 512
ATTN_TQ = 2048
ATTN_TK = 512
MODS_TN = 3 * D_MODEL
Q_SCALE = (QK_NOPE_DIM + QK_ROPE_DIM) ** -0.5 * math.log2(math.e)

F32 = jnp.float32
BF16 = jnp.bfloat16


def _dot(a, b):
    return jnp.dot(a, b, preferred_element_type=F32)


def _rms(x, g):
    return x * lax.rsqrt(jnp.mean(x * x, axis=-1, keepdims=True) + EPS) * g


def _sigmoid(x):
    return 0.5 * jnp.tanh(0.5 * x) + 0.5


def _normed_input(x, mods_ref, gain_ref, base):
    shift = mods_ref[0, base:base + 1, :]
    scale = mods_ref[0, base + 1:base + 2, :]
    return _rms(x, gain_ref[...]) * (1.0 + scale) + shift


def _resident(shape):
    nd = len(shape)
    return pl.BlockSpec(shape, lambda *_: (0,) * nd, pipeline_mode=pl.Buffered(1))


def _params(n_grid):
    return pltpu.CompilerParams(dimension_semantics=("parallel",) * n_grid,
                                vmem_limit_bytes=V7X_VMEM_LIMIT)


def _mods_kernel(c_ref, w_ref, b_ref, o_ref):
    c = c_ref[...]
    s = (c * jax.nn.sigmoid(c)).astype(BF16)
    o_ref[...] = _dot(s, w_ref[...].astype(BF16)) + b_ref[...]


def _mods(cond, mod_w, mod_b):
    n = mod_w.shape[1]
    tn = MODS_TN
    return pl.pallas_call(
        _mods_kernel,
        grid=(n // tn,),
        in_specs=[pl.BlockSpec((8, D_MODEL), lambda j: (0, 0)),
                  pl.BlockSpec((D_MODEL, tn), lambda j: (0, j)),
                  pl.BlockSpec((1, tn), lambda j: (0, j))],
        out_specs=pl.BlockSpec((8, tn), lambda j: (0, j)),
        out_shape=jax.ShapeDtypeStruct((8, n), F32),
        compiler_params=_params(1),
        name="adaln_mods",
    )(cond, mod_w, mod_b.reshape(1, n))


def _swiglu_residual(x, mods_ref, gain_ref, wi_ref, wo_ref, mod_base):
    h = _normed_input(x, mods_ref, gain_ref, mod_base).astype(BF16)
    g = _dot(h, wi_ref[:, :FFN_DIM])
    u = _dot(h, wi_ref[:, FFN_DIM:])
    a = (g * jax.nn.sigmoid(g) * u).astype(BF16)
    gate = mods_ref[0, mod_base + 2:mod_base + 3, :]
    return x + 0.5 * gate * _dot(a, wo_ref[...])


def _ffn_kernel(x_ref, mods_ref, gain_ref, wi_ref, wo_ref, o_ref):
    o_ref[0] = _swiglu_residual(x_ref[0], mods_ref, gain_ref, wi_ref, wo_ref, 0)


def _ffn(x, mods, gain, wi, wo, *, tm):
    nb, l, _ = x.shape
    tok = pl.BlockSpec((1, tm, D_MODEL), lambda b, i: (b, i, 0))
    return pl.pallas_call(
        _ffn_kernel,
        grid=(nb, l // tm),
        in_specs=[tok,
                  pl.BlockSpec((1, N_MOD, D_MODEL), lambda b, i: (b, 0, 0)),
                  _resident((1, D_MODEL)),
                  _resident(wi.shape), _resident(wo.shape)],
        out_specs=tok,
        out_shape=jax.ShapeDtypeStruct(x.shape, F32),
        compiler_params=_params(2),
        name="ffn",
    )(x, mods, gain, wi, wo)


def _store_kv(k_nope, vals, kr_b, k_ref, vt_ref):
    ones = jnp.ones((V_ROWS - V_HEAD_DIM, vals.shape[0]), BF16)
    for hd in range(MLA_HEADS):
        cols = slice(hd * V_HEAD_DIM, (hd + 1) * V_HEAD_DIM)
        k_ref[0, :, hd * QK_PAD:hd * QK_PAD + QK_NOPE_DIM] = k_nope[:, cols].astype(BF16)
        k_ref[0, :, hd * QK_PAD + QK_NOPE_DIM:(hd + 1) * QK_PAD] = kr_b
        vt_ref[0, hd * V_ROWS:hd * V_ROWS + V_HEAD_DIM, :] = vals[:, cols].T.astype(BF16)
        vt_ref[0, hd * V_ROWS + V_HEAD_DIM:(hd + 1) * V_ROWS, :] = ones


def _attend_tile(q_t, k_nope, vals, kr_b, ob_ref):
    ones = jnp.ones((V_ROWS - V_HEAD_DIM, vals.shape[0]), BF16)
    head_cols = [slice(hd * V_HEAD_DIM, (hd + 1) * V_HEAD_DIM) for hd in range(MLA_HEADS)]
    scores = [_dot(jnp.concatenate([k_nope[:, cols].astype(BF16), kr_b], axis=1), q_t[hd])
              for hd, cols in enumerate(head_cols)]
    for hd, cols in enumerate(head_cols):
        vt_h = jnp.concatenate([vals[:, cols].T.astype(BF16), ones], axis=0)
        s = scores[hd]
        p = jnp.exp2(s - jnp.max(s, axis=0, keepdims=True))
        acc = _dot(vt_h, p.astype(BF16))
        ot = acc[:V_HEAD_DIM] * (1.0 / acc[V_HEAD_DIM:V_HEAD_DIM + 1])
        ob_ref[0, :, cols] = ot.T.astype(BF16)


def _mixer_tokens(x_ref, mods_ref, gain_ref, wu_ref, wv_ref, wlat_ref, wga_ref, vnorm_ref,
                  ws_ref, bs_ref, qnorm_ref, wq_ref, kvnorm_ref, wk_ref, wvup_ref, wa_ref,
                  a_ref, mixed_ref, *, rope_refs, latent_refs, qkv_refs, ob_ref):
    tm = x_ref.shape[1]
    x = x_ref[0]
    h = _normed_input(x, mods_ref, gain_ref, 3).astype(BF16)
    lat = _dot(h, wlat_ref[...])
    v = _dot(h, wv_ref[...])
    u = _dot(h, wu_ref[...])
    ga_logit = _dot(h, wga_ref[...])

    q_lat = lat[:, 0:Q_LORA_RANK]
    ckv = _rms(lat[:, Q_LORA_RANK:Q_LORA_RANK + KV_LORA_RANK], kvnorm_ref[...])
    kr_off = Q_LORA_RANK + KV_LORA_RANK
    kr = lat[:, kr_off:kr_off + ROPE_PAD]
    if latent_refs is not None:
        ckv_ref, krope_ref = latent_refs
        ckv_ref[0] = ckv
        krope_ref[0] = kr[:, 0:QK_ROPE_DIM]
    qn = _rms(q_lat, qnorm_ref[...]).astype(BF16)
    ckv_b = ckv.astype(BF16)
    q_all = _dot(qn, wq_ref[...])
    if rope_refs is not None:
        wqsw_ref, cos_ref, sin_ref = rope_refs
        q_sw = _dot(qn, wqsw_ref[...])
    k_nope = _dot(ckv_b, wk_ref[...])
    vals = _dot(ckv_b, wvup_ref[...])

    vv = _rms(v, vnorm_ref[...]).astype(BF16)
    n_chunks = tm // CHUNK
    for g in range(GMLP_GROUPS):
        cols = slice(g * GMLP_GROUP_DIM, (g + 1) * GMLP_GROUP_DIM)
        blk = jnp.concatenate(
            [vv[c * CHUNK:(c + 1) * CHUNK, cols] for c in range(n_chunks)], axis=1)
        mix = _dot(ws_ref[g], blk)
        for c in range(n_chunks):
            mixed_ref[c * CHUNK:(c + 1) * CHUNK, cols] = (
                mix[:, c * CHUNK:(c + 1) * CHUNK] + bs_ref[:, cols])
    out_a = (u * mixed_ref[...]).astype(BF16)
    a_proj = _dot(out_a, wa_ref[...])

    if rope_refs is not None:
        cos = cos_ref[...]
        sin = sin_ref[...]
        kr = kr * cos + lat[:, kr_off + ROPE_PAD:kr_off + 2 * ROPE_PAD] * sin
    q_t = []
    for hd in range(MLA_HEADS):
        nope = q_all[:, hd * QK_PAD:hd * QK_PAD + QK_NOPE_DIM]
        rot = q_all[:, hd * QK_PAD + QK_NOPE_DIM:(hd + 1) * QK_PAD]
        if rope_refs is not None:
            rot = rot * cos + q_sw[:, hd * ROPE_PAD:(hd + 1) * ROPE_PAD] * sin
        nope_t = (nope * Q_SCALE).T.astype(BF16)
        rot_t = (rot * Q_SCALE).T.astype(BF16)
        if qkv_refs is not None:
            qkv_refs[0][0, hd * QK_PAD:hd * QK_PAD + QK_NOPE_DIM, :] = nope_t
            qkv_refs[0][0, hd * QK_PAD + QK_NOPE_DIM:(hd + 1) * QK_PAD, :] = rot_t
        else:
            q_t.append(jnp.concatenate([nope_t, rot_t], axis=0))
    if qkv_refs is not None:
        _store_kv(k_nope, vals, kr.astype(BF16), qkv_refs[1], qkv_refs[2])
    else:
        _attend_tile(q_t, k_nope, vals, kr.astype(BF16), ob_ref)
    a_ref[0] = _sigmoid(ga_logit) * a_proj


def _mixer_ctx_kernel(x_ref, mods_ref, gain_ref, wu_ref, wv_ref, wlat_ref, wga_ref, vnorm_ref,
                      ws_ref, bs_ref, qnorm_ref, wq_ref, kvnorm_ref, wk_ref, wvup_ref, wa_ref,
                      a_ref, ob_ref, ckv_ref, krope_ref, mixed_ref):
    _mixer_tokens(x_ref, mods_ref, gain_ref, wu_ref, wv_ref, wlat_ref, wga_ref, vnorm_ref,
                  ws_ref, bs_ref, qnorm_ref, wq_ref, kvnorm_ref, wk_ref, wvup_ref, wa_ref,
                  a_ref, mixed_ref, rope_refs=None, latent_refs=(ckv_ref, krope_ref),
                  qkv_refs=None, ob_ref=ob_ref)


def _mixer_lat_kernel(x_ref, mods_ref, gain_ref, wu_ref, wv_ref, wlat_ref, wga_ref, vnorm_ref,
                      ws_ref, bs_ref, qnorm_ref, wq_ref, kvnorm_ref, wk_ref, wvup_ref, wa_ref,
                      wqsw_ref, cos_ref, sin_ref, cckv_ref, ckr_ref,
                      a_ref, qt_ref, k_ref, vt_ref, mixed_ref):
    step = pl.program_id(1)

    @pl.when(step == 0)
    def _():
        ckv_b = cckv_ref[0].astype(BF16)
        _store_kv(_dot(ckv_b, wk_ref[...]), _dot(ckv_b, wvup_ref[...]), ckr_ref[0].astype(BF16),
                  k_ref, vt_ref.at[0])

    @pl.when(step > 0)
    def _():
        _mixer_tokens(x_ref, mods_ref, gain_ref, wu_ref, wv_ref, wlat_ref, wga_ref, vnorm_ref,
                      ws_ref, bs_ref, qnorm_ref, wq_ref, kvnorm_ref, wk_ref, wvup_ref, wa_ref,
                      a_ref, mixed_ref, rope_refs=(wqsw_ref, cos_ref, sin_ref), latent_refs=None,
                      qkv_refs=(qt_ref.at[0], k_ref, vt_ref.at[0]), ob_ref=None)


def _mixer_weights(w):
    return [w['norm_mix'], w['w_u'], w['w_v'], w['w_lat'], w['w_ga'], w['v_norm'], w['w_s'],
            w['b_s'], w['q_norm'], w['w_q'], w['kv_norm'], w['w_k'], w['w_vup'], w['w_a']]


def _mixer_ctx(x, mods, w, *, tm):
    nb, l, _ = x.shape
    tok = lambda width: pl.BlockSpec((1, tm, width), lambda b, i: (b, i, 0))
    weights = _mixer_weights(w)
    return pl.pallas_call(
        _mixer_ctx_kernel,
        grid=(nb, l // tm),
        in_specs=[tok(D_MODEL), pl.BlockSpec((1, N_MOD, D_MODEL), lambda b, i: (b, 0, 0))]
                 + [_resident(a.shape) for a in weights],
        out_specs=[tok(D_MODEL), tok(MLA_HEADS * V_HEAD_DIM), tok(KV_LORA_RANK), tok(QK_ROPE_DIM)],
        out_shape=[jax.ShapeDtypeStruct((nb, l, D_MODEL), F32),
                   jax.ShapeDtypeStruct((nb, l, MLA_HEADS * V_HEAD_DIM), BF16),
                   jax.ShapeDtypeStruct((nb, l, KV_LORA_RANK), F32),
                   jax.ShapeDtypeStruct((nb, l, QK_ROPE_DIM), F32)],
        scratch_shapes=[pltpu.VMEM((tm, GMLP_DIM), F32)],
        compiler_params=_params(2),
        name="mixer_ctx",
    )(x, mods, *weights)


def _mixer_lat(x, mods, w, cos, sin, cache_ckv, cache_kr, *, tm):
    nb, l, _ = x.shape
    past = cache_ckv.shape[1]
    assert past == tm, "the cached context must fill exactly one key block"
    prev = lambda i: jnp.maximum(i - 1, 0)
    tok = pl.BlockSpec((1, tm, D_MODEL), lambda b, i: (b, prev(i), 0))
    table = pl.BlockSpec((tm, ROPE_PAD), lambda b, i: (prev(i), 0))
    weights = _mixer_weights(w) + [w['w_q_sw']]
    return pl.pallas_call(
        _mixer_lat_kernel,
        grid=(nb, 1 + l // tm),
        in_specs=[tok, pl.BlockSpec((1, N_MOD, D_MODEL), lambda b, i: (b, 0, 0))]
                 + [_resident(a.shape) for a in weights] + [table, table]
                 + [pl.BlockSpec((1, past, KV_LORA_RANK), lambda b, i: (b, 0, 0)),
                    pl.BlockSpec((1, past, ROPE_PAD), lambda b, i: (b, 0, 0))],
        out_specs=[tok,
                   pl.BlockSpec((1, 1, MLA_HEADS * QK_PAD, tm), lambda b, i: (b, prev(i), 0, 0)),
                   pl.BlockSpec((1, tm, MLA_HEADS * QK_PAD), lambda b, i: (b, i, 0)),
                   pl.BlockSpec((1, 1, MLA_HEADS * V_ROWS, tm), lambda b, i: (b, i, 0, 0))],
        out_shape=[jax.ShapeDtypeStruct((nb, l, D_MODEL), F32),
                   jax.ShapeDtypeStruct((nb, l // tm, MLA_HEADS * QK_PAD, tm), BF16),
                   jax.ShapeDtypeStruct((nb, past + l, MLA_HEADS * QK_PAD), BF16),
                   jax.ShapeDtypeStruct((nb, (past + l) // tm, MLA_HEADS * V_ROWS, tm), BF16)],
        scratch_shapes=[pltpu.VMEM((tm, GMLP_DIM), F32)],
        compiler_params=pltpu.CompilerParams(dimension_semantics=("parallel", "arbitrary"),
                                             vmem_limit_bytes=V7X_VMEM_LIMIT),
        name="mixer_lat",
    )(x, mods, *weights, cos, sin, cache_ckv, cache_kr)


Q_LANES = 256
SCORES_AHEAD = 2


def _attn_kernel(qt_ref, k_ref, vt_ref, o_ref, *, heads, tk, tk_first):
    seq_k = k_ref.shape[1]
    bounds = [0] + list(range(tk_first, seq_k + 1, tk))
    chunks = [slice(lo, hi) for lo, hi in zip(bounds[:-1], bounds[1:])]
    n_chunks = len(chunks)
    streams = [(j, t) for j in range(heads) for t in range(qt_ref.shape[1])]
    qts = [qt_ref[0, t, j * QK_PAD:(j + 1) * QK_PAD, :] for j, t in streams]

    def values(j, c, p):
        lo, hi = chunks[c].start, chunks[c].stop
        acc = None
        for t in range(lo // Q_LANES, hi // Q_LANES):
            part = _dot(vt_ref[0, t, j * V_ROWS:(j + 1) * V_ROWS, :],
                        p[t * Q_LANES - lo:(t + 1) * Q_LANES - lo])
            acc = part if acc is None else acc + part
        return acc

    def scores(i, c):
        j = streams[i][0]
        return _dot(k_ref[0, chunks[c], j * QK_PAD:(j + 1) * QK_PAD], qts[i])

    pending = [[scores(i, c) for c in range(min(SCORES_AHEAD, n_chunks))]
               for i in range(len(streams))]
    state = [None] * len(streams)
    for c in range(n_chunks):
        for i, (j, _) in enumerate(streams):
            if c + SCORES_AHEAD < n_chunks:
                pending[i].append(scores(i, c + SCORES_AHEAD))
            s = pending[i].pop(0)
            m_c = jnp.max(s, axis=0, keepdims=True)
            if c == 0:
                p = jnp.exp2(s - m_c)
                state[i] = (m_c, values(j, c, p.astype(BF16)))
            else:
                m, acc = state[i]
                m_new = jnp.maximum(m, m_c)
                alpha = jnp.exp2(m - m_new)
                p = jnp.exp2(s - m_new)
                state[i] = (m_new, alpha * acc + values(j, c, p.astype(BF16)))
    for i, (j, t) in enumerate(streams):
        q0 = t * Q_LANES
        _, acc = state[i]
        ot = acc[:V_HEAD_DIM] * (1.0 / acc[V_HEAD_DIM:V_HEAD_DIM + 1])
        o_ref[0, q0:q0 + Q_LANES, j * V_HEAD_DIM:(j + 1) * V_HEAD_DIM] = ot.T.astype(BF16)


def _attention(qt, k, vt, *, seq_q, seq_k, heads, tq, tk, tk_first):
    nb = qt.shape[0]
    return pl.pallas_call(
        functools.partial(_attn_kernel, heads=heads, tk=tk, tk_first=tk_first),
        grid=(nb, MLA_HEADS // heads, seq_q // tq),
        in_specs=[pl.BlockSpec((1, tq // Q_LANES, heads * QK_PAD, Q_LANES),
                               lambda b, h, i: (b, i, h, 0)),
                  pl.BlockSpec((1, seq_k, heads * QK_PAD), lambda b, h, i: (b, 0, h)),
                  pl.BlockSpec((1, seq_k // Q_LANES, heads * V_ROWS, Q_LANES),
                               lambda b, h, i: (b, 0, h, 0))],
        out_specs=pl.BlockSpec((1, tq, heads * V_HEAD_DIM), lambda b, h, i: (b, i, h)),
        out_shape=jax.ShapeDtypeStruct((nb, seq_q, MLA_HEADS * V_HEAD_DIM), BF16),
        compiler_params=_params(3),
        name="attention",
    )(qt, k, vt)


def _merge_ffn_kernel(x_ref, a_ref, ob_ref, mods_ref, gmix_ref, wgb_ref, wb_ref, wo_ref,
                      gffn_ref, wi_ref, wo2_ref, fin_ref, o_ref, *, final_norm):
    x = x_ref[0]
    h = _normed_input(x, mods_ref, gmix_ref, 3).astype(BF16)
    gb = _sigmoid(_dot(h, wgb_ref[...]))
    merged = (a_ref[0] + gb * _dot(ob_ref[0], wb_ref[...])).astype(BF16)
    x = x + mods_ref[0, 5:6, :] * _dot(merged, wo_ref[...])
    out = _swiglu_residual(x, mods_ref, gffn_ref, wi_ref, wo2_ref, 6)
    if final_norm:
        out = _rms(out, fin_ref[...])
    o_ref[0] = out


def _merge_ffn(x, a_part, out_b, mods, gmix, wgb, wb, wo, gffn, wi, wo2, fin, *, final_norm, tm):
    nb, l, _ = x.shape
    tok = pl.BlockSpec((1, tm, D_MODEL), lambda b, i: (b, i, 0))
    return pl.pallas_call(
        functools.partial(_merge_ffn_kernel, final_norm=final_norm),
        grid=(nb, l // tm),
        in_specs=[tok, tok, tok,
                  pl.BlockSpec((1, N_MOD, D_MODEL), lambda b, i: (b, 0, 0)),
                  _resident((1, D_MODEL)),
                  _resident(wgb.shape), _resident(wb.shape), _resident(wo.shape),
                  _resident((1, D_MODEL)), _resident(wi.shape), _resident(wo2.shape),
                  _resident((1, D_MODEL))],
        out_specs=tok,
        out_shape=jax.ShapeDtypeStruct(x.shape, F32),
        compiler_params=_params(2),
        name="merge_ffn",
    )(x, a_part, out_b, mods, gmix, wgb, wb, wo, gffn, wi, wo2, fin)


def _rope_tables(l):
    rows = l // GRID_W
    r = np.repeat(np.arange(rows, dtype=np.float32), GRID_W)
    col = np.tile(np.arange(GRID_W, dtype=np.float32), rows)
    half = QK_ROPE_DIM // 2
    inv = (1.0 / (np.float32(ROPE_BASE) ** (np.arange(0, half, 2, dtype=np.float32) / half))
           ).astype(np.float32)
    ang_r, ang_c = r[:, None] * inv, col[:, None] * inv
    pad = ROPE_PAD - QK_ROPE_DIM
    cos = np.concatenate([np.cos(ang_r), np.cos(ang_r), np.cos(ang_c), np.cos(ang_c),
                          np.ones((l, pad), np.float32)], axis=1)
    sin = np.concatenate([-np.sin(ang_r), np.sin(ang_r), -np.sin(ang_c), np.sin(ang_c),
                          np.zeros((l, pad), np.float32)], axis=1)
    return jnp.asarray(cos, F32), jnp.asarray(sin, F32)


def _layer_weights(l, norm_ffn1, ffn1_w_in, ffn1_w_out, norm_mix, w_in, gmlp_v_norm, gmlp_w_s,
                   gmlp_b_s, q_norm, w_q_up, kv_norm, w_kv_up, w_a_proj, w_b_proj, w_o,
                   norm_ffn2, ffn2_w_in, ffn2_w_out):
    bf = lambda a: a.astype(BF16)
    row = lambda a: a.reshape(1, -1)
    swap = np.arange(QK_ROPE_DIM) ^ (QK_ROPE_DIM // 4)
    win = w_in[l]
    o_v, o_q, o_kv, o_kr, o_ga = np.cumsum(
        [GMLP_DIM, GMLP_DIM, Q_LORA_RANK, KV_LORA_RANK, QK_ROPE_DIM]).tolist()
    o_gb = o_ga + D_MODEL
    kr_w = win[:, o_kr:o_ga]
    pad_kr = jnp.zeros((D_MODEL, ROPE_PAD - QK_ROPE_DIM), F32)
    w_lat = jnp.concatenate([win[:, o_q:o_kr], kr_w, pad_kr, kr_w[:, swap], pad_kr], axis=1)

    wq3 = w_q_up[l].reshape(Q_LORA_RANK, MLA_HEADS, QK_NOPE_DIM + QK_ROPE_DIM)
    pad_q = jnp.zeros((Q_LORA_RANK, MLA_HEADS, ROPE_PAD - QK_ROPE_DIM), F32)
    w_q = jnp.concatenate([wq3, pad_q], axis=2).reshape(Q_LORA_RANK, MLA_HEADS * QK_PAD)
    w_q_sw = jnp.concatenate([wq3[:, :, QK_NOPE_DIM:][:, :, swap], pad_q], axis=2).reshape(
        Q_LORA_RANK, MLA_HEADS * ROPE_PAD)
    wkv3 = w_kv_up[l].reshape(KV_LORA_RANK, MLA_HEADS, QK_NOPE_DIM + V_HEAD_DIM)
    return {
        'norm_ffn1': row(norm_ffn1[l]),
        'ffn1_i': bf(ffn1_w_in[l]),
        'ffn1_o': bf(ffn1_w_out[l]),
        'norm_mix': row(norm_mix[l]),
        'w_u': bf(win[:, :GMLP_DIM]), 'w_v': bf(win[:, o_v:o_q]), 'w_lat': bf(w_lat),
        'w_ga': bf(win[:, o_ga:o_gb]), 'w_gb': bf(win[:, o_gb:]),
        'v_norm': row(gmlp_v_norm[l]),
        'w_s': bf(gmlp_w_s[l]),
        'b_s': jnp.repeat(gmlp_b_s[l], GMLP_GROUP_DIM, axis=1),
        'q_norm': row(q_norm[l]), 'w_q': bf(w_q), 'w_q_sw': bf(w_q_sw),
        'kv_norm': row(kv_norm[l]),
        'w_k': bf(wkv3[:, :, :QK_NOPE_DIM].reshape(KV_LORA_RANK, -1)),
        'w_vup': bf(wkv3[:, :, QK_NOPE_DIM:].reshape(KV_LORA_RANK, -1)),
        'w_a': bf(w_a_proj[l]), 'w_b': bf(w_b_proj[l]), 'w_o': bf(w_o[l]),
        'norm_ffn2': row(norm_ffn2[l]),
        'ffn2_i': bf(ffn2_w_in[l]),
        'ffn2_o': bf(ffn2_w_out[l]),
    }


def _trunk_layer(x, mods, w, fin, mix_attend, *, final_norm, tm_ffn, tm_merge):
    x = _ffn(x, mods, w['norm_ffn1'], w['ffn1_i'], w['ffn1_o'], tm=tm_ffn)
    a_part, out_b, extras = mix_attend(x)
    x = _merge_ffn(x, a_part, out_b, mods, w['norm_mix'], w['w_gb'], w['w_b'], w['w_o'],
                   w['norm_ffn2'], w['ffn2_i'], w['ffn2_o'], fin, final_norm=final_norm,
                   tm=tm_merge)
    return x, extras


def kernel(x_prompt, x_sample, c, cache_ckv, cache_krope, c_ctx, mod_w, mod_b, norm_ffn1, ffn1_w_in, ffn1_w_out, norm_mix, w_in, gmlp_v_norm, gmlp_w_s, gmlp_b_s, q_norm, w_q_up, kv_norm, w_kv_up, w_a_proj, w_b_proj, w_o, norm_ffn2, ffn2_w_in, ffn2_w_out, norm_final):
    batch, seq, _ = x_prompt.shape
    dec_batch, dec_seq, _ = x_sample.shape
    past = cache_ckv.shape[2]
    depth = mod_w.shape[0]
    fin = norm_final.reshape(1, D_MODEL)
    cos, sin = _rope_tables(dec_seq)
    cond = jnp.concatenate([c_ctx[None, :], c, jnp.zeros((8 - 1 - dec_batch, D_MODEL), F32)], axis=0)

    xp = x_prompt.reshape(1, batch * seq, D_MODEL)
    xs = x_sample
    ckv_list, krope_list = [], []
    for l in range(depth):
        w = _layer_weights(l, norm_ffn1, ffn1_w_in, ffn1_w_out, norm_mix, w_in, gmlp_v_norm,
                           gmlp_w_s, gmlp_b_s, q_norm, w_q_up, kv_norm, w_kv_up, w_a_proj,
                           w_b_proj, w_o, norm_ffn2, ffn2_w_in, ffn2_w_out)
        mods = _mods(cond, mod_w[l], mod_b[l]).reshape(8, N_MOD, D_MODEL)
        mods_ctx, mods_lat = mods[0:1], mods[1:1 + dec_batch]
        last = l == depth - 1
        kr_pad = jnp.pad(cache_krope[:, l], ((0, 0), (0, 0), (0, ROPE_PAD - QK_ROPE_DIM)))

        def ctx_mix(x):
            a_part, out_b, ckv, krope = _mixer_ctx(x, mods_ctx, w, tm=seq)
            return a_part, out_b, (ckv, krope)

        def lat_mix(x):
            a_part, qt, k, vt = _mixer_lat(x, mods_lat, w, cos, sin, cache_ckv[:, l], kr_pad, tm=past)
            out_b = _attention(qt, k, vt, seq_q=dec_seq, seq_k=past + dec_seq, heads=1,
                               tq=ATTN_TQ, tk=ATTN_TK, tk_first=past)
            return a_part, out_b, ()

        xp, (ckv_l, krope_l) = _trunk_layer(xp, mods_ctx, w, fin, ctx_mix, final_norm=last,
                                            tm_ffn=FFN_TM, tm_merge=MERGE_TM)
        ckv_list.append(ckv_l.reshape(batch, seq, KV_LORA_RANK))
        krope_list.append(krope_l.reshape(batch, seq, QK_ROPE_DIM))
        xs, _ = _trunk_layer(xs, mods_lat, w, fin, lat_mix, final_norm=last, tm_ffn=FFN_TM,
                             tm_merge=MERGE_TM)
    y_prompt = xp.reshape(batch, seq, D_MODEL)
    new_ckv = jnp.stack(ckv_list, axis=1)
    new_krope = jnp.stack(krope_list, axis=1)
    return (y_prompt, xs, new_ckv, new_krope)
```

```python
import functools
import math

import numpy as np
import jax
import jax.numpy as jnp
from jax import lax
from jax.experimental import pallas as pl
from jax.experimental.pallas import tpu as pltpu

D_MODEL = 1024
GRID_W = 64
FFN_DIM = 2816
GMLP_GROUPS = 8
GMLP_GROUP_DIM = 128
GMLP_DIM = GMLP_GROUPS * GMLP_GROUP_DIM
CHUNK = 128
MLA_HEADS = 8
QK_NOPE_DIM = 128
QK_ROPE_DIM = 64
V_HEAD_DIM = 128
Q_LORA_RANK = 256
KV_LORA_RANK = 256
ROPE_BASE = 10000.0
N_MOD = 9
EPS = 1e-6

V_ROWS = 144
QK_PAD = 256
ROPE_PAD = QK_PAD - QK_NOPE_DIM
V7X_VMEM_LIMIT = 56 * 1024 * 1024
FFN_TM = 1024
MERGE_TM = 512
ATTN_TQ = 2048
ATTN_TK = 512
MODS_TN = 3 * D_MODEL
Q_SCALE = (QK_NOPE_DIM + QK_ROPE_DIM) ** -0.5 * math.log2(math.e)

F32 = jnp.float32
BF16 = jnp.bfloat16


def _dot(a, b):
    return jnp.dot(a, b, preferred_element_type=F32)


def _rms(x, g):
    return x * lax.rsqrt(jnp.mean(x * x, axis=-1, keepdims=True) + EPS) * g


def _sigmoid(x):
    return 0.5 * jnp.tanh(0.5 * x) + 0.5


def _normed_input(x, mods_ref, gain_ref, base):
    shift = mods_ref[0, base:base + 1, :]
    scale = mods_ref[0, base + 1:base + 2, :]
    return _rms(x, gain_ref[...]) * (1.0 + scale) + shift


def _resident(shape):
    nd = len(shape)
    return pl.BlockSpec(shape, lambda *_: (0,) * nd, pipeline_mode=pl.Buffered(1))


def _params(n_grid):
    return pltpu.CompilerParams(dimension_semantics=("parallel",) * n_grid,
                                vmem_limit_bytes=V7X_VMEM_LIMIT)


def _mods_kernel(c_ref, w_ref, b_ref, o_ref):
    c = c_ref[...]
    s = (c * jax.nn.sigmoid(c)).astype(BF16)
    o_ref[...] = _dot(s, w_ref[...].astype(BF16)) + b_ref[...]


def _mods(cond, mod_w, mod_b):
    n = mod_w.shape[1]
    tn = MODS_TN
    return pl.pallas_call(
        _mods_kernel,
        grid=(n // tn,),
        in_specs=[pl.BlockSpec((8, D_MODEL), lambda j: (0, 0)),
                  pl.BlockSpec((D_MODEL, tn), lambda j: (0, j)),
                  pl.BlockSpec((1, tn), lambda j: (0, j))],
        out_specs=pl.BlockSpec((8, tn), lambda j: (0, j)),
        out_shape=jax.ShapeDtypeStruct((8, n), F32),
        compiler_params=_params(1),
        name="adaln_mods",
    )(cond, mod_w, mod_b.reshape(1, n))


def _swiglu_residual(x, mods_ref, gain_ref, wi_ref, wo_ref, mod_base):
    h = _normed_input(x, mods_ref, gain_ref, mod_base).astype(BF16)
    g = _dot(h, wi_ref[:, :FFN_DIM])
    u = _dot(h, wi_ref[:, FFN_DIM:])
    a = (g * jax.nn.sigmoid(g) * u).astype(BF16)
    gate = mods_ref[0, mod_base + 2:mod_base + 3, :]
    return x + 0.5 * gate * _dot(a, wo_ref[...])


def _ffn_kernel(x_ref, mods_ref, gain_ref, wi_ref, wo_ref, o_ref):
    o_ref[0] = _swiglu_residual(x_ref[0], mods_ref, gain_ref, wi_ref, wo_ref, 0)


def _ffn(x, mods, gain, wi, wo, *, tm):
    nb, l, _ = x.shape
    tok = pl.BlockSpec((1, tm, D_MODEL), lambda b, i: (b, i, 0))
    return pl.pallas_call(
        _ffn_kernel,
        grid=(nb, l // tm),
        in_specs=[tok,
                  pl.BlockSpec((1, N_MOD, D_MODEL), lambda b, i: (b, 0, 0)),
                  _resident((1, D_MODEL)),
                  _resident(wi.shape), _resident(wo.shape)],
        out_specs=tok,
        out_shape=jax.ShapeDtypeStruct(x.shape, F32),
        compiler_params=_params(2),
        name="ffn",
    )(x, mods, gain, wi, wo)


TOKEN_TILE = 256
MIXER_TM = 2 * TOKEN_TILE


def _store_kv(k_nope, vals, kr_b, k_ref, vt_ref, rows, tile):
    ones = jnp.ones((V_ROWS - V_HEAD_DIM, vals.shape[0]), BF16)
    for hd in range(MLA_HEADS):
        cols = slice(hd * V_HEAD_DIM, (hd + 1) * V_HEAD_DIM)
        k_ref[0, rows, hd * QK_PAD:hd * QK_PAD + QK_NOPE_DIM] = k_nope[:, cols].astype(BF16)
        k_ref[0, rows, hd * QK_PAD + QK_NOPE_DIM:(hd + 1) * QK_PAD] = kr_b
        vt_ref[0, tile, hd * V_ROWS:hd * V_ROWS + V_HEAD_DIM, :] = vals[:, cols].T.astype(BF16)
        vt_ref[0, tile, hd * V_ROWS + V_HEAD_DIM:(hd + 1) * V_ROWS, :] = ones


def _attend_tile(q_t, k_nope, vals, kr_b, ob_ref, rows):
    ones = jnp.ones((V_ROWS - V_HEAD_DIM, vals.shape[0]), BF16)
    head_cols = [slice(hd * V_HEAD_DIM, (hd + 1) * V_HEAD_DIM) for hd in range(MLA_HEADS)]
    scores = [_dot(jnp.concatenate([k_nope[:, cols].astype(BF16), kr_b], axis=1), q_t[hd])
              for hd, cols in enumerate(head_cols)]
    for hd, cols in enumerate(head_cols):
        vt_h = jnp.concatenate([vals[:, cols].T.astype(BF16), ones], axis=0)
        s = scores[hd]
        p = jnp.exp2(s - jnp.max(s, axis=0, keepdims=True))
        acc = _dot(vt_h, p.astype(BF16))
        ot = acc[:V_HEAD_DIM] * (1.0 / acc[V_HEAD_DIM:V_HEAD_DIM + 1])
        ob_ref[0, rows, cols] = ot.T.astype(BF16)


def _mixer_tokens(x_ref, mods_ref, gain_ref, wu_ref, wv_ref, wlat_ref, wga_ref, vnorm_ref,
                  ws_ref, bs_ref, qnorm_ref, wq_ref, kvnorm_ref, wk_ref, wvup_ref, wa_ref,
                  a_ref, mixed_ref, *, tile, rope_refs, latent_refs, qkv_refs, ob_ref):
    rows = slice(tile * TOKEN_TILE, (tile + 1) * TOKEN_TILE)
    x = x_ref[0, rows, :]
    h = _normed_input(x, mods_ref, gain_ref, 3).astype(BF16)
    lat = _dot(h, wlat_ref[...])
    v = _dot(h, wv_ref[...])
    u = _dot(h, wu_ref[...])
    ga_logit = _dot(h, wga_ref[...])

    q_lat = lat[:, 0:Q_LORA_RANK]
    ckv = _rms(lat[:, Q_LORA_RANK:Q_LORA_RANK + KV_LORA_RANK], kvnorm_ref[...])
    kr_off = Q_LORA_RANK + KV_LORA_RANK
    kr = lat[:, kr_off:kr_off + ROPE_PAD]
    if latent_refs is not None:
        ckv_ref, krope_ref = latent_refs
        ckv_ref[0, rows, :] = ckv
        krope_ref[0, rows, :] = kr[:, 0:QK_ROPE_DIM]
    qn = _rms(q_lat, qnorm_ref[...]).astype(BF16)
    ckv_b = ckv.astype(BF16)
    q_all = _dot(qn, wq_ref[...])
    if rope_refs is not None:
        wqsw_ref, cos_ref, sin_ref = rope_refs
        q_sw = _dot(qn, wqsw_ref[...])
    k_nope = _dot(ckv_b, wk_ref[...])
    vals = _dot(ckv_b, wvup_ref[...])

    vv = _rms(v, vnorm_ref[...]).astype(BF16)
    n_chunks = TOKEN_TILE // CHUNK
    for g in range(GMLP_GROUPS):
        cols = slice(g * GMLP_GROUP_DIM, (g + 1) * GMLP_GROUP_DIM)
        blk = jnp.concatenate(
            [vv[c * CHUNK:(c + 1) * CHUNK, cols] for c in range(n_chunks)], axis=1)
        mix = _dot(ws_ref[g], blk)
        for c in range(n_chunks):
            mixed_ref[tile, c * CHUNK:(c + 1) * CHUNK, cols] = (
                mix[:, c * CHUNK:(c + 1) * CHUNK] + bs_ref[:, cols])
    out_a = (u * mixed_ref[tile]).astype(BF16)
    a_proj = _dot(out_a, wa_ref[...])

    if rope_refs is not None:
        cos = cos_ref[rows, :]
        sin = sin_ref[rows, :]
        kr = kr * cos + lat[:, kr_off + ROPE_PAD:kr_off + 2 * ROPE_PAD] * sin
    q_t = []
    for hd in range(MLA_HEADS):
        nope = q_all[:, hd * QK_PAD:hd * QK_PAD + QK_NOPE_DIM]
        rot = q_all[:, hd * QK_PAD + QK_NOPE_DIM:(hd + 1) * QK_PAD]
        if rope_refs is not None:
            rot = rot * cos + q_sw[:, hd * ROPE_PAD:(hd + 1) * ROPE_PAD] * sin
        nope_t = (nope * Q_SCALE).T.astype(BF16)
        rot_t = (rot * Q_SCALE).T.astype(BF16)
        if qkv_refs is not None:
            qkv_refs[0][0, tile, hd * QK_PAD:hd * QK_PAD + QK_NOPE_DIM, :] = nope_t
            qkv_refs[0][0, tile, hd * QK_PAD + QK_NOPE_DIM:(hd + 1) * QK_PAD, :] = rot_t
        else:
            q_t.append(jnp.concatenate([nope_t, rot_t], axis=0))
    if qkv_refs is not None:
        _store_kv(k_nope, vals, kr.astype(BF16), qkv_refs[1], qkv_refs[2], rows, tile)
    else:
        _attend_tile(q_t, k_nope, vals, kr.astype(BF16), ob_ref, rows)
    a_ref[0, rows, :] = _sigmoid(ga_logit) * a_proj


def _mixer_ctx_kernel(x_ref, mods_ref, gain_ref, wu_ref, wv_ref, wlat_ref, wga_ref, vnorm_ref,
                      ws_ref, bs_ref, qnorm_ref, wq_ref, kvnorm_ref, wk_ref, wvup_ref, wa_ref,
                      a_ref, ob_ref, ckv_ref, krope_ref, mixed_ref):
    for tile in range(x_ref.shape[1] // TOKEN_TILE):
        _mixer_tokens(x_ref, mods_ref, gain_ref, wu_ref, wv_ref, wlat_ref, wga_ref, vnorm_ref,
                      ws_ref, bs_ref, qnorm_ref, wq_ref, kvnorm_ref, wk_ref, wvup_ref, wa_ref,
                      a_ref, mixed_ref, tile=tile, rope_refs=None,
                      latent_refs=(ckv_ref, krope_ref), qkv_refs=None, ob_ref=ob_ref)


def _mixer_lat_kernel(x_ref, mods_ref, gain_ref, wu_ref, wv_ref, wlat_ref, wga_ref, vnorm_ref,
                      ws_ref, bs_ref, qnorm_ref, wq_ref, kvnorm_ref, wk_ref, wvup_ref, wa_ref,
                      wqsw_ref, cos_ref, sin_ref, a_ref, qt_ref, k_ref, vt_ref, mixed_ref):
    for tile in range(x_ref.shape[1] // TOKEN_TILE):
        _mixer_tokens(x_ref, mods_ref, gain_ref, wu_ref, wv_ref, wlat_ref, wga_ref, vnorm_ref,
                      ws_ref, bs_ref, qnorm_ref, wq_ref, kvnorm_ref, wk_ref, wvup_ref, wa_ref,
                      a_ref, mixed_ref, tile=tile, rope_refs=(wqsw_ref, cos_ref, sin_ref),
                      latent_refs=None, qkv_refs=(qt_ref, k_ref, vt_ref), ob_ref=None)


def _cache_kv_kernel(ckv_ref, kr_ref, wk_ref, wvup_ref, k_ref, vt_ref):
    ckv_b = ckv_ref[0].astype(BF16)
    _store_kv(_dot(ckv_b, wk_ref[...]), _dot(ckv_b, wvup_ref[...]), kr_ref[0].astype(BF16),
              k_ref, vt_ref, slice(None), 0)


def _mixer_weights(w):
    return [w['norm_mix'], w['w_u'], w['w_v'], w['w_lat'], w['w_ga'], w['v_norm'], w['w_s'],
            w['b_s'], w['q_norm'], w['w_q'], w['kv_norm'], w['w_k'], w['w_vup'], w['w_a']]


def _mixer_ctx(x, mods, w, *, tm):
    nb, l, _ = x.shape
    tok = lambda width: pl.BlockSpec((1, tm, width), lambda b, i: (b, i, 0))
    weights = _mixer_weights(w)
    return pl.pallas_call(
        _mixer_ctx_kernel,
        grid=(nb, l // tm),
        in_specs=[tok(D_MODEL), pl.BlockSpec((1, N_MOD, D_MODEL), lambda b, i: (b, 0, 0))]
                 + [_resident(a.shape) for a in weights],
        out_specs=[tok(D_MODEL), tok(MLA_HEADS * V_HEAD_DIM), tok(KV_LORA_RANK), tok(QK_ROPE_DIM)],
        out_shape=[jax.ShapeDtypeStruct((nb, l, D_MODEL), F32),
                   jax.ShapeDtypeStruct((nb, l, MLA_HEADS * V_HEAD_DIM), BF16),
                   jax.ShapeDtypeStruct((nb, l, KV_LORA_RANK), F32),
                   jax.ShapeDtypeStruct((nb, l, QK_ROPE_DIM), F32)],
        scratch_shapes=[pltpu.VMEM((tm // TOKEN_TILE, TOKEN_TILE, GMLP_DIM), F32)],
        compiler_params=_params(2),
        name="mixer_ctx",
    )(x, mods, *weights)


def _mixer_lat(x, mods, w, cos, sin, *, tm):
    nb, l, _ = x.shape
    tiles = tm // TOKEN_TILE
    tok = lambda width: pl.BlockSpec((1, tm, width), lambda b, i: (b, i, 0))
    tiled = lambda height: pl.BlockSpec((1, tiles, height, TOKEN_TILE), lambda b, i: (b, i, 0, 0))
    table = pl.BlockSpec((tm, ROPE_PAD), lambda b, i: (i, 0))
    weights = _mixer_weights(w) + [w['w_q_sw']]
    return pl.pallas_call(
        _mixer_lat_kernel,
        grid=(nb, l // tm),
        in_specs=[tok(D_MODEL), pl.BlockSpec((1, N_MOD, D_MODEL), lambda b, i: (b, 0, 0))]
                 + [_resident(a.shape) for a in weights] + [table, table],
        out_specs=[tok(D_MODEL), tiled(MLA_HEADS * QK_PAD), tok(MLA_HEADS * QK_PAD),
                   tiled(MLA_HEADS * V_ROWS)],
        out_shape=[jax.ShapeDtypeStruct((nb, l, D_MODEL), F32),
                   jax.ShapeDtypeStruct((nb, l // TOKEN_TILE, MLA_HEADS * QK_PAD, TOKEN_TILE), BF16),
                   jax.ShapeDtypeStruct((nb, l, MLA_HEADS * QK_PAD), BF16),
                   jax.ShapeDtypeStruct((nb, l // TOKEN_TILE, MLA_HEADS * V_ROWS, TOKEN_TILE), BF16)],
        scratch_shapes=[pltpu.VMEM((tiles, TOKEN_TILE, GMLP_DIM), F32)],
        compiler_params=_params(2),
        name="mixer_lat",
    )(x, mods, *weights, cos, sin)


def _cache_kv(ckv, kr_pad, wk, wvup):
    nb, past, _ = ckv.shape
    assert past == TOKEN_TILE, "the cached context must fill exactly one layout tile"
    return pl.pallas_call(
        _cache_kv_kernel,
        grid=(nb,),
        in_specs=[pl.BlockSpec((1, past, KV_LORA_RANK), lambda b: (b, 0, 0)),
                  pl.BlockSpec((1, past, ROPE_PAD), lambda b: (b, 0, 0)),
                  _resident(wk.shape), _resident(wvup.shape)],
        out_specs=[pl.BlockSpec((1, past, MLA_HEADS * QK_PAD), lambda b: (b, 0, 0)),
                   pl.BlockSpec((1, 1, MLA_HEADS * V_ROWS, past), lambda b: (b, 0, 0, 0))],
        out_shape=[jax.ShapeDtypeStruct((nb, past, MLA_HEADS * QK_PAD), BF16),
                   jax.ShapeDtypeStruct((nb, 1, MLA_HEADS * V_ROWS, past), BF16)],
        compiler_params=_params(1),
        name="cache_kv",
    )(ckv, kr_pad, wk, wvup)


SCORES_AHEAD = 2


def _attn_kernel(qt_ref, kc_ref, vtc_ref, k_ref, vt_ref, o_ref, *, heads, tk):
    chunks = [(kc_ref, vtc_ref, 0, kc_ref.shape[1])]
    chunks += [(k_ref, vt_ref, lo, lo + tk) for lo in range(0, k_ref.shape[1], tk)]
    n_chunks = len(chunks)
    streams = [(j, t) for j in range(heads) for t in range(qt_ref.shape[1])]
    qts = [qt_ref[0, t, j * QK_PAD:(j + 1) * QK_PAD, :] for j, t in streams]

    def scores(i, c):
        j = streams[i][0]
        keys, _, lo, hi = chunks[c]
        return _dot(keys[0, lo:hi, j * QK_PAD:(j + 1) * QK_PAD], qts[i])

    def values(j, c, p):
        _, vt, lo, hi = chunks[c]
        acc = None
        for t in range(lo // TOKEN_TILE, hi // TOKEN_TILE):
            part = _dot(vt[0, t, j * V_ROWS:(j + 1) * V_ROWS, :],
                        p[t * TOKEN_TILE - lo:(t + 1) * TOKEN_TILE - lo])
            acc = part if acc is None else acc + part
        return acc

    pending = [[scores(i, c) for c in range(min(SCORES_AHEAD, n_chunks))]
               for i in range(len(streams))]
    state = [None] * len(streams)
    for c in range(n_chunks):
        for i, (j, _) in enumerate(streams):
            if c + SCORES_AHEAD < n_chunks:
                pending[i].append(scores(i, c + SCORES_AHEAD))
            s = pending[i].pop(0)
            m_c = jnp.max(s, axis=0, keepdims=True)
            if c == 0:
                p = jnp.exp2(s - m_c)
                state[i] = (m_c, values(j, c, p.astype(BF16)))
            else:
                m, acc = state[i]
                m_new = jnp.maximum(m, m_c)
                alpha = jnp.exp2(m - m_new)
                p = jnp.exp2(s - m_new)
                state[i] = (m_new, alpha * acc + values(j, c, p.astype(BF16)))
    for i, (j, t) in enumerate(streams):
        q0 = t * TOKEN_TILE
        _, acc = state[i]
        ot = acc[:V_HEAD_DIM] * (1.0 / acc[V_HEAD_DIM:V_HEAD_DIM + 1])
        o_ref[0, q0:q0 + TOKEN_TILE, j * V_HEAD_DIM:(j + 1) * V_HEAD_DIM] = ot.T.astype(BF16)


def _attention(qt, k_cache, vt_cache, k, vt, *, heads, tq, tk):
    nb, q_tiles = qt.shape[:2]
    seq_q = q_tiles * TOKEN_TILE
    past, seq_k = k_cache.shape[1], k.shape[1]
    return pl.pallas_call(
        functools.partial(_attn_kernel, heads=heads, tk=tk),
        grid=(nb, MLA_HEADS // heads, seq_q // tq),
        in_specs=[pl.BlockSpec((1, tq // TOKEN_TILE, heads * QK_PAD, TOKEN_TILE),
                               lambda b, h, i: (b, i, h, 0)),
                  pl.BlockSpec((1, past, heads * QK_PAD), lambda b, h, i: (b, 0, h)),
                  pl.BlockSpec((1, past // TOKEN_TILE, heads * V_ROWS, TOKEN_TILE),
                               lambda b, h, i: (b, 0, h, 0)),
                  pl.BlockSpec((1, seq_k, heads * QK_PAD), lambda b, h, i: (b, 0, h)),
                  pl.BlockSpec((1, seq_k // TOKEN_TILE, heads * V_ROWS, TOKEN_TILE),
                               lambda b, h, i: (b, 0, h, 0))],
        out_specs=pl.BlockSpec((1, tq, heads * V_HEAD_DIM), lambda b, h, i: (b, i, h)),
        out_shape=jax.ShapeDtypeStruct((nb, seq_q, MLA_HEADS * V_HEAD_DIM), BF16),
        compiler_params=_params(3),
        name="attention",
    )(qt, k_cache, vt_cache, k, vt)


def _merge_ffn_kernel(x_ref, a_ref, ob_ref, mods_ref, gmix_ref, wgb_ref, wb_ref, wo_ref,
                      gffn_ref, wi_ref, wo2_ref, fin_ref, o_ref, *, final_norm):
    tm = x_ref.shape[1]
    halves = [slice(0, tm // 2), slice(tm // 2, tm)]
    merged_x = []
    for r in halves:
        x = x_ref[0, r, :]
        h = _normed_input(x, mods_ref, gmix_ref, 3).astype(BF16)
        gb = _sigmoid(_dot(h, wgb_ref[...]))
        merged = (a_ref[0, r, :] + gb * _dot(ob_ref[0, r, :], wb_ref[...])).astype(BF16)
        merged_x.append(x + mods_ref[0, 5:6, :] * _dot(merged, wo_ref[...]))
    for r, x in zip(halves, merged_x):
        out = _swiglu_residual(x, mods_ref, gffn_ref, wi_ref, wo2_ref, 6)
        if final_norm:
            out = _rms(out, fin_ref[...])
        o_ref[0, r, :] = out


def _merge_ffn(x, a_part, out_b, mods, gmix, wgb, wb, wo, gffn, wi, wo2, fin, *, final_norm, tm):
    nb, l, _ = x.shape
    tok = pl.BlockSpec((1, tm, D_MODEL), lambda b, i: (b, i, 0))
    return pl.pallas_call(
        functools.partial(_merge_ffn_kernel, final_norm=final_norm),
        grid=(nb, l // tm),
        in_specs=[tok, tok, tok,
                  pl.BlockSpec((1, N_MOD, D_MODEL), lambda b, i: (b, 0, 0)),
                  _resident((1, D_MODEL)),
                  _resident(wgb.shape), _resident(wb.shape), _resident(wo.shape),
                  _resident((1, D_MODEL)), _resident(wi.shape), _resident(wo2.shape),
                  _resident((1, D_MODEL))],
        out_specs=tok,
        out_shape=jax.ShapeDtypeStruct(x.shape, F32),
        compiler_params=_params(2),
        name="merge_ffn",
    )(x, a_part, out_b, mods, gmix, wgb, wb, wo, gffn, wi, wo2, fin)


def _rope_tables(l):
    rows = l // GRID_W
    r = np.repeat(np.arange(rows, dtype=np.float32), GRID_W)
    col = np.tile(np.arange(GRID_W, dtype=np.float32), rows)
    half = QK_ROPE_DIM // 2
    inv = (1.0 / (np.float32(ROPE_BASE) ** (np.arange(0, half, 2, dtype=np.float32) / half))
           ).astype(np.float32)
    ang_r, ang_c = r[:, None] * inv, col[:, None] * inv
    pad = ROPE_PAD - QK_ROPE_DIM
    cos = np.concatenate([np.cos(ang_r), np.cos(ang_r), np.cos(ang_c), np.cos(ang_c),
                          np.ones((l, pad), np.float32)], axis=1)
    sin = np.concatenate([-np.sin(ang_r), np.sin(ang_r), -np.sin(ang_c), np.sin(ang_c),
                          np.zeros((l, pad), np.float32)], axis=1)
    return jnp.asarray(cos, F32), jnp.asarray(sin, F32)


def _layer_weights(l, norm_ffn1, ffn1_w_in, ffn1_w_out, norm_mix, w_in, gmlp_v_norm, gmlp_w_s,
                   gmlp_b_s, q_norm, w_q_up, kv_norm, w_kv_up, w_a_proj, w_b_proj, w_o,
                   norm_ffn2, ffn2_w_in, ffn2_w_out):
    bf = lambda a: a.astype(BF16)
    row = lambda a: a.reshape(1, -1)
    swap = np.arange(QK_ROPE_DIM) ^ (QK_ROPE_DIM // 4)
    win = w_in[l]
    o_v, o_q, o_kv, o_kr, o_ga = np.cumsum(
        [GMLP_DIM, GMLP_DIM, Q_LORA_RANK, KV_LORA_RANK, QK_ROPE_DIM]).tolist()
    o_gb = o_ga + D_MODEL
    kr_w = win[:, o_kr:o_ga]
    pad_kr = jnp.zeros((D_MODEL, ROPE_PAD - QK_ROPE_DIM), F32)
    w_lat = jnp.concatenate([win[:, o_q:o_kr], kr_w, pad_kr, kr_w[:, swap], pad_kr], axis=1)

    wq3 = w_q_up[l].reshape(Q_LORA_RANK, MLA_HEADS, QK_NOPE_DIM + QK_ROPE_DIM)
    pad_q = jnp.zeros((Q_LORA_RANK, MLA_HEADS, ROPE_PAD - QK_ROPE_DIM), F32)
    w_q = jnp.concatenate([wq3, pad_q], axis=2).reshape(Q_LORA_RANK, MLA_HEADS * QK_PAD)
    w_q_sw = jnp.concatenate([wq3[:, :, QK_NOPE_DIM:][:, :, swap], pad_q], axis=2).reshape(
        Q_LORA_RANK, MLA_HEADS * ROPE_PAD)
    wkv3 = w_kv_up[l].reshape(KV_LORA_RANK, MLA_HEADS, QK_NOPE_DIM + V_HEAD_DIM)
    return {
        'norm_ffn1': row(norm_ffn1[l]),
        'ffn1_i': bf(ffn1_w_in[l]),
        'ffn1_o': bf(ffn1_w_out[l]),
        'norm_mix': row(norm_mix[l]),
        'w_u': bf(win[:, :GMLP_DIM]), 'w_v': bf(win[:, o_v:o_q]), 'w_lat': bf(w_lat),
        'w_ga': bf(win[:, o_ga:o_gb]), 'w_gb': bf(win[:, o_gb:]),
        'v_norm': row(gmlp_v_norm[l]),
        'w_s': bf(gmlp_w_s[l]),
        'b_s': jnp.repeat(gmlp_b_s[l], GMLP_GROUP_DIM, axis=1),
        'q_norm': row(q_norm[l]), 'w_q': bf(w_q), 'w_q_sw': bf(w_q_sw),
        'kv_norm': row(kv_norm[l]),
        'w_k': bf(wkv3[:, :, :QK_NOPE_DIM].reshape(KV_LORA_RANK, -1)),
        'w_vup': bf(wkv3[:, :, QK_NOPE_DIM:].reshape(KV_LORA_RANK, -1)),
        'w_a': bf(w_a_proj[l]), 'w_b': bf(w_b_proj[l]), 'w_o': bf(w_o[l]),
        'norm_ffn2': row(norm_ffn2[l]),
        'ffn2_i': bf(ffn2_w_in[l]),
        'ffn2_o': bf(ffn2_w_out[l]),
    }


def _trunk_layer(x, mods, w, fin, mix_attend, *, final_norm, tm_ffn, tm_merge):
    x = _ffn(x, mods, w['norm_ffn1'], w['ffn1_i'], w['ffn1_o'], tm=tm_ffn)
    a_part, out_b, extras = mix_attend(x)
    x = _merge_ffn(x, a_part, out_b, mods, w['norm_mix'], w['w_gb'], w['w_b'], w['w_o'],
                   w['norm_ffn2'], w['ffn2_i'], w['ffn2_o'], fin, final_norm=final_norm,
                   tm=tm_merge)
    return x, extras


def kernel(x_prompt, x_sample, c, cache_ckv, cache_krope, c_ctx, mod_w, mod_b, norm_ffn1, ffn1_w_in, ffn1_w_out, norm_mix, w_in, gmlp_v_norm, gmlp_w_s, gmlp_b_s, q_norm, w_q_up, kv_norm, w_kv_up, w_a_proj, w_b_proj, w_o, norm_ffn2, ffn2_w_in, ffn2_w_out, norm_final):
    batch, seq, _ = x_prompt.shape
    dec_batch, dec_seq, _ = x_sample.shape
    depth = mod_w.shape[0]
    fin = norm_final.reshape(1, D_MODEL)
    cos, sin = _rope_tables(dec_seq)
    cond = jnp.concatenate([c_ctx[None, :], c, jnp.zeros((8 - 1 - dec_batch, D_MODEL), F32)], axis=0)

    xp = x_prompt.reshape(1, batch * seq, D_MODEL)
    xs = x_sample
    ckv_list, krope_list = [], []
    for l in range(depth):
        w = _layer_weights(l, norm_ffn1, ffn1_w_in, ffn1_w_out, norm_mix, w_in, gmlp_v_norm,
                           gmlp_w_s, gmlp_b_s, q_norm, w_q_up, kv_norm, w_kv_up, w_a_proj,
                           w_b_proj, w_o, norm_ffn2, ffn2_w_in, ffn2_w_out)
        mods = _mods(cond, mod_w[l], mod_b[l]).reshape(8, N_MOD, D_MODEL)
        mods_ctx, mods_lat = mods[0:1], mods[1:1 + dec_batch]
        last = l == depth - 1
        kr_pad = jnp.pad(cache_krope[:, l], ((0, 0), (0, 0), (0, ROPE_PAD - QK_ROPE_DIM)))
        assert seq == TOKEN_TILE, "each mixer sub-tile of the context stream must be one sequence"

        def ctx_mix(x):
            a_part, out_b, ckv, krope = _mixer_ctx(x, mods_ctx, w, tm=MIXER_TM)
            return a_part, out_b, (ckv, krope)

        def lat_mix(x):
            a_part, qt, k, vt = _mixer_lat(x, mods_lat, w, cos, sin, tm=MIXER_TM)
            k_cache, vt_cache = _cache_kv(cache_ckv[:, l], kr_pad, w['w_k'], w['w_vup'])
            out_b = _attention(qt, k_cache, vt_cache, k, vt, heads=1, tq=ATTN_TQ, tk=ATTN_TK)
            return a_part, out_b, ()

        xp, (ckv_l, krope_l) = _trunk_layer(xp, mods_ctx, w, fin, ctx_mix, final_norm=last,
                                            tm_ffn=FFN_TM, tm_merge=MERGE_TM)
        ckv_list.append(ckv_l.reshape(batch, seq, KV_LORA_RANK))
        krope_list.append(krope_l.reshape(batch, seq, QK_ROPE_DIM))
        xs, _ = _trunk_layer(xs, mods_lat, w, fin, lat_mix, final_norm=last, tm_ffn=FFN_TM,
                             tm_merge=MERGE_TM)
    y_prompt = xp.reshape(batch, seq, D_MODEL)
    new_ckv = jnp.stack(ckv_list, axis=1)
    new_krope = jnp.stack(krope_list, axis=1)
    return (y_prompt, xs, new_ckv, new_krope)
```

```python
import functools
import math

import numpy as np
import jax
import jax.numpy as jnp
from jax import lax
from jax.experimental import pallas as pl
from jax.experimental.pallas import tpu as pltpu

D_MODEL = 1024
GRID_W = 64
FFN_DIM = 2816
GMLP_GROUPS = 8
GMLP_GROUP_DIM = 128
GMLP_DIM = GMLP_GROUPS * GMLP_GROUP_DIM
CHUNK = 128
MLA_HEADS = 8
QK_NOPE_DIM = 128
QK_ROPE_DIM = 64
V_HEAD_DIM = 128
Q_LORA_RANK = 256
KV_LORA_RANK = 256
ROPE_BASE = 10000.0
N_MOD = 9
EPS = 1e-6

V_ROWS = 144
QK_PAD = 256
ROPE_PAD = QK_PAD - QK_NOPE_DIM
V7X_VMEM_LIMIT = 56 * 1024 * 1024
FFN_TM = 1024
MERGE_TM = 512
ATTN_TQ = 2048
ATTN_TK = 512
MODS_TN = 3 * D_MODEL
Q_SCALE = (QK_NOPE_DIM + QK_ROPE_DIM) ** -0.5 * math.log2(math.e)

F32 = jnp.float32
BF16 = jnp.bfloat16


def _dot(a, b):
    return jnp.dot(a, b, preferred_element_type=F32)


def _rms(x, g):
    return x * lax.rsqrt(jnp.mean(x * x, axis=-1, keepdims=True) + EPS) * g


def _sigmoid(x):
    return 0.5 * jnp.tanh(0.5 * x) + 0.5


def _normed_input(x, mods_ref, gain_ref, base):
    shift = mods_ref[0, base:base + 1, :]
    scale = mods_ref[0, base + 1:base + 2, :]
    return _rms(x, gain_ref[...]) * (1.0 + scale) + shift


def _resident(shape):
    nd = len(shape)
    return pl.BlockSpec(shape, lambda *_: (0,) * nd, pipeline_mode=pl.Buffered(1))


def _params(n_grid):
    return pltpu.CompilerParams(dimension_semantics=("parallel",) * n_grid,
                                vmem_limit_bytes=V7X_VMEM_LIMIT)


def _mods_kernel(c_ref, w_ref, b_ref, o_ref):
    c = c_ref[...]
    s = (c * jax.nn.sigmoid(c)).astype(BF16)
    o_ref[...] = _dot(s, w_ref[...].astype(BF16)) + b_ref[...]


def _mods(cond, mod_w, mod_b):
    n = mod_w.shape[1]
    tn = MODS_TN
    return pl.pallas_call(
        _mods_kernel,
        grid=(n // tn,),
        in_specs=[pl.BlockSpec((8, D_MODEL), lambda j: (0, 0)),
                  pl.BlockSpec((D_MODEL, tn), lambda j: (0, j)),
                  pl.BlockSpec((1, tn), lambda j: (0, j))],
        out_specs=pl.BlockSpec((8, tn), lambda j: (0, j)),
        out_shape=jax.ShapeDtypeStruct((8, n), F32),
        compiler_params=_params(1),
        name="adaln_mods",
    )(cond, mod_w, mod_b.reshape(1, n))


def _swiglu_residual(x, mods_ref, gain_ref, wi_ref, wo_ref, mod_base):
    h = _normed_input(x, mods_ref, gain_ref, mod_base).astype(BF16)
    g = _dot(h, wi_ref[:, :FFN_DIM])
    u = _dot(h, wi_ref[:, FFN_DIM:])
    a = (g * jax.nn.sigmoid(g) * u).astype(BF16)
    gate = mods_ref[0, mod_base + 2:mod_base + 3, :]
    return x + 0.5 * gate * _dot(a, wo_ref[...])


def _ffn_kernel(x_ref, mods_ref, gain_ref, wi_ref, wo_ref, *refs):
    n_cast = len(refs) // 2
    o_ref = refs[n_cast]
    o_ref[0] = _swiglu_residual(x_ref[0], mods_ref, gain_ref, wi_ref, wo_ref, 0)
    for src, dst in zip(refs[:n_cast], refs[n_cast + 1:]):
        dst[...] = src[...].astype(BF16)


def _ffn(x, mods, gain, wi, wo, *, tm, cast=()):
    nb, l, _ = x.shape
    steps_l = l // tm
    steps = nb * steps_l
    tok = pl.BlockSpec((1, tm, D_MODEL), lambda b, i: (b, i, 0))
    slab = lambda a: pl.BlockSpec((a.shape[0] // steps, a.shape[1]),
                                  lambda b, i: (b * steps_l + i, 0))
    for a in cast:
        assert a.shape[0] % (steps * 16) == 0, "row slabs must hold whole bf16 sublane tiles"
    out = pl.pallas_call(
        _ffn_kernel,
        grid=(nb, steps_l),
        in_specs=[tok,
                  pl.BlockSpec((1, N_MOD, D_MODEL), lambda b, i: (b, 0, 0)),
                  _resident((1, D_MODEL)),
                  _resident(wi.shape), _resident(wo.shape)] + [slab(a) for a in cast],
        out_specs=[tok] + [slab(a) for a in cast],
        out_shape=[jax.ShapeDtypeStruct(x.shape, F32)]
                  + [jax.ShapeDtypeStruct(a.shape, BF16) for a in cast],
        compiler_params=_params(2),
        name="ffn",
    )(x, mods, gain, wi, wo, *cast)
    return out[0], out[1:]


TOKEN_TILE = 256
MIXER_TM = 2 * TOKEN_TILE


def _store_kv(k_nope, vals, kr_b, k_ref, vt_ref, rows, tile):
    ones = jnp.ones((V_ROWS - V_HEAD_DIM, vals.shape[0]), BF16)
    for hd in range(MLA_HEADS):
        cols = slice(hd * V_HEAD_DIM, (hd + 1) * V_HEAD_DIM)
        k_ref[0, rows, hd * QK_PAD:hd * QK_PAD + QK_NOPE_DIM] = k_nope[:, cols].astype(BF16)
        k_ref[0, rows, hd * QK_PAD + QK_NOPE_DIM:(hd + 1) * QK_PAD] = kr_b
        vt_ref[0, tile, hd * V_ROWS:hd * V_ROWS + V_HEAD_DIM, :] = vals[:, cols].T.astype(BF16)
        vt_ref[0, tile, hd * V_ROWS + V_HEAD_DIM:(hd + 1) * V_ROWS, :] = ones


def _attend_tile(q_t, k_nope, vals, kr_b, ob_ref, rows):
    ones = jnp.ones((V_ROWS - V_HEAD_DIM, vals.shape[0]), BF16)
    head_cols = [slice(hd * V_HEAD_DIM, (hd + 1) * V_HEAD_DIM) for hd in range(MLA_HEADS)]
    scores = [_dot(jnp.concatenate([k_nope[:, cols].astype(BF16), kr_b], axis=1), q_t[hd])
              for hd, cols in enumerate(head_cols)]
    for hd, cols in enumerate(head_cols):
        vt_h = jnp.concatenate([vals[:, cols].T.astype(BF16), ones], axis=0)
        s = scores[hd]
        p = jnp.exp2(s - jnp.max(s, axis=0, keepdims=True))
        acc = _dot(vt_h, p.astype(BF16))
        ot = acc[:V_HEAD_DIM] * (1.0 / acc[V_HEAD_DIM:V_HEAD_DIM + 1])
        ob_ref[0, rows, cols] = ot.T.astype(BF16)


def _mixer_tokens(x_ref, mods_ref, gain_ref, wu_ref, wv_ref, wlat_ref, wga_ref, vnorm_ref,
                  ws_ref, bs_ref, qnorm_ref, wq_ref, kvnorm_ref, wk_ref, wvup_ref, wa_ref,
                  a_ref, mixed_ref, *, tile, rope_refs, latent_refs, qkv_refs, ob_ref):
    rows = slice(tile * TOKEN_TILE, (tile + 1) * TOKEN_TILE)
    x = x_ref[0, rows, :]
    h = _normed_input(x, mods_ref, gain_ref, 3).astype(BF16)
    lat = _dot(h, wlat_ref[...])
    v = _dot(h, wv_ref[...])
    u = _dot(h, wu_ref[...])
    ga_logit = _dot(h, wga_ref[...])

    q_lat = lat[:, 0:Q_LORA_RANK]
    ckv = _rms(lat[:, Q_LORA_RANK:Q_LORA_RANK + KV_LORA_RANK], kvnorm_ref[...])
    kr_off = Q_LORA_RANK + KV_LORA_RANK
    kr = lat[:, kr_off:kr_off + ROPE_PAD]
    if latent_refs is not None:
        ckv_ref, krope_ref = latent_refs
        ckv_ref[0, rows, :] = ckv
        krope_ref[0, rows, :] = kr[:, 0:QK_ROPE_DIM]
    qn = _rms(q_lat, qnorm_ref[...]).astype(BF16)
    ckv_b = ckv.astype(BF16)
    q_all = _dot(qn, wq_ref[...])
    if rope_refs is not None:
        wqsw_ref, cos_ref, sin_ref = rope_refs
        q_sw = _dot(qn, wqsw_ref[...])
    k_nope = _dot(ckv_b, wk_ref[...])
    vals = _dot(ckv_b, wvup_ref[...])

    vv = _rms(v, vnorm_ref[...]).astype(BF16)
    n_chunks = TOKEN_TILE // CHUNK
    for g in range(GMLP_GROUPS):
        cols = slice(g * GMLP_GROUP_DIM, (g + 1) * GMLP_GROUP_DIM)
        blk = jnp.concatenate(
            [vv[c * CHUNK:(c + 1) * CHUNK, cols] for c in range(n_chunks)], axis=1)
        mix = _dot(ws_ref[g], blk)
        for c in range(n_chunks):
            mixed_ref[tile, c * CHUNK:(c + 1) * CHUNK, cols] = (
                mix[:, c * CHUNK:(c + 1) * CHUNK] + bs_ref[:, cols])
    out_a = (u * mixed_ref[tile]).astype(BF16)
    a_proj = _dot(out_a, wa_ref[...])

    if rope_refs is not None:
        cos = cos_ref[rows, :]
        sin = sin_ref[rows, :]
        kr = kr * cos + lat[:, kr_off + ROPE_PAD:kr_off + 2 * ROPE_PAD] * sin
    q_t = []
    for hd in range(MLA_HEADS):
        nope = q_all[:, hd * QK_PAD:hd * QK_PAD + QK_NOPE_DIM]
        rot = q_all[:, hd * QK_PAD + QK_NOPE_DIM:(hd + 1) * QK_PAD]
        if rope_refs is not None:
            rot = rot * cos + q_sw[:, hd * ROPE_PAD:(hd + 1) * ROPE_PAD] * sin
        nope_t = (nope * Q_SCALE).T.astype(BF16)
        rot_t = (rot * Q_SCALE).T.astype(BF16)
        if qkv_refs is not None:
            qkv_refs[0][0, tile, hd * QK_PAD:hd * QK_PAD + QK_NOPE_DIM, :] = nope_t
            qkv_refs[0][0, tile, hd * QK_PAD + QK_NOPE_DIM:(hd + 1) * QK_PAD, :] = rot_t
        else:
            q_t.append(jnp.concatenate([nope_t, rot_t], axis=0))
    if qkv_refs is not None:
        _store_kv(k_nope, vals, kr.astype(BF16), qkv_refs[1], qkv_refs[2], rows, tile)
    else:
        _attend_tile(q_t, k_nope, vals, kr.astype(BF16), ob_ref, rows)
    a_ref[0, rows, :] = _sigmoid(ga_logit) * a_proj


def _mixer_ctx_kernel(x_ref, mods_ref, gain_ref, wu_ref, wv_ref, wlat_ref, wga_ref, vnorm_ref,
                      ws_ref, bs_ref, qnorm_ref, wq_ref, kvnorm_ref, wk_ref, wvup_ref, wa_ref,
                      a_ref, ob_ref, ckv_ref, krope_ref, mixed_ref):
    for tile in range(x_ref.shape[1] // TOKEN_TILE):
        _mixer_tokens(x_ref, mods_ref, gain_ref, wu_ref, wv_ref, wlat_ref, wga_ref, vnorm_ref,
                      ws_ref, bs_ref, qnorm_ref, wq_ref, kvnorm_ref, wk_ref, wvup_ref, wa_ref,
                      a_ref, mixed_ref, tile=tile, rope_refs=None,
                      latent_refs=(ckv_ref, krope_ref), qkv_refs=None, ob_ref=ob_ref)


def _mixer_lat_kernel(x_ref, mods_ref, gain_ref, wu_ref, wv_ref, wlat_ref, wga_ref, vnorm_ref,
                      ws_ref, bs_ref, qnorm_ref, wq_ref, kvnorm_ref, wk_ref, wvup_ref, wa_ref,
                      wqsw_ref, cos_ref, sin_ref, a_ref, qt_ref, k_ref, vt_ref, mixed_ref):
    for tile in range(x_ref.shape[1] // TOKEN_TILE):
        _mixer_tokens(x_ref, mods_ref, gain_ref, wu_ref, wv_ref, wlat_ref, wga_ref, vnorm_ref,
                      ws_ref, bs_ref, qnorm_ref, wq_ref, kvnorm_ref, wk_ref, wvup_ref, wa_ref,
                      a_ref, mixed_ref, tile=tile, rope_refs=(wqsw_ref, cos_ref, sin_ref),
                      latent_refs=None, qkv_refs=(qt_ref, k_ref, vt_ref), ob_ref=None)


def _cache_kv_kernel(ckv_ref, kr_ref, wk_ref, wvup_ref, k_ref, vt_ref):
    ckv_b = ckv_ref[0].astype(BF16)
    _store_kv(_dot(ckv_b, wk_ref[...]), _dot(ckv_b, wvup_ref[...]), kr_ref[0].astype(BF16),
              k_ref, vt_ref, slice(None), 0)


def _mixer_weights(w):
    return [w['norm_mix'], w['w_u'], w['w_v'], w['w_lat'], w['w_ga'], w['v_norm'], w['w_s'],
            w['b_s'], w['q_norm'], w['w_q'], w['kv_norm'], w['w_k'], w['w_vup'], w['w_a']]


def _mixer_ctx(x, mods, w, *, tm):
    nb, l, _ = x.shape
    tok = lambda width: pl.BlockSpec((1, tm, width), lambda b, i: (b, i, 0))
    weights = _mixer_weights(w)
    return pl.pallas_call(
        _mixer_ctx_kernel,
        grid=(nb, l // tm),
        in_specs=[tok(D_MODEL), pl.BlockSpec((1, N_MOD, D_MODEL), lambda b, i: (b, 0, 0))]
                 + [_resident(a.shape) for a in weights],
        out_specs=[tok(D_MODEL), tok(MLA_HEADS * V_HEAD_DIM), tok(KV_LORA_RANK), tok(QK_ROPE_DIM)],
        out_shape=[jax.ShapeDtypeStruct((nb, l, D_MODEL), F32),
                   jax.ShapeDtypeStruct((nb, l, MLA_HEADS * V_HEAD_DIM), BF16),
                   jax.ShapeDtypeStruct((nb, l, KV_LORA_RANK), F32),
                   jax.ShapeDtypeStruct((nb, l, QK_ROPE_DIM), F32)],
        scratch_shapes=[pltpu.VMEM((tm // TOKEN_TILE, TOKEN_TILE, GMLP_DIM), F32)],
        compiler_params=_params(2),
        name="mixer_ctx",
    )(x, mods, *weights)


def _mixer_lat(x, mods, w, cos, sin, *, tm):
    nb, l, _ = x.shape
    tiles = tm // TOKEN_TILE
    tok = lambda width: pl.BlockSpec((1, tm, width), lambda b, i: (b, i, 0))
    tiled = lambda height: pl.BlockSpec((1, tiles, height, TOKEN_TILE), lambda b, i: (b, i, 0, 0))
    table = pl.BlockSpec((tm, ROPE_PAD), lambda b, i: (i, 0))
    weights = _mixer_weights(w) + [w['w_q_sw']]
    return pl.pallas_call(
        _mixer_lat_kernel,
        grid=(nb, l // tm),
        in_specs=[tok(D_MODEL), pl.BlockSpec((1, N_MOD, D_MODEL), lambda b, i: (b, 0, 0))]
                 + [_resident(a.shape) for a in weights] + [table, table],
        out_specs=[tok(D_MODEL), tiled(MLA_HEADS * QK_PAD), tok(MLA_HEADS * QK_PAD),
                   tiled(MLA_HEADS * V_ROWS)],
        out_shape=[jax.ShapeDtypeStruct((nb, l, D_MODEL), F32),
                   jax.ShapeDtypeStruct((nb, l // TOKEN_TILE, MLA_HEADS * QK_PAD, TOKEN_TILE), BF16),
                   jax.ShapeDtypeStruct((nb, l, MLA_HEADS * QK_PAD), BF16),
                   jax.ShapeDtypeStruct((nb, l // TOKEN_TILE, MLA_HEADS * V_ROWS, TOKEN_TILE), BF16)],
        scratch_shapes=[pltpu.VMEM((tiles, TOKEN_TILE, GMLP_DIM), F32)],
        compiler_params=_params(2),
        name="mixer_lat",
    )(x, mods, *weights, cos, sin)


def _cache_kv(ckv, kr_pad, wk, wvup):
    nb, past, _ = ckv.shape
    assert past == TOKEN_TILE, "the cached context must fill exactly one layout tile"
    return pl.pallas_call(
        _cache_kv_kernel,
        grid=(nb,),
        in_specs=[pl.BlockSpec((1, past, KV_LORA_RANK), lambda b: (b, 0, 0)),
                  pl.BlockSpec((1, past, ROPE_PAD), lambda b: (b, 0, 0)),
                  _resident(wk.shape), _resident(wvup.shape)],
        out_specs=[pl.BlockSpec((1, past, MLA_HEADS * QK_PAD), lambda b: (b, 0, 0)),
                   pl.BlockSpec((1, 1, MLA_HEADS * V_ROWS, past), lambda b: (b, 0, 0, 0))],
        out_shape=[jax.ShapeDtypeStruct((nb, past, MLA_HEADS * QK_PAD), BF16),
                   jax.ShapeDtypeStruct((nb, 1, MLA_HEADS * V_ROWS, past), BF16)],
        compiler_params=_params(1),
        name="cache_kv",
    )(ckv, kr_pad, wk, wvup)


SCORES_AHEAD = 2


def _attn_kernel(qt_ref, kc_ref, vtc_ref, k_ref, vt_ref, o_ref, *, heads, tk):
    chunks = [(kc_ref, vtc_ref, 0, kc_ref.shape[1])]
    chunks += [(k_ref, vt_ref, lo, lo + tk) for lo in range(0, k_ref.shape[1], tk)]
    n_chunks = len(chunks)
    streams = [(j, t) for j in range(heads) for t in range(qt_ref.shape[1])]
    qts = [qt_ref[0, t, j * QK_PAD:(j + 1) * QK_PAD, :] for j, t in streams]

    def scores(i, c):
        j = streams[i][0]
        keys, _, lo, hi = chunks[c]
        return _dot(keys[0, lo:hi, j * QK_PAD:(j + 1) * QK_PAD], qts[i])

    def values(j, c, p):
        _, vt, lo, hi = chunks[c]
        acc = None
        for t in range(lo // TOKEN_TILE, hi // TOKEN_TILE):
            part = _dot(vt[0, t, j * V_ROWS:(j + 1) * V_ROWS, :],
                        p[t * TOKEN_TILE - lo:(t + 1) * TOKEN_TILE - lo])
            acc = part if acc is None else acc + part
        return acc

    pending = [[scores(i, c) for c in range(min(SCORES_AHEAD, n_chunks))]
               for i in range(len(streams))]
    state = [None] * len(streams)
    for c in range(n_chunks):
        for i, (j, _) in enumerate(streams):
            if c + SCORES_AHEAD < n_chunks:
                pending[i].append(scores(i, c + SCORES_AHEAD))
            s = pending[i].pop(0)
            m_c = jnp.max(s, axis=0, keepdims=True)
            if c == 0:
                p = jnp.exp2(s - m_c)
                state[i] = (m_c, values(j, c, p.astype(BF16)))
            else:
                m, acc = state[i]
                m_new = jnp.maximum(m, m_c)
                alpha = jnp.exp2(m - m_new)
                p = jnp.exp2(s - m_new)
                state[i] = (m_new, alpha * acc + values(j, c, p.astype(BF16)))
    for i, (j, t) in enumerate(streams):
        q0 = t * TOKEN_TILE
        _, acc = state[i]
        ot = acc[:V_HEAD_DIM] * (1.0 / acc[V_HEAD_DIM:V_HEAD_DIM + 1])
        o_ref[0, q0:q0 + TOKEN_TILE, j * V_HEAD_DIM:(j + 1) * V_HEAD_DIM] = ot.T.astype(BF16)


def _attention(qt, k_cache, vt_cache, k, vt, *, heads, tq, tk):
    nb, q_tiles = qt.shape[:2]
    seq_q = q_tiles * TOKEN_TILE
    past, seq_k = k_cache.shape[1], k.shape[1]
    return pl.pallas_call(
        functools.partial(_attn_kernel, heads=heads, tk=tk),
        grid=(nb, MLA_HEADS // heads, seq_q // tq),
        in_specs=[pl.BlockSpec((1, tq // TOKEN_TILE, heads * QK_PAD, TOKEN_TILE),
                               lambda b, h, i: (b, i, h, 0)),
                  pl.BlockSpec((1, past, heads * QK_PAD), lambda b, h, i: (b, 0, h)),
                  pl.BlockSpec((1, past // TOKEN_TILE, heads * V_ROWS, TOKEN_TILE),
                               lambda b, h, i: (b, 0, h, 0)),
                  pl.BlockSpec((1, seq_k, heads * QK_PAD), lambda b, h, i: (b, 0, h)),
                  pl.BlockSpec((1, seq_k // TOKEN_TILE, heads * V_ROWS, TOKEN_TILE),
                               lambda b, h, i: (b, 0, h, 0))],
        out_specs=pl.BlockSpec((1, tq, heads * V_HEAD_DIM), lambda b, h, i: (b, i, h)),
        out_shape=jax.ShapeDtypeStruct((nb, seq_q, MLA_HEADS * V_HEAD_DIM), BF16),
        compiler_params=_params(3),
        name="attention",
    )(qt, k_cache, vt_cache, k, vt)


def _merge_ffn_kernel(x_ref, a_ref, ob_ref, mods_ref, gmix_ref, wgb_ref, wb_ref, wo_ref,
                      gffn_ref, wi_ref, wo2_ref, fin_ref, o_ref, *, final_norm):
    tm = x_ref.shape[1]
    halves = [slice(0, tm // 2), slice(tm // 2, tm)]
    merged_x = []
    for r in halves:
        x = x_ref[0, r, :]
        h = _normed_input(x, mods_ref, gmix_ref, 3).astype(BF16)
        gb = _sigmoid(_dot(h, wgb_ref[...]))
        merged = (a_ref[0, r, :] + gb * _dot(ob_ref[0, r, :], wb_ref[...])).astype(BF16)
        merged_x.append(x + mods_ref[0, 5:6, :] * _dot(merged, wo_ref[...]))
    for r, x in zip(halves, merged_x):
        out = _swiglu_residual(x, mods_ref, gffn_ref, wi_ref, wo2_ref, 6)
        if final_norm:
            out = _rms(out, fin_ref[...])
        o_ref[0, r, :] = out


def _merge_ffn(x, a_part, out_b, mods, gmix, wgb, wb, wo, gffn, wi, wo2, fin, *, final_norm, tm):
    nb, l, _ = x.shape
    tok = pl.BlockSpec((1, tm, D_MODEL), lambda b, i: (b, i, 0))
    return pl.pallas_call(
        functools.partial(_merge_ffn_kernel, final_norm=final_norm),
        grid=(nb, l // tm),
        in_specs=[tok, tok, tok,
                  pl.BlockSpec((1, N_MOD, D_MODEL), lambda b, i: (b, 0, 0)),
                  _resident((1, D_MODEL)),
                  _resident(wgb.shape), _resident(wb.shape), _resident(wo.shape),
                  _resident((1, D_MODEL)), _resident(wi.shape), _resident(wo2.shape),
                  _resident((1, D_MODEL))],
        out_specs=tok,
        out_shape=jax.ShapeDtypeStruct(x.shape, F32),
        compiler_params=_params(2),
        name="merge_ffn",
    )(x, a_part, out_b, mods, gmix, wgb, wb, wo, gffn, wi, wo2, fin)


def _rope_tables(l):
    rows = l // GRID_W
    r = np.repeat(np.arange(rows, dtype=np.float32), GRID_W)
    col = np.tile(np.arange(GRID_W, dtype=np.float32), rows)
    half = QK_ROPE_DIM // 2
    inv = (1.0 / (np.float32(ROPE_BASE) ** (np.arange(0, half, 2, dtype=np.float32) / half))
           ).astype(np.float32)
    ang_r, ang_c = r[:, None] * inv, col[:, None] * inv
    pad = ROPE_PAD - QK_ROPE_DIM
    cos = np.concatenate([np.cos(ang_r), np.cos(ang_r), np.cos(ang_c), np.cos(ang_c),
                          np.ones((l, pad), np.float32)], axis=1)
    sin = np.concatenate([-np.sin(ang_r), np.sin(ang_r), -np.sin(ang_c), np.sin(ang_c),
                          np.zeros((l, pad), np.float32)], axis=1)
    return jnp.asarray(cos, F32), jnp.asarray(sin, F32)


def _layer_weights(l, norm_ffn1, ffn1_w_in, ffn1_w_out, norm_mix, w_in, gmlp_v_norm, gmlp_w_s,
                   gmlp_b_s, q_norm, w_q_up, kv_norm, w_kv_up, norm_ffn2):
    bf = lambda a: a.astype(BF16)
    row = lambda a: a.reshape(1, -1)
    swap = np.arange(QK_ROPE_DIM) ^ (QK_ROPE_DIM // 4)
    win = w_in[l]
    o_v, o_q, o_kv, o_kr, o_ga = np.cumsum(
        [GMLP_DIM, GMLP_DIM, Q_LORA_RANK, KV_LORA_RANK, QK_ROPE_DIM]).tolist()
    o_gb = o_ga + D_MODEL
    kr_w = win[:, o_kr:o_ga]
    pad_kr = jnp.zeros((D_MODEL, ROPE_PAD - QK_ROPE_DIM), F32)
    w_lat = jnp.concatenate([win[:, o_q:o_kr], kr_w, pad_kr, kr_w[:, swap], pad_kr], axis=1)

    wq3 = w_q_up[l].reshape(Q_LORA_RANK, MLA_HEADS, QK_NOPE_DIM + QK_ROPE_DIM)
    pad_q = jnp.zeros((Q_LORA_RANK, MLA_HEADS, ROPE_PAD - QK_ROPE_DIM), F32)
    w_q = jnp.concatenate([wq3, pad_q], axis=2).reshape(Q_LORA_RANK, MLA_HEADS * QK_PAD)
    w_q_sw = jnp.concatenate([wq3[:, :, QK_NOPE_DIM:][:, :, swap], pad_q], axis=2).reshape(
        Q_LORA_RANK, MLA_HEADS * ROPE_PAD)
    wkv3 = w_kv_up[l].reshape(KV_LORA_RANK, MLA_HEADS, QK_NOPE_DIM + V_HEAD_DIM)
    return {
        'norm_ffn1': row(norm_ffn1[l]),
        'ffn1_i': bf(ffn1_w_in[l]),
        'ffn1_o': bf(ffn1_w_out[l]),
        'norm_mix': row(norm_mix[l]),
        'w_u': bf(win[:, :GMLP_DIM]), 'w_v': bf(win[:, o_v:o_q]), 'w_lat': bf(w_lat),
        'w_ga': bf(win[:, o_ga:o_gb]), 'w_gb': bf(win[:, o_gb:]),
        'v_norm': row(gmlp_v_norm[l]),
        'w_s': bf(gmlp_w_s[l]),
        'b_s': jnp.repeat(gmlp_b_s[l], GMLP_GROUP_DIM, axis=1),
        'q_norm': row(q_norm[l]), 'w_q': bf(w_q), 'w_q_sw': bf(w_q_sw),
        'kv_norm': row(kv_norm[l]),
        'w_k': bf(wkv3[:, :, :QK_NOPE_DIM].reshape(KV_LORA_RANK, -1)),
        'w_vup': bf(wkv3[:, :, QK_NOPE_DIM:].reshape(KV_LORA_RANK, -1)),
        'norm_ffn2': row(norm_ffn2[l]),
    }


def kernel(x_prompt, x_sample, c, cache_ckv, cache_krope, c_ctx, mod_w, mod_b, norm_ffn1, ffn1_w_in, ffn1_w_out, norm_mix, w_in, gmlp_v_norm, gmlp_w_s, gmlp_b_s, q_norm, w_q_up, kv_norm, w_kv_up, w_a_proj, w_b_proj, w_o, norm_ffn2, ffn2_w_in, ffn2_w_out, norm_final):
    batch, seq, _ = x_prompt.shape
    dec_batch, dec_seq, _ = x_sample.shape
    depth = mod_w.shape[0]
    assert seq == TOKEN_TILE, "each mixer sub-tile of the context stream must be one sequence"
    fin = norm_final.reshape(1, D_MODEL)
    cos, sin = _rope_tables(dec_seq)
    cond = jnp.concatenate([c_ctx[None, :], c, jnp.zeros((8 - 1 - dec_batch, D_MODEL), F32)], axis=0)

    xp = x_prompt.reshape(1, batch * seq, D_MODEL)
    xs = x_sample
    ckv_list, krope_list = [], []
    for l in range(depth):
        w = _layer_weights(l, norm_ffn1, ffn1_w_in, ffn1_w_out, norm_mix, w_in, gmlp_v_norm,
                           gmlp_w_s, gmlp_b_s, q_norm, w_q_up, kv_norm, w_kv_up, norm_ffn2)
        mods = _mods(cond, mod_w[l], mod_b[l]).reshape(8, N_MOD, D_MODEL)
        mods_ctx, mods_lat = mods[0:1], mods[1:1 + dec_batch]
        last = l == depth - 1
        kr_pad = jnp.pad(cache_krope[:, l], ((0, 0), (0, 0), (0, ROPE_PAD - QK_ROPE_DIM)))

        xp, (w['w_a'], w['w_b'], w['w_o']) = _ffn(
            xp, mods_ctx, w['norm_ffn1'], w['ffn1_i'], w['ffn1_o'], tm=FFN_TM,
            cast=(w_a_proj[l], w_b_proj[l], w_o[l]))
        xs, (w['ffn2_i'], w['ffn2_o']) = _ffn(
            xs, mods_lat, w['norm_ffn1'], w['ffn1_i'], w['ffn1_o'], tm=FFN_TM,
            cast=(ffn2_w_in[l], ffn2_w_out[l]))

        a_ctx, ob_ctx, ckv_l, krope_l = _mixer_ctx(xp, mods_ctx, w, tm=MIXER_TM)
        a_lat, qt, k, vt = _mixer_lat(xs, mods_lat, w, cos, sin, tm=MIXER_TM)
        k_cache, vt_cache = _cache_kv(cache_ckv[:, l], kr_pad, w['w_k'], w['w_vup'])
        ob_lat = _attention(qt, k_cache, vt_cache, k, vt, heads=1, tq=ATTN_TQ, tk=ATTN_TK)

        merge = lambda x, a_part, out_b, m: _merge_ffn(
            x, a_part, out_b, m, w['norm_mix'], w['w_gb'], w['w_b'], w['w_o'],
            w['norm_ffn2'], w['ffn2_i'], w['ffn2_o'], fin, final_norm=last, tm=MERGE_TM)
        xp = merge(xp, a_ctx, ob_ctx, mods_ctx)
        xs = merge(xs, a_lat, ob_lat, mods_lat)
        ckv_list.append(ckv_l.reshape(batch, seq, KV_LORA_RANK))
        krope_list.append(krope_l.reshape(batch, seq, QK_ROPE_DIM))
    y_prompt = xp.reshape(batch, seq, D_MODEL)
    new_ckv = jnp.stack(ckv_list, axis=1)
    new_krope = jnp.stack(krope_list, axis=1)
    return (y_prompt, xs, new_ckv, new_krope)
```

```python
import functools
import math

import numpy as np
import jax
import jax.numpy as jnp
from jax import lax
from jax.experimental import pallas as pl
from jax.experimental.pallas import tpu as pltpu

D_MODEL = 1024
GRID_W = 64
FFN_DIM = 2816
GMLP_GROUPS = 8
GMLP_GROUP_DIM = 128
GMLP_DIM = GMLP_GROUPS * GMLP_GROUP_DIM
CHUNK = 128
MLA_HEADS = 8
QK_NOPE_DIM = 128
QK_ROPE_DIM = 64
V_HEAD_DIM = 128
Q_LORA_RANK = 256
KV_LORA_RANK = 256
ROPE_BASE = 10000.0
N_MOD = 9
EPS = 1e-6

V_ROWS = 144
QK_PAD = 256
ROPE_PAD = QK_PAD - QK_NOPE_DIM
V7X_VMEM_LIMIT = 56 * 1024 * 1024
FFN_TM = 1024
MERGE_TM = 512
ATTN_TQ = 2048
ATTN_TK = 512
MODS_TN = 3 * D_MODEL
Q_SCALE = (QK_NOPE_DIM + QK_ROPE_DIM) ** -0.5 * math.log2(math.e)

F32 = jnp.float32
BF16 = jnp.bfloat16


def _dot(a, b):
    return jnp.dot(a, b, preferred_element_type=F32)


def _rms(x, g):
    return x * lax.rsqrt(jnp.mean(x * x, axis=-1, keepdims=True) + EPS) * g


def _sigmoid(x):
    return 0.5 * jnp.tanh(0.5 * x) + 0.5


def _normed_input(x, mods_ref, gain_ref, base):
    shift = mods_ref[0, base:base + 1, :]
    scale = mods_ref[0, base + 1:base + 2, :]
    return _rms(x, gain_ref[...]) * (1.0 + scale) + shift


def _resident(shape):
    nd = len(shape)
    return pl.BlockSpec(shape, lambda *_: (0,) * nd, pipeline_mode=pl.Buffered(1))


def _params(n_grid):
    return pltpu.CompilerParams(dimension_semantics=("parallel",) * n_grid,
                                vmem_limit_bytes=V7X_VMEM_LIMIT)


def _mods_kernel(c_ref, w_ref, b_ref, o_ref):
    c = c_ref[...]
    s = (c * jax.nn.sigmoid(c)).astype(BF16)
    o_ref[...] = _dot(s, w_ref[...].astype(BF16)) + b_ref[...]


def _mods(cond, mod_w, mod_b):
    n = mod_w.shape[1]
    tn = MODS_TN
    return pl.pallas_call(
        _mods_kernel,
        grid=(n // tn,),
        in_specs=[pl.BlockSpec((8, D_MODEL), lambda j: (0, 0)),
                  pl.BlockSpec((D_MODEL, tn), lambda j: (0, j)),
                  pl.BlockSpec((1, tn), lambda j: (0, j))],
        out_specs=pl.BlockSpec((8, tn), lambda j: (0, j)),
        out_shape=jax.ShapeDtypeStruct((8, n), F32),
        compiler_params=_params(1),
        name="adaln_mods",
    )(cond, mod_w, mod_b.reshape(1, n))


def _swiglu_residual(x, mods_ref, gain_ref, wi_ref, wo_ref, mod_base):
    h = _normed_input(x, mods_ref, gain_ref, mod_base).astype(BF16)
    g = _dot(h, wi_ref[:, :FFN_DIM])
    u = _dot(h, wi_ref[:, FFN_DIM:])
    a = (g * jax.nn.sigmoid(g) * u).astype(BF16)
    gate = mods_ref[0, mod_base + 2:mod_base + 3, :]
    return x + 0.5 * gate * _dot(a, wo_ref[...])


def _ffn_kernel(x_ref, mods_ref, gain_ref, wi_ref, wo_ref, *refs):
    n_cast = len(refs) // 2
    o_ref = refs[n_cast]
    o_ref[0] = _swiglu_residual(x_ref[0], mods_ref, gain_ref, wi_ref, wo_ref, 0)
    for src, dst in zip(refs[:n_cast], refs[n_cast + 1:]):
        dst[...] = src[...].astype(BF16)


def _ffn(x, mods, gain, wi, wo, *, tm, cast=()):
    nb, l, _ = x.shape
    steps_l = l // tm
    steps = nb * steps_l
    tok = pl.BlockSpec((1, tm, D_MODEL), lambda b, i: (b, i, 0))
    slab = lambda a: pl.BlockSpec((a.shape[0] // steps, a.shape[1]),
                                  lambda b, i: (b * steps_l + i, 0))
    for a in cast:
        assert a.shape[0] % (steps * 16) == 0, "row slabs must hold whole bf16 sublane tiles"
    out = pl.pallas_call(
        _ffn_kernel,
        grid=(nb, steps_l),
        in_specs=[tok,
                  pl.BlockSpec((1, N_MOD, D_MODEL), lambda b, i: (b, 0, 0)),
                  _resident((1, D_MODEL)),
                  _resident(wi.shape), _resident(wo.shape)] + [slab(a) for a in cast],
        out_specs=[tok] + [slab(a) for a in cast],
        out_shape=[jax.ShapeDtypeStruct(x.shape, F32)]
                  + [jax.ShapeDtypeStruct(a.shape, BF16) for a in cast],
        compiler_params=_params(2),
        name="ffn",
    )(x, mods, gain, wi, wo, *cast)
    return out[0], out[1:]


TOKEN_TILE = 256
MIXER_TM = 2 * TOKEN_TILE
CTX_MIXER_TM = 4 * TOKEN_TILE


def _store_kv(k_nope, vals, kr_b, k_ref, vt_ref, rows, tile, batch=0):
    ones = jnp.ones((V_ROWS - V_HEAD_DIM, vals.shape[0]), BF16)
    for hd in range(MLA_HEADS):
        cols = slice(hd * V_HEAD_DIM, (hd + 1) * V_HEAD_DIM)
        k_ref[batch, rows, hd * QK_PAD:hd * QK_PAD + QK_NOPE_DIM] = k_nope[:, cols].astype(BF16)
        k_ref[batch, rows, hd * QK_PAD + QK_NOPE_DIM:(hd + 1) * QK_PAD] = kr_b
        vt_ref[batch, tile, hd * V_ROWS:hd * V_ROWS + V_HEAD_DIM, :] = vals[:, cols].T.astype(BF16)
        vt_ref[batch, tile, hd * V_ROWS + V_HEAD_DIM:(hd + 1) * V_ROWS, :] = ones


def _tile_scores(q_t, k_nope, kr_b):
    return [_dot(jnp.concatenate([k_nope[:, hd * V_HEAD_DIM:(hd + 1) * V_HEAD_DIM].astype(BF16),
                                  kr_b], axis=1), q_t[hd]) for hd in range(MLA_HEADS)]


def _tile_attend(scores, vals, ob_ref, rows):
    ones = jnp.ones((V_ROWS - V_HEAD_DIM, vals.shape[0]), BF16)
    for hd, s in enumerate(scores):
        cols = slice(hd * V_HEAD_DIM, (hd + 1) * V_HEAD_DIM)
        vt_h = jnp.concatenate([vals[:, cols].T.astype(BF16), ones], axis=0)
        p = jnp.exp2(s - jnp.max(s, axis=0, keepdims=True))
        acc = _dot(vt_h, p.astype(BF16))
        ot = acc[:V_HEAD_DIM] * (1.0 / acc[V_HEAD_DIM:V_HEAD_DIM + 1])
        ob_ref[0, rows, cols] = ot.T.astype(BF16)


def _mixer_tokens(x_ref, mods_ref, gain_ref, wu_ref, wv_ref, wlat_ref, wga_ref, vnorm_ref,
                  ws_ref, bs_ref, qnorm_ref, wq_ref, kvnorm_ref, wk_ref, wvup_ref, wa_ref,
                  a_ref, mixed_ref, *, tile, rope_refs, latent_refs, qkv_refs, ob_ref):
    rows = slice(tile * TOKEN_TILE, (tile + 1) * TOKEN_TILE)
    x = x_ref[0, rows, :]
    h = _normed_input(x, mods_ref, gain_ref, 3).astype(BF16)
    lat = _dot(h, wlat_ref[...])
    v = _dot(h, wv_ref[...])
    u = _dot(h, wu_ref[...])
    ga_logit = _dot(h, wga_ref[...])

    q_lat = lat[:, 0:Q_LORA_RANK]
    ckv = _rms(lat[:, Q_LORA_RANK:Q_LORA_RANK + KV_LORA_RANK], kvnorm_ref[...])
    kr_off = Q_LORA_RANK + KV_LORA_RANK
    kr = lat[:, kr_off:kr_off + ROPE_PAD]
    if latent_refs is not None:
        ckv_ref, krope_ref = latent_refs
        ckv_ref[0, rows, :] = ckv
        krope_ref[0, rows, :] = kr[:, 0:QK_ROPE_DIM]
    qn = _rms(q_lat, qnorm_ref[...]).astype(BF16)
    ckv_b = ckv.astype(BF16)
    q_all = _dot(qn, wq_ref[...])
    if rope_refs is not None:
        wqsw_ref, cos_ref, sin_ref = rope_refs
        q_sw = _dot(qn, wqsw_ref[...])
    k_nope = _dot(ckv_b, wk_ref[...])
    vals = _dot(ckv_b, wvup_ref[...])

    vv = _rms(v, vnorm_ref[...]).astype(BF16)
    n_chunks = TOKEN_TILE // CHUNK
    for g in range(GMLP_GROUPS):
        cols = slice(g * GMLP_GROUP_DIM, (g + 1) * GMLP_GROUP_DIM)
        blk = jnp.concatenate(
            [vv[c * CHUNK:(c + 1) * CHUNK, cols] for c in range(n_chunks)], axis=1)
        mix = _dot(ws_ref[g], blk)
        for c in range(n_chunks):
            mixed_ref[tile, c * CHUNK:(c + 1) * CHUNK, cols] = (
                mix[:, c * CHUNK:(c + 1) * CHUNK] + bs_ref[:, cols])
    out_a = (u * mixed_ref[tile]).astype(BF16)
    if qkv_refs is not None:
        a_proj = _dot(out_a, wa_ref[...])

    if rope_refs is not None:
        cos = cos_ref[rows, :]
        sin = sin_ref[rows, :]
        kr = kr * cos + lat[:, kr_off + ROPE_PAD:kr_off + 2 * ROPE_PAD] * sin
    q_t = []
    for hd in range(MLA_HEADS):
        nope = q_all[:, hd * QK_PAD:hd * QK_PAD + QK_NOPE_DIM]
        rot = q_all[:, hd * QK_PAD + QK_NOPE_DIM:(hd + 1) * QK_PAD]
        if rope_refs is not None:
            rot = rot * cos + q_sw[:, hd * ROPE_PAD:(hd + 1) * ROPE_PAD] * sin
        nope_t = (nope * Q_SCALE).T.astype(BF16)
        rot_t = (rot * Q_SCALE).T.astype(BF16)
        if qkv_refs is not None:
            qkv_refs[0][0, tile, hd * QK_PAD:hd * QK_PAD + QK_NOPE_DIM, :] = nope_t
            qkv_refs[0][0, tile, hd * QK_PAD + QK_NOPE_DIM:(hd + 1) * QK_PAD, :] = rot_t
        else:
            q_t.append(jnp.concatenate([nope_t, rot_t], axis=0))
    if qkv_refs is not None:
        _store_kv(k_nope, vals, kr.astype(BF16), qkv_refs[1], qkv_refs[2], rows, tile)
    else:
        scores = _tile_scores(q_t, k_nope, kr.astype(BF16))
        a_proj = _dot(out_a, wa_ref[...])
    a_ref[0, rows, :] = _sigmoid(ga_logit) * a_proj
    if qkv_refs is None:
        return lambda: _tile_attend(scores, vals, ob_ref, rows)


def _mixer_ctx_kernel(x_ref, mods_ref, gain_ref, wu_ref, wv_ref, wlat_ref, wga_ref, vnorm_ref,
                      ws_ref, bs_ref, qnorm_ref, wq_ref, kvnorm_ref, wk_ref, wvup_ref, wa_ref,
                      a_ref, ob_ref, ckv_ref, krope_ref, mixed_ref):
    attend = [
        _mixer_tokens(x_ref, mods_ref, gain_ref, wu_ref, wv_ref, wlat_ref, wga_ref, vnorm_ref,
                      ws_ref, bs_ref, qnorm_ref, wq_ref, kvnorm_ref, wk_ref, wvup_ref, wa_ref,
                      a_ref, mixed_ref, tile=tile, rope_refs=None,
                      latent_refs=(ckv_ref, krope_ref), qkv_refs=None, ob_ref=ob_ref)
        for tile in range(x_ref.shape[1] // TOKEN_TILE)]
    for finish in attend:
        finish()


def _mixer_lat_kernel(x_ref, mods_ref, gain_ref, wu_ref, wv_ref, wlat_ref, wga_ref, vnorm_ref,
                      ws_ref, bs_ref, qnorm_ref, wq_ref, kvnorm_ref, wk_ref, wvup_ref, wa_ref,
                      wqsw_ref, cos_ref, sin_ref, a_ref, qt_ref, k_ref, vt_ref, mixed_ref):
    for tile in range(x_ref.shape[1] // TOKEN_TILE):
        _mixer_tokens(x_ref, mods_ref, gain_ref, wu_ref, wv_ref, wlat_ref, wga_ref, vnorm_ref,
                      ws_ref, bs_ref, qnorm_ref, wq_ref, kvnorm_ref, wk_ref, wvup_ref, wa_ref,
                      a_ref, mixed_ref, tile=tile, rope_refs=(wqsw_ref, cos_ref, sin_ref),
                      latent_refs=None, qkv_refs=(qt_ref, k_ref, vt_ref), ob_ref=None)


def _cache_kv_kernel(ckv_ref, kr_ref, wk_ref, wvup_ref, k_ref, vt_ref):
    for b in range(ckv_ref.shape[0]):
        ckv_b = ckv_ref[b].astype(BF16)
        _store_kv(_dot(ckv_b, wk_ref[...]), _dot(ckv_b, wvup_ref[...]), kr_ref[b].astype(BF16),
                  k_ref, vt_ref, slice(None), 0, batch=b)


def _mixer_weights(w):
    return [w['norm_mix'], w['w_u'], w['w_v'], w['w_lat'], w['w_ga'], w['v_norm'], w['w_s'],
            w['b_s'], w['q_norm'], w['w_q'], w['kv_norm'], w['w_k'], w['w_vup'], w['w_a']]


def _mixer_ctx(x, mods, w, *, tm):
    nb, l, _ = x.shape
    tok = lambda width: pl.BlockSpec((1, tm, width), lambda b, i: (b, i, 0))
    weights = _mixer_weights(w)
    return pl.pallas_call(
        _mixer_ctx_kernel,
        grid=(nb, l // tm),
        in_specs=[tok(D_MODEL), pl.BlockSpec((1, N_MOD, D_MODEL), lambda b, i: (b, 0, 0))]
                 + [_resident(a.shape) for a in weights],
        out_specs=[tok(D_MODEL), tok(MLA_HEADS * V_HEAD_DIM), tok(KV_LORA_RANK), tok(QK_ROPE_DIM)],
        out_shape=[jax.ShapeDtypeStruct((nb, l, D_MODEL), F32),
                   jax.ShapeDtypeStruct((nb, l, MLA_HEADS * V_HEAD_DIM), BF16),
                   jax.ShapeDtypeStruct((nb, l, KV_LORA_RANK), F32),
                   jax.ShapeDtypeStruct((nb, l, QK_ROPE_DIM), F32)],
        scratch_shapes=[pltpu.VMEM((tm // TOKEN_TILE, TOKEN_TILE, GMLP_DIM), F32)],
        compiler_params=_params(2),
        name="mixer_ctx",
    )(x, mods, *weights)


def _mixer_lat(x, mods, w, cos, sin, *, tm):
    nb, l, _ = x.shape
    tiles = tm // TOKEN_TILE
    tok = lambda width: pl.BlockSpec((1, tm, width), lambda b, i: (b, i, 0))
    tiled = lambda height: pl.BlockSpec((1, tiles, height, TOKEN_TILE), lambda b, i: (b, i, 0, 0))
    table = pl.BlockSpec((tm, ROPE_PAD), lambda b, i: (i, 0))
    weights = _mixer_weights(w) + [w['w_q_sw']]
    return pl.pallas_call(
        _mixer_lat_kernel,
        grid=(nb, l // tm),
        in_specs=[tok(D_MODEL), pl.BlockSpec((1, N_MOD, D_MODEL), lambda b, i: (b, 0, 0))]
                 + [_resident(a.shape) for a in weights] + [table, table],
        out_specs=[tok(D_MODEL), tiled(MLA_HEADS * QK_PAD), tok(MLA_HEADS * QK_PAD),
                   tiled(MLA_HEADS * V_ROWS)],
        out_shape=[jax.ShapeDtypeStruct((nb, l, D_MODEL), F32),
                   jax.ShapeDtypeStruct((nb, l // TOKEN_TILE, MLA_HEADS * QK_PAD, TOKEN_TILE), BF16),
                   jax.ShapeDtypeStruct((nb, l, MLA_HEADS * QK_PAD), BF16),
                   jax.ShapeDtypeStruct((nb, l // TOKEN_TILE, MLA_HEADS * V_ROWS, TOKEN_TILE), BF16)],
        scratch_shapes=[pltpu.VMEM((tiles, TOKEN_TILE, GMLP_DIM), F32)],
        compiler_params=_params(2),
        name="mixer_lat",
    )(x, mods, *weights, cos, sin)


def _cache_kv(ckv, kr_pad, wk, wvup):
    nb, past, _ = ckv.shape
    assert past == TOKEN_TILE, "the cached context must fill exactly one layout tile"
    return pl.pallas_call(
        _cache_kv_kernel,
        grid=(1,),
        in_specs=[pl.BlockSpec((nb, past, KV_LORA_RANK), lambda i: (0, 0, 0)),
                  pl.BlockSpec((nb, past, ROPE_PAD), lambda i: (0, 0, 0)),
                  _resident(wk.shape), _resident(wvup.shape)],
        out_specs=[pl.BlockSpec((nb, past, MLA_HEADS * QK_PAD), lambda i: (0, 0, 0)),
                   pl.BlockSpec((nb, 1, MLA_HEADS * V_ROWS, past), lambda i: (0, 0, 0, 0))],
        out_shape=[jax.ShapeDtypeStruct((nb, past, MLA_HEADS * QK_PAD), BF16),
                   jax.ShapeDtypeStruct((nb, 1, MLA_HEADS * V_ROWS, past), BF16)],
        compiler_params=_params(1),
        name="cache_kv",
    )(ckv, kr_pad, wk, wvup)


SCORES_AHEAD = 2


def _attn_kernel(qt_ref, kc_ref, vtc_ref, k_ref, vt_ref, o_ref, *, heads, tk):
    chunks = [(kc_ref, vtc_ref, 0, kc_ref.shape[1])]
    chunks += [(k_ref, vt_ref, lo, lo + tk) for lo in range(0, k_ref.shape[1], tk)]
    n_chunks = len(chunks)
    streams = [(j, t) for j in range(heads) for t in range(qt_ref.shape[1])]
    qts = [qt_ref[0, t, j * QK_PAD:(j + 1) * QK_PAD, :] for j, t in streams]

    def scores(i, c):
        j = streams[i][0]
        keys, _, lo, hi = chunks[c]
        return _dot(keys[0, lo:hi, j * QK_PAD:(j + 1) * QK_PAD], qts[i])

    def values(j, c, p):
        _, vt, lo, hi = chunks[c]
        acc = None
        for t in range(lo // TOKEN_TILE, hi // TOKEN_TILE):
            part = _dot(vt[0, t, j * V_ROWS:(j + 1) * V_ROWS, :],
                        p[t * TOKEN_TILE - lo:(t + 1) * TOKEN_TILE - lo])
            acc = part if acc is None else acc + part
        return acc

    pending = [[scores(i, c) for c in range(min(SCORES_AHEAD, n_chunks))]
               for i in range(len(streams))]
    state = [None] * len(streams)
    for c in range(n_chunks):
        for i, (j, _) in enumerate(streams):
            if c + SCORES_AHEAD < n_chunks:
                pending[i].append(scores(i, c + SCORES_AHEAD))
            s = pending[i].pop(0)
            m_c = jnp.max(s, axis=0, keepdims=True)
            if c == 0:
                p = jnp.exp2(s - m_c)
                state[i] = (m_c, values(j, c, p.astype(BF16)))
            else:
                m, acc = state[i]
                m_new = jnp.maximum(m, m_c)
                alpha = jnp.exp2(m - m_new)
                p = jnp.exp2(s - m_new)
                state[i] = (m_new, alpha * acc + values(j, c, p.astype(BF16)))
    for i, (j, t) in enumerate(streams):
        q0 = t * TOKEN_TILE
        _, acc = state[i]
        ot = acc[:V_HEAD_DIM] * (1.0 / acc[V_HEAD_DIM:V_HEAD_DIM + 1])
        o_ref[0, q0:q0 + TOKEN_TILE, j * V_HEAD_DIM:(j + 1) * V_HEAD_DIM] = ot.T.astype(BF16)


def _attention(qt, k_cache, vt_cache, k, vt, *, heads, tq, tk):
    nb, q_tiles = qt.shape[:2]
    seq_q = q_tiles * TOKEN_TILE
    past, seq_k = k_cache.shape[1], k.shape[1]
    return pl.pallas_call(
        functools.partial(_attn_kernel, heads=heads, tk=tk),
        grid=(nb, MLA_HEADS // heads, seq_q // tq),
        in_specs=[pl.BlockSpec((1, tq // TOKEN_TILE, heads * QK_PAD, TOKEN_TILE),
                               lambda b, h, i: (b, i, h, 0)),
                  pl.BlockSpec((1, past, heads * QK_PAD), lambda b, h, i: (b, 0, h)),
                  pl.BlockSpec((1, past // TOKEN_TILE, heads * V_ROWS, TOKEN_TILE),
                               lambda b, h, i: (b, 0, h, 0)),
                  pl.BlockSpec((1, seq_k, heads * QK_PAD), lambda b, h, i: (b, 0, h)),
                  pl.BlockSpec((1, seq_k // TOKEN_TILE, heads * V_ROWS, TOKEN_TILE),
                               lambda b, h, i: (b, 0, h, 0))],
        out_specs=pl.BlockSpec((1, tq, heads * V_HEAD_DIM), lambda b, h, i: (b, i, h)),
        out_shape=jax.ShapeDtypeStruct((nb, seq_q, MLA_HEADS * V_HEAD_DIM), BF16),
        compiler_params=_params(3),
        name="attention",
    )(qt, k_cache, vt_cache, k, vt)


def _merge_ffn_kernel(x_ref, a_ref, ob_ref, mods_ref, gmix_ref, wgb_ref, wb_ref, wo_ref,
                      gffn_ref, wi_ref, wo2_ref, fin_ref, o_ref, *, final_norm):
    tm = x_ref.shape[1]
    halves = [slice(0, tm // 2), slice(tm // 2, tm)]
    b_proj = [_dot(ob_ref[0, r, :], wb_ref[...]) for r in halves]
    merged_x = []
    for r, b in zip(halves, b_proj):
        x = x_ref[0, r, :]
        h = _normed_input(x, mods_ref, gmix_ref, 3).astype(BF16)
        gb = _sigmoid(_dot(h, wgb_ref[...]))
        merged = (a_ref[0, r, :] + gb * b).astype(BF16)
        merged_x.append(x + mods_ref[0, 5:6, :] * _dot(merged, wo_ref[...]))
    for r, x in zip(halves, merged_x):
        out = _swiglu_residual(x, mods_ref, gffn_ref, wi_ref, wo2_ref, 6)
        if final_norm:
            out = _rms(out, fin_ref[...])
        o_ref[0, r, :] = out


def _merge_ffn(x, a_part, out_b, mods, gmix, wgb, wb, wo, gffn, wi, wo2, fin, *, final_norm, tm):
    nb, l, _ = x.shape
    tok = pl.BlockSpec((1, tm, D_MODEL), lambda b, i: (b, i, 0))
    return pl.pallas_call(
        functools.partial(_merge_ffn_kernel, final_norm=final_norm),
        grid=(nb, l // tm),
        in_specs=[tok, tok, tok,
                  pl.BlockSpec((1, N_MOD, D_MODEL), lambda b, i: (b, 0, 0)),
                  _resident((1, D_MODEL)),
                  _resident(wgb.shape), _resident(wb.shape), _resident(wo.shape),
                  _resident((1, D_MODEL)), _resident(wi.shape), _resident(wo2.shape),
                  _resident((1, D_MODEL))],
        out_specs=tok,
        out_shape=jax.ShapeDtypeStruct(x.shape, F32),
        compiler_params=_params(2),
        name="merge_ffn",
    )(x, a_part, out_b, mods, gmix, wgb, wb, wo, gffn, wi, wo2, fin)


def _rope_tables(l):
    rows = l // GRID_W
    r = np.repeat(np.arange(rows, dtype=np.float32), GRID_W)
    col = np.tile(np.arange(GRID_W, dtype=np.float32), rows)
    half = QK_ROPE_DIM // 2
    inv = (1.0 / (np.float32(ROPE_BASE) ** (np.arange(0, half, 2, dtype=np.float32) / half))
           ).astype(np.float32)
    ang_r, ang_c = r[:, None] * inv, col[:, None] * inv
    pad = ROPE_PAD - QK_ROPE_DIM
    cos = np.concatenate([np.cos(ang_r), np.cos(ang_r), np.cos(ang_c), np.cos(ang_c),
                          np.ones((l, pad), np.float32)], axis=1)
    sin = np.concatenate([-np.sin(ang_r), np.sin(ang_r), -np.sin(ang_c), np.sin(ang_c),
                          np.zeros((l, pad), np.float32)], axis=1)
    return jnp.asarray(cos, F32), jnp.asarray(sin, F32)


def _layer_weights(l, norm_ffn1, ffn1_w_in, ffn1_w_out, norm_mix, w_in, gmlp_v_norm, gmlp_w_s,
                   gmlp_b_s, q_norm, w_q_up, kv_norm, w_kv_up, norm_ffn2):
    bf = lambda a: a.astype(BF16)
    row = lambda a: a.reshape(1, -1)
    swap = np.arange(QK_ROPE_DIM) ^ (QK_ROPE_DIM // 4)
    win = w_in[l]
    o_v, o_q, o_kv, o_kr, o_ga = np.cumsum(
        [GMLP_DIM, GMLP_DIM, Q_LORA_RANK, KV_LORA_RANK, QK_ROPE_DIM]).tolist()
    o_gb = o_ga + D_MODEL
    kr_w = win[:, o_kr:o_ga]
    pad_kr = jnp.zeros((D_MODEL, ROPE_PAD - QK_ROPE_DIM), F32)
    w_lat = jnp.concatenate([win[:, o_q:o_kr], kr_w, pad_kr, kr_w[:, swap], pad_kr], axis=1)

    wq3 = w_q_up[l].reshape(Q_LORA_RANK, MLA_HEADS, QK_NOPE_DIM + QK_ROPE_DIM)
    pad_q = jnp.zeros((Q_LORA_RANK, MLA_HEADS, ROPE_PAD - QK_ROPE_DIM), F32)
    w_q = jnp.concatenate([wq3, pad_q], axis=2).reshape(Q_LORA_RANK, MLA_HEADS * QK_PAD)
    w_q_sw = jnp.concatenate([wq3[:, :, QK_NOPE_DIM:][:, :, swap], pad_q], axis=2).reshape(
        Q_LORA_RANK, MLA_HEADS * ROPE_PAD)
    wkv3 = w_kv_up[l].reshape(KV_LORA_RANK, MLA_HEADS, QK_NOPE_DIM + V_HEAD_DIM)
    return {
        'norm_ffn1': row(norm_ffn1[l]),
        'ffn1_i': bf(ffn1_w_in[l]),
        'ffn1_o': bf(ffn1_w_out[l]),
        'norm_mix': row(norm_mix[l]),
        'w_u': bf(win[:, :GMLP_DIM]), 'w_v': bf(win[:, o_v:o_q]), 'w_lat': bf(w_lat),
        'w_ga': bf(win[:, o_ga:o_gb]), 'w_gb': bf(win[:, o_gb:]),
        'v_norm': row(gmlp_v_norm[l]),
        'w_s': bf(gmlp_w_s[l]),
        'b_s': jnp.repeat(gmlp_b_s[l], GMLP_GROUP_DIM, axis=1),
        'q_norm': row(q_norm[l]), 'w_q': bf(w_q), 'w_q_sw': bf(w_q_sw),
        'kv_norm': row(kv_norm[l]),
        'w_k': bf(wkv3[:, :, :QK_NOPE_DIM].reshape(KV_LORA_RANK, -1)),
        'w_vup': bf(wkv3[:, :, QK_NOPE_DIM:].reshape(KV_LORA_RANK, -1)),
        'norm_ffn2': row(norm_ffn2[l]),
    }


def kernel(x_prompt, x_sample, c, cache_ckv, cache_krope, c_ctx, mod_w, mod_b, norm_ffn1, ffn1_w_in, ffn1_w_out, norm_mix, w_in, gmlp_v_norm, gmlp_w_s, gmlp_b_s, q_norm, w_q_up, kv_norm, w_kv_up, w_a_proj, w_b_proj, w_o, norm_ffn2, ffn2_w_in, ffn2_w_out, norm_final):
    batch, seq, _ = x_prompt.shape
    dec_batch, dec_seq, _ = x_sample.shape
    depth = mod_w.shape[0]
    assert seq == TOKEN_TILE, "each mixer sub-tile of the context stream must be one sequence"
    fin = norm_final.reshape(1, D_MODEL)
    cos, sin = _rope_tables(dec_seq)
    cond = jnp.concatenate([c_ctx[None, :], c, jnp.zeros((8 - 1 - dec_batch, D_MODEL), F32)], axis=0)

    xp = x_prompt.reshape(1, batch * seq, D_MODEL)
    xs = x_sample
    ckv_list, krope_list = [], []
    for l in range(depth):
        w = _layer_weights(l, norm_ffn1, ffn1_w_in, ffn1_w_out, norm_mix, w_in, gmlp_v_norm,
                           gmlp_w_s, gmlp_b_s, q_norm, w_q_up, kv_norm, w_kv_up, norm_ffn2)
        mods = _mods(cond, mod_w[l], mod_b[l]).reshape(8, N_MOD, D_MODEL)
        mods_ctx, mods_lat = mods[0:1], mods[1:1 + dec_batch]
        last = l == depth - 1
        kr_pad = jnp.pad(cache_krope[:, l], ((0, 0), (0, 0), (0, ROPE_PAD - QK_ROPE_DIM)))

        xp, (w['w_a'], w['w_b'], w['w_o']) = _ffn(
            xp, mods_ctx, w['norm_ffn1'], w['ffn1_i'], w['ffn1_o'], tm=FFN_TM,
            cast=(w_a_proj[l], w_b_proj[l], w_o[l]))
        xs, (w['ffn2_i'], w['ffn2_o']) = _ffn(
            xs, mods_lat, w['norm_ffn1'], w['ffn1_i'], w['ffn1_o'], tm=FFN_TM,
            cast=(ffn2_w_in[l], ffn2_w_out[l]))

        a_ctx, ob_ctx, ckv_l, krope_l = _mixer_ctx(xp, mods_ctx, w, tm=CTX_MIXER_TM)
        a_lat, qt, k, vt = _mixer_lat(xs, mods_lat, w, cos, sin, tm=MIXER_TM)
        k_cache, vt_cache = _cache_kv(cache_ckv[:, l], kr_pad, w['w_k'], w['w_vup'])
        ob_lat = _attention(qt, k_cache, vt_cache, k, vt, heads=1, tq=ATTN_TQ, tk=ATTN_TK)

        merge = lambda x, a_part, out_b, m: _merge_ffn(
            x, a_part, out_b, m, w['norm_mix'], w['w_gb'], w['w_b'], w['w_o'],
            w['norm_ffn2'], w['ffn2_i'], w['ffn2_o'], fin, final_norm=last, tm=MERGE_TM)
        xp = merge(xp, a_ctx, ob_ctx, mods_ctx)
        xs = merge(xs, a_lat, ob_lat, mods_lat)
        ckv_list.append(ckv_l.reshape(batch, seq, KV_LORA_RANK))
        krope_list.append(krope_l.reshape(batch, seq, QK_ROPE_DIM))
    y_prompt = xp.reshape(batch, seq, D_MODEL)
    new_ckv = jnp.stack(ckv_list, axis=1)
    new_krope = jnp.stack(krope_list, axis=1)
    return (y_prompt, xs, new_ckv, new_krope)
```

```python
import functools
import math

import numpy as np
import jax
import jax.numpy as jnp
from jax import lax
from jax.experimental import pallas as pl
from jax.experimental.pallas import tpu as pltpu

D_MODEL = 1024
GRID_W = 64
FFN_DIM = 2816
GMLP_GROUPS = 8
GMLP_GROUP_DIM = 128
GMLP_DIM = GMLP_GROUPS * GMLP_GROUP_DIM
CHUNK = 128
MLA_HEADS = 8
QK_NOPE_DIM = 128
QK_ROPE_DIM = 64
V_HEAD_DIM = 128
Q_LORA_RANK = 256
KV_LORA_RANK = 256
ROPE_BASE = 10000.0
N_MOD = 9
EPS = 1e-6

V_ROWS = 144
QK_PAD = 256
ROPE_PAD = QK_PAD - QK_NOPE_DIM
V7X_VMEM_LIMIT = 56 * 1024 * 1024
FFN_TM = 1024
MERGE_TM = 512
ATTN_TQ = 2048
ATTN_TK = 512
MODS_TN = 3 * D_MODEL
Q_SCALE = (QK_NOPE_DIM + QK_ROPE_DIM) ** -0.5 * math.log2(math.e)

F32 = jnp.float32
BF16 = jnp.bfloat16


def _dot(a, b):
    return jnp.dot(a, b, preferred_element_type=F32)


def _rms(x, g):
    return x * lax.rsqrt(jnp.mean(x * x, axis=-1, keepdims=True) + EPS) * g


def _sigmoid(x):
    return 0.5 * jnp.tanh(0.5 * x) + 0.5


def _normed_input(x, mods_ref, gain_ref, base):
    shift = mods_ref[0, base:base + 1, :]
    scale = mods_ref[0, base + 1:base + 2, :]
    return _rms(x, gain_ref[...]) * (1.0 + scale) + shift


def _resident(shape):
    nd = len(shape)
    return pl.BlockSpec(shape, lambda *_: (0,) * nd, pipeline_mode=pl.Buffered(1))


def _params(n_grid):
    return pltpu.CompilerParams(dimension_semantics=("parallel",) * n_grid,
                                vmem_limit_bytes=V7X_VMEM_LIMIT)


def _mods_kernel(c_ref, w_ref, b_ref, o_ref):
    c = c_ref[...]
    s = (c * jax.nn.sigmoid(c)).astype(BF16)
    o_ref[...] = _dot(s, w_ref[...].astype(BF16)) + b_ref[...]


def _mods(cond, mod_w, mod_b):
    n = mod_w.shape[1]
    tn = MODS_TN
    return pl.pallas_call(
        _mods_kernel,
        grid=(n // tn,),
        in_specs=[pl.BlockSpec((8, D_MODEL), lambda j: (0, 0)),
                  pl.BlockSpec((D_MODEL, tn), lambda j: (0, j)),
                  pl.BlockSpec((1, tn), lambda j: (0, j))],
        out_specs=pl.BlockSpec((8, tn), lambda j: (0, j)),
        out_shape=jax.ShapeDtypeStruct((8, n), F32),
        compiler_params=_params(1),
        name="adaln_mods",
    )(cond, mod_w, mod_b.reshape(1, n))


def _swiglu_residual(x, mods_ref, gain_ref, wi_ref, wo_ref, mod_base):
    h = _normed_input(x, mods_ref, gain_ref, mod_base).astype(BF16)
    g = _dot(h, wi_ref[:, :FFN_DIM])
    u = _dot(h, wi_ref[:, FFN_DIM:])
    a = (g * jax.nn.sigmoid(g) * u).astype(BF16)
    gate = mods_ref[0, mod_base + 2:mod_base + 3, :]
    return x + 0.5 * gate * _dot(a, wo_ref[...])


def _ffn_kernel(x_ref, mods_ref, gain_ref, wi_ref, wo_ref, *refs):
    n_cast = len(refs) // 2
    o_ref = refs[n_cast]
    o_ref[0] = _swiglu_residual(x_ref[0], mods_ref, gain_ref, wi_ref, wo_ref, 0)
    for src, dst in zip(refs[:n_cast], refs[n_cast + 1:]):
        dst[...] = src[...].astype(BF16)


def _ffn(x, mods, gain, wi, wo, *, tm, cast=()):
    nb, l, _ = x.shape
    steps_l = l // tm
    steps = nb * steps_l
    tok = pl.BlockSpec((1, tm, D_MODEL), lambda b, i: (b, i, 0))
    slab = lambda a: pl.BlockSpec((a.shape[0] // steps, a.shape[1]),
                                  lambda b, i: (b * steps_l + i, 0))
    for a in cast:
        assert a.shape[0] % (steps * 16) == 0, "row slabs must hold whole bf16 sublane tiles"
    out = pl.pallas_call(
        _ffn_kernel,
        grid=(nb, steps_l),
        in_specs=[tok,
                  pl.BlockSpec((1, N_MOD, D_MODEL), lambda b, i: (b, 0, 0)),
                  _resident((1, D_MODEL)),
                  _resident(wi.shape), _resident(wo.shape)] + [slab(a) for a in cast],
        out_specs=[tok] + [slab(a) for a in cast],
        out_shape=[jax.ShapeDtypeStruct(x.shape, F32)]
                  + [jax.ShapeDtypeStruct(a.shape, BF16) for a in cast],
        compiler_params=_params(2),
        name="ffn",
    )(x, mods, gain, wi, wo, *cast)
    return out[0], out[1:]


TOKEN_TILE = 256
MIXER_TM = 2 * TOKEN_TILE
CTX_MIXER_TM = 4 * TOKEN_TILE


def _store_kv(k_nope, vals, kr_b, k_ref, vt_ref, rows, tile, batch=0):
    ones = jnp.ones((V_ROWS - V_HEAD_DIM, vals.shape[0]), BF16)
    for hd in range(MLA_HEADS):
        cols = slice(hd * V_HEAD_DIM, (hd + 1) * V_HEAD_DIM)
        k_ref[batch, rows, hd * QK_PAD:hd * QK_PAD + QK_NOPE_DIM] = k_nope[:, cols].astype(BF16)
        k_ref[batch, rows, hd * QK_PAD + QK_NOPE_DIM:(hd + 1) * QK_PAD] = kr_b
        vt_ref[batch, tile, hd * V_ROWS:hd * V_ROWS + V_HEAD_DIM, :] = vals[:, cols].T.astype(BF16)
        vt_ref[batch, tile, hd * V_ROWS + V_HEAD_DIM:(hd + 1) * V_ROWS, :] = ones


def _tile_scores(q_t, k_nope, kr_b):
    return [_dot(jnp.concatenate([k_nope[:, hd * V_HEAD_DIM:(hd + 1) * V_HEAD_DIM].astype(BF16),
                                  kr_b], axis=1), q_t[hd]) for hd in range(MLA_HEADS)]


def _tile_attend(scores, vals, ob_ref, rows):
    ones = jnp.ones((V_ROWS - V_HEAD_DIM, vals.shape[0]), BF16)
    for hd, s in enumerate(scores):
        cols = slice(hd * V_HEAD_DIM, (hd + 1) * V_HEAD_DIM)
        vt_h = jnp.concatenate([vals[:, cols].T.astype(BF16), ones], axis=0)
        p = jnp.exp2(s - jnp.max(s, axis=0, keepdims=True))
        acc = _dot(vt_h, p.astype(BF16))
        ot = acc[:V_HEAD_DIM] * (1.0 / acc[V_HEAD_DIM:V_HEAD_DIM + 1])
        ob_ref[0, rows, cols] = ot.T.astype(BF16)


def _mixer_tokens(x_ref, mods_ref, gain_ref, wu_ref, wv_ref, wlat_ref, wga_ref, vnorm_ref,
                  ws_ref, bs_ref, qnorm_ref, wq_ref, kvnorm_ref, wk_ref, wvup_ref, wa_ref,
                  a_ref, mixed_ref, *, tile, rope_refs, latent_refs, qkv_refs, ob_ref):
    rows = slice(tile * TOKEN_TILE, (tile + 1) * TOKEN_TILE)
    x = x_ref[0, rows, :]
    h = _normed_input(x, mods_ref, gain_ref, 3).astype(BF16)
    lat = _dot(h, wlat_ref[...])
    v = _dot(h, wv_ref[...])
    u = _dot(h, wu_ref[...])
    ga_logit = _dot(h, wga_ref[...])

    q_lat = lat[:, 0:Q_LORA_RANK]
    ckv = _rms(lat[:, Q_LORA_RANK:Q_LORA_RANK + KV_LORA_RANK], kvnorm_ref[...])
    kr_off = Q_LORA_RANK + KV_LORA_RANK
    kr = lat[:, kr_off:kr_off + ROPE_PAD]
    if latent_refs is not None:
        ckv_ref, krope_ref = latent_refs
        ckv_ref[0, rows, :] = ckv
        krope_ref[0, rows, :] = kr[:, 0:QK_ROPE_DIM]
    qn = _rms(q_lat, qnorm_ref[...]).astype(BF16)
    ckv_b = ckv.astype(BF16)
    q_all = _dot(qn, wq_ref[...])
    if rope_refs is not None:
        wqsw_ref, cos_ref, sin_ref = rope_refs
        q_sw = _dot(qn, wqsw_ref[...])
    k_nope = _dot(ckv_b, wk_ref[...])
    vals = _dot(ckv_b, wvup_ref[...])

    vv = _rms(v, vnorm_ref[...]).astype(BF16)
    n_chunks = TOKEN_TILE // CHUNK
    for g in range(GMLP_GROUPS):
        cols = slice(g * GMLP_GROUP_DIM, (g + 1) * GMLP_GROUP_DIM)
        blk = jnp.concatenate(
            [vv[c * CHUNK:(c + 1) * CHUNK, cols] for c in range(n_chunks)], axis=1)
        mix = _dot(ws_ref[g], blk)
        for c in range(n_chunks):
            mixed_ref[tile, c * CHUNK:(c + 1) * CHUNK, cols] = (
                mix[:, c * CHUNK:(c + 1) * CHUNK] + bs_ref[:, cols])
    out_a = (u * mixed_ref[tile]).astype(BF16)
    if qkv_refs is not None:
        a_proj = _dot(out_a, wa_ref[...])

    if rope_refs is not None:
        cos = cos_ref[rows, :]
        sin = sin_ref[rows, :]
        kr = kr * cos + lat[:, kr_off + ROPE_PAD:kr_off + 2 * ROPE_PAD] * sin
    q_t = []
    for hd in range(MLA_HEADS):
        nope = q_all[:, hd * QK_PAD:hd * QK_PAD + QK_NOPE_DIM]
        rot = q_all[:, hd * QK_PAD + QK_NOPE_DIM:(hd + 1) * QK_PAD]
        if rope_refs is not None:
            rot = rot * cos + q_sw[:, hd * ROPE_PAD:(hd + 1) * ROPE_PAD] * sin
        nope_t = (nope * Q_SCALE).T.astype(BF16)
        rot_t = (rot * Q_SCALE).T.astype(BF16)
        if qkv_refs is not None:
            qkv_refs[0][0, tile, hd * QK_PAD:hd * QK_PAD + QK_NOPE_DIM, :] = nope_t
            qkv_refs[0][0, tile, hd * QK_PAD + QK_NOPE_DIM:(hd + 1) * QK_PAD, :] = rot_t
        else:
            q_t.append(jnp.concatenate([nope_t, rot_t], axis=0))
    if qkv_refs is not None:
        _store_kv(k_nope, vals, kr.astype(BF16), qkv_refs[1], qkv_refs[2], rows, tile)
    else:
        scores = _tile_scores(q_t, k_nope, kr.astype(BF16))
        a_proj = _dot(out_a, wa_ref[...])
    a_ref[0, rows, :] = _sigmoid(ga_logit) * a_proj
    if qkv_refs is None:
        return lambda: _tile_attend(scores, vals, ob_ref, rows)


def _mixer_ctx_kernel(x_ref, mods_ref, gain_ref, wu_ref, wv_ref, wlat_ref, wga_ref, vnorm_ref,
                      ws_ref, bs_ref, qnorm_ref, wq_ref, kvnorm_ref, wk_ref, wvup_ref, wa_ref,
                      a_ref, ob_ref, ckv_ref, krope_ref, mixed_ref):
    attend = [
        _mixer_tokens(x_ref, mods_ref, gain_ref, wu_ref, wv_ref, wlat_ref, wga_ref, vnorm_ref,
                      ws_ref, bs_ref, qnorm_ref, wq_ref, kvnorm_ref, wk_ref, wvup_ref, wa_ref,
                      a_ref, mixed_ref, tile=tile, rope_refs=None,
                      latent_refs=(ckv_ref, krope_ref), qkv_refs=None, ob_ref=ob_ref)
        for tile in range(x_ref.shape[1] // TOKEN_TILE)]
    for finish in attend:
        finish()


def _mixer_lat_kernel(x_ref, mods_ref, gain_ref, wu_ref, wv_ref, wlat_ref, wga_ref, vnorm_ref,
                      ws_ref, bs_ref, qnorm_ref, wq_ref, kvnorm_ref, wk_ref, wvup_ref, wa_ref,
                      wqsw_ref, cos_ref, sin_ref, a_ref, qt_ref, k_ref, vt_ref, mixed_ref):
    for tile in range(x_ref.shape[1] // TOKEN_TILE):
        _mixer_tokens(x_ref, mods_ref, gain_ref, wu_ref, wv_ref, wlat_ref, wga_ref, vnorm_ref,
                      ws_ref, bs_ref, qnorm_ref, wq_ref, kvnorm_ref, wk_ref, wvup_ref, wa_ref,
                      a_ref, mixed_ref, tile=tile, rope_refs=(wqsw_ref, cos_ref, sin_ref),
                      latent_refs=None, qkv_refs=(qt_ref, k_ref, vt_ref), ob_ref=None)


def _cache_kv_kernel(ckv_ref, kr_ref, wk_ref, wvup_ref, k_ref, vt_ref):
    for b in range(ckv_ref.shape[0]):
        ckv_b = ckv_ref[b].astype(BF16)
        _store_kv(_dot(ckv_b, wk_ref[...]), _dot(ckv_b, wvup_ref[...]), kr_ref[b].astype(BF16),
                  k_ref, vt_ref, slice(None), 0, batch=b)


def _mixer_weights(w):
    return [w['norm_mix'], w['w_u'], w['w_v'], w['w_lat'], w['w_ga'], w['v_norm'], w['w_s'],
            w['b_s'], w['q_norm'], w['w_q'], w['kv_norm'], w['w_k'], w['w_vup'], w['w_a']]


def _mixer_ctx(x, mods, w, *, tm):
    nb, l, _ = x.shape
    tok = lambda width: pl.BlockSpec((1, tm, width), lambda b, i: (b, i, 0))
    weights = _mixer_weights(w)
    return pl.pallas_call(
        _mixer_ctx_kernel,
        grid=(nb, l // tm),
        in_specs=[tok(D_MODEL), pl.BlockSpec((1, N_MOD, D_MODEL), lambda b, i: (b, 0, 0))]
                 + [_resident(a.shape) for a in weights],
        out_specs=[tok(D_MODEL), tok(MLA_HEADS * V_HEAD_DIM), tok(KV_LORA_RANK), tok(QK_ROPE_DIM)],
        out_shape=[jax.ShapeDtypeStruct((nb, l, D_MODEL), F32),
                   jax.ShapeDtypeStruct((nb, l, MLA_HEADS * V_HEAD_DIM), BF16),
                   jax.ShapeDtypeStruct((nb, l, KV_LORA_RANK), F32),
                   jax.ShapeDtypeStruct((nb, l, QK_ROPE_DIM), F32)],
        scratch_shapes=[pltpu.VMEM((tm // TOKEN_TILE, TOKEN_TILE, GMLP_DIM), F32)],
        compiler_params=_params(2),
        name="mixer_ctx",
    )(x, mods, *weights)


def _mixer_lat(x, mods, w, cos, sin, *, tm):
    nb, l, _ = x.shape
    tiles = tm // TOKEN_TILE
    tok = lambda width: pl.BlockSpec((1, tm, width), lambda b, i: (b, i, 0))
    tiled = lambda height: pl.BlockSpec((1, tiles, height, TOKEN_TILE), lambda b, i: (b, i, 0, 0))
    table = pl.BlockSpec((tm, ROPE_PAD), lambda b, i: (i, 0))
    weights = _mixer_weights(w) + [w['w_q_sw']]
    return pl.pallas_call(
        _mixer_lat_kernel,
        grid=(nb, l // tm),
        in_specs=[tok(D_MODEL), pl.BlockSpec((1, N_MOD, D_MODEL), lambda b, i: (b, 0, 0))]
                 + [_resident(a.shape) for a in weights] + [table, table],
        out_specs=[tok(D_MODEL), tiled(MLA_HEADS * QK_PAD), tok(MLA_HEADS * QK_PAD),
                   tiled(MLA_HEADS * V_ROWS)],
        out_shape=[jax.ShapeDtypeStruct((nb, l, D_MODEL), F32),
                   jax.ShapeDtypeStruct((nb, l // TOKEN_TILE, MLA_HEADS * QK_PAD, TOKEN_TILE), BF16),
                   jax.ShapeDtypeStruct((nb, l, MLA_HEADS * QK_PAD), BF16),
                   jax.ShapeDtypeStruct((nb, l // TOKEN_TILE, MLA_HEADS * V_ROWS, TOKEN_TILE), BF16)],
        scratch_shapes=[pltpu.VMEM((tiles, TOKEN_TILE, GMLP_DIM), F32)],
        compiler_params=_params(2),
        name="mixer_lat",
    )(x, mods, *weights, cos, sin)


def _cache_kv(ckv, kr_pad, wk, wvup):
    nb, past, _ = ckv.shape
    assert past == TOKEN_TILE, "the cached context must fill exactly one layout tile"
    return pl.pallas_call(
        _cache_kv_kernel,
        grid=(1,),
        in_specs=[pl.BlockSpec((nb, past, KV_LORA_RANK), lambda i: (0, 0, 0)),
                  pl.BlockSpec((nb, past, ROPE_PAD), lambda i: (0, 0, 0)),
                  _resident(wk.shape), _resident(wvup.shape)],
        out_specs=[pl.BlockSpec((nb, past, MLA_HEADS * QK_PAD), lambda i: (0, 0, 0)),
                   pl.BlockSpec((nb, 1, MLA_HEADS * V_ROWS, past), lambda i: (0, 0, 0, 0))],
        out_shape=[jax.ShapeDtypeStruct((nb, past, MLA_HEADS * QK_PAD), BF16),
                   jax.ShapeDtypeStruct((nb, 1, MLA_HEADS * V_ROWS, past), BF16)],
        compiler_params=_params(1),
        name="cache_kv",
    )(ckv, kr_pad, wk, wvup)


SCORES_AHEAD = 2


def _attn_kernel(qt_ref, kc_ref, vtc_ref, k_ref, vt_ref, o_ref, *, heads, tk):
    chunks = [(kc_ref, vtc_ref, 0, kc_ref.shape[1])]
    chunks += [(k_ref, vt_ref, lo, lo + tk) for lo in range(0, k_ref.shape[1], tk)]
    n_chunks = len(chunks)
    streams = [(j, t) for j in range(heads) for t in range(qt_ref.shape[1])]
    qts = [qt_ref[0, t, j * QK_PAD:(j + 1) * QK_PAD, :] for j, t in streams]

    def scores(i, c):
        j = streams[i][0]
        keys, _, lo, hi = chunks[c]
        return _dot(keys[0, lo:hi, j * QK_PAD:(j + 1) * QK_PAD], qts[i])

    def values(j, c, p):
        _, vt, lo, hi = chunks[c]
        acc = None
        for t in range(lo // TOKEN_TILE, hi // TOKEN_TILE):
            part = _dot(vt[0, t, j * V_ROWS:(j + 1) * V_ROWS, :],
                        p[t * TOKEN_TILE - lo:(t + 1) * TOKEN_TILE - lo])
            acc = part if acc is None else acc + part
        return acc

    pending = [[scores(i, c) for c in range(min(SCORES_AHEAD, n_chunks))]
               for i in range(len(streams))]
    state = [None] * len(streams)
    for c in range(n_chunks):
        for i, (j, _) in enumerate(streams):
            if c + SCORES_AHEAD < n_chunks:
                pending[i].append(scores(i, c + SCORES_AHEAD))
            s = pending[i].pop(0)
            m_c = jnp.max(s, axis=0, keepdims=True)
            if c == 0:
                p = jnp.exp2(s - m_c)
                state[i] = (m_c, values(j, c, p.astype(BF16)))
            else:
                m, acc = state[i]
                m_new = jnp.maximum(m, m_c)
                alpha = jnp.exp2(m - m_new)
                p = jnp.exp2(s - m_new)
                state[i] = (m_new, alpha * acc + values(j, c, p.astype(BF16)))
    for i, (j, t) in enumerate(streams):
        q0 = t * TOKEN_TILE
        _, acc = state[i]
        ot = acc[:V_HEAD_DIM] * (1.0 / acc[V_HEAD_DIM:V_HEAD_DIM + 1])
        o_ref[0, q0:q0 + TOKEN_TILE, j * V_HEAD_DIM:(j + 1) * V_HEAD_DIM] = ot.T.astype(BF16)


def _attention(qt, k_cache, vt_cache, k, vt, *, heads, tq, tk):
    nb, q_tiles = qt.shape[:2]
    seq_q = q_tiles * TOKEN_TILE
    past, seq_k = k_cache.shape[1], k.shape[1]
    return pl.pallas_call(
        functools.partial(_attn_kernel, heads=heads, tk=tk),
        grid=(nb, MLA_HEADS // heads, seq_q // tq),
        in_specs=[pl.BlockSpec((1, tq // TOKEN_TILE, heads * QK_PAD, TOKEN_TILE),
                               lambda b, h, i: (b, i, h, 0)),
                  pl.BlockSpec((1, past, heads * QK_PAD), lambda b, h, i: (b, 0, h)),
                  pl.BlockSpec((1, past // TOKEN_TILE, heads * V_ROWS, TOKEN_TILE),
                               lambda b, h, i: (b, 0, h, 0)),
                  pl.BlockSpec((1, seq_k, heads * QK_PAD), lambda b, h, i: (b, 0, h)),
                  pl.BlockSpec((1, seq_k // TOKEN_TILE, heads * V_ROWS, TOKEN_TILE),
                               lambda b, h, i: (b, 0, h, 0))],
        out_specs=pl.BlockSpec((1, tq, heads * V_HEAD_DIM), lambda b, h, i: (b, i, h)),
        out_shape=jax.ShapeDtypeStruct((nb, seq_q, MLA_HEADS * V_HEAD_DIM), BF16),
        compiler_params=_params(3),
        name="attention",
    )(qt, k_cache, vt_cache, k, vt)


def _merge_ffn_kernel(x_ref, a_ref, ob_ref, mods_ref, gmix_ref, wgb_ref, wb_ref, wo_ref,
                      gffn_ref, wi_hbm, wo2_hbm, fin_ref, o_ref, wi_ref, wo2_ref, sems, *,
                      final_norm):
    first = jnp.logical_and(pl.program_id(0) == 0, pl.program_id(1) == 0)
    copies = (pltpu.make_async_copy(wi_hbm, wi_ref, sems.at[0]),
              pltpu.make_async_copy(wo2_hbm, wo2_ref, sems.at[1]))

    @pl.when(first)
    def _():
        for cp in copies:
            cp.start()

    tm = x_ref.shape[1]
    halves = [slice(0, tm // 2), slice(tm // 2, tm)]
    b_proj = [_dot(ob_ref[0, r, :], wb_ref[...]) for r in halves]
    merged_x = []
    for r, b in zip(halves, b_proj):
        x = x_ref[0, r, :]
        h = _normed_input(x, mods_ref, gmix_ref, 3).astype(BF16)
        gb = _sigmoid(_dot(h, wgb_ref[...]))
        merged = (a_ref[0, r, :] + gb * b).astype(BF16)
        merged_x.append(x + mods_ref[0, 5:6, :] * _dot(merged, wo_ref[...]))

    @pl.when(first)
    def _():
        for cp in copies:
            cp.wait()

    for r, x in zip(halves, merged_x):
        out = _swiglu_residual(x, mods_ref, gffn_ref, wi_ref, wo2_ref, 6)
        if final_norm:
            out = _rms(out, fin_ref[...])
        o_ref[0, r, :] = out


def _merge_ffn(x, a_part, out_b, mods, gmix, wgb, wb, wo, gffn, wi, wo2, fin, *, final_norm, tm):
    nb, l, _ = x.shape
    tok = pl.BlockSpec((1, tm, D_MODEL), lambda b, i: (b, i, 0))
    in_hbm = pl.BlockSpec(memory_space=pl.ANY)
    return pl.pallas_call(
        functools.partial(_merge_ffn_kernel, final_norm=final_norm),
        grid=(nb, l // tm),
        in_specs=[tok, tok, tok,
                  pl.BlockSpec((1, N_MOD, D_MODEL), lambda b, i: (b, 0, 0)),
                  _resident((1, D_MODEL)),
                  _resident(wgb.shape), _resident(wb.shape), _resident(wo.shape),
                  _resident((1, D_MODEL)), in_hbm, in_hbm,
                  _resident((1, D_MODEL))],
        out_specs=tok,
        out_shape=jax.ShapeDtypeStruct(x.shape, F32),
        scratch_shapes=[pltpu.VMEM(wi.shape, BF16), pltpu.VMEM(wo2.shape, BF16),
                        pltpu.SemaphoreType.DMA((2,))],
        compiler_params=pltpu.CompilerParams(dimension_semantics=("arbitrary", "arbitrary"),
                                             vmem_limit_bytes=V7X_VMEM_LIMIT),
        name="merge_ffn",
    )(x, a_part, out_b, mods, gmix, wgb, wb, wo, gffn, wi, wo2, fin)


def _rope_tables(l):
    rows = l // GRID_W
    r = np.repeat(np.arange(rows, dtype=np.float32), GRID_W)
    col = np.tile(np.arange(GRID_W, dtype=np.float32), rows)
    half = QK_ROPE_DIM // 2
    inv = (1.0 / (np.float32(ROPE_BASE) ** (np.arange(0, half, 2, dtype=np.float32) / half))
           ).astype(np.float32)
    ang_r, ang_c = r[:, None] * inv, col[:, None] * inv
    pad = ROPE_PAD - QK_ROPE_DIM
    cos = np.concatenate([np.cos(ang_r), np.cos(ang_r), np.cos(ang_c), np.cos(ang_c),
                          np.ones((l, pad), np.float32)], axis=1)
    sin = np.concatenate([-np.sin(ang_r), np.sin(ang_r), -np.sin(ang_c), np.sin(ang_c),
                          np.zeros((l, pad), np.float32)], axis=1)
    return jnp.asarray(cos, F32), jnp.asarray(sin, F32)


def _layer_weights(l, norm_ffn1, ffn1_w_in, ffn1_w_out, norm_mix, w_in, gmlp_v_norm, gmlp_w_s,
                   gmlp_b_s, q_norm, w_q_up, kv_norm, w_kv_up, norm_ffn2):
    bf = lambda a: a.astype(BF16)
    row = lambda a: a.reshape(1, -1)
    swap = np.arange(QK_ROPE_DIM) ^ (QK_ROPE_DIM // 4)
    win = w_in[l]
    o_v, o_q, o_kv, o_kr, o_ga = np.cumsum(
        [GMLP_DIM, GMLP_DIM, Q_LORA_RANK, KV_LORA_RANK, QK_ROPE_DIM]).tolist()
    o_gb = o_ga + D_MODEL
    kr_w = win[:, o_kr:o_ga]
    pad_kr = jnp.zeros((D_MODEL, ROPE_PAD - QK_ROPE_DIM), F32)
    w_lat = jnp.concatenate([win[:, o_q:o_kr], kr_w, pad_kr, kr_w[:, swap], pad_kr], axis=1)

    wq3 = w_q_up[l].reshape(Q_LORA_RANK, MLA_HEADS, QK_NOPE_DIM + QK_ROPE_DIM)
    pad_q = jnp.zeros((Q_LORA_RANK, MLA_HEADS, ROPE_PAD - QK_ROPE_DIM), F32)
    w_q = jnp.concatenate([wq3, pad_q], axis=2).reshape(Q_LORA_RANK, MLA_HEADS * QK_PAD)
    w_q_sw = jnp.concatenate([wq3[:, :, QK_NOPE_DIM:][:, :, swap], pad_q], axis=2).reshape(
        Q_LORA_RANK, MLA_HEADS * ROPE_PAD)
    wkv3 = w_kv_up[l].reshape(KV_LORA_RANK, MLA_HEADS, QK_NOPE_DIM + V_HEAD_DIM)
    return {
        'norm_ffn1': row(norm_ffn1[l]),
        'ffn1_i': bf(ffn1_w_in[l]),
        'ffn1_o': bf(ffn1_w_out[l]),
        'norm_mix': row(norm_mix[l]),
        'w_u': bf(win[:, :GMLP_DIM]), 'w_v': bf(win[:, o_v:o_q]), 'w_lat': bf(w_lat),
        'w_ga': bf(win[:, o_ga:o_gb]), 'w_gb': bf(win[:, o_gb:]),
        'v_norm': row(gmlp_v_norm[l]),
        'w_s': bf(gmlp_w_s[l]),
        'b_s': jnp.repeat(gmlp_b_s[l], GMLP_GROUP_DIM, axis=1),
        'q_norm': row(q_norm[l]), 'w_q': bf(w_q), 'w_q_sw': bf(w_q_sw),
        'kv_norm': row(kv_norm[l]),
        'w_k': bf(wkv3[:, :, :QK_NOPE_DIM].reshape(KV_LORA_RANK, -1)),
        'w_vup': bf(wkv3[:, :, QK_NOPE_DIM:].reshape(KV_LORA_RANK, -1)),
        'norm_ffn2': row(norm_ffn2[l]),
    }


def kernel(x_prompt, x_sample, c, cache_ckv, cache_krope, c_ctx, mod_w, mod_b, norm_ffn1, ffn1_w_in, ffn1_w_out, norm_mix, w_in, gmlp_v_norm, gmlp_w_s, gmlp_b_s, q_norm, w_q_up, kv_norm, w_kv_up, w_a_proj, w_b_proj, w_o, norm_ffn2, ffn2_w_in, ffn2_w_out, norm_final):
    batch, seq, _ = x_prompt.shape
    dec_batch, dec_seq, _ = x_sample.shape
    depth = mod_w.shape[0]
    assert seq == TOKEN_TILE, "each mixer sub-tile of the context stream must be one sequence"
    fin = norm_final.reshape(1, D_MODEL)
    cos, sin = _rope_tables(dec_seq)
    cond = jnp.concatenate([c_ctx[None, :], c, jnp.zeros((8 - 1 - dec_batch, D_MODEL), F32)], axis=0)

    xp = x_prompt.reshape(1, batch * seq, D_MODEL)
    xs = x_sample
    ckv_list, krope_list = [], []
    for l in range(depth):
        w = _layer_weights(l, norm_ffn1, ffn1_w_in, ffn1_w_out, norm_mix, w_in, gmlp_v_norm,
                           gmlp_w_s, gmlp_b_s, q_norm, w_q_up, kv_norm, w_kv_up, norm_ffn2)
        mods = _mods(cond, mod_w[l], mod_b[l]).reshape(8, N_MOD, D_MODEL)
        mods_ctx, mods_lat = mods[0:1], mods[1:1 + dec_batch]
        last = l == depth - 1
        kr_pad = jnp.pad(cache_krope[:, l], ((0, 0), (0, 0), (0, ROPE_PAD - QK_ROPE_DIM)))

        xp, (w['w_a'], w['w_b'], w['w_o']) = _ffn(
            xp, mods_ctx, w['norm_ffn1'], w['ffn1_i'], w['ffn1_o'], tm=FFN_TM,
            cast=(w_a_proj[l], w_b_proj[l], w_o[l]))
        xs, (w['ffn2_i'], w['ffn2_o']) = _ffn(
            xs, mods_lat, w['norm_ffn1'], w['ffn1_i'], w['ffn1_o'], tm=FFN_TM,
            cast=(ffn2_w_in[l], ffn2_w_out[l]))

        a_ctx, ob_ctx, ckv_l, krope_l = _mixer_ctx(xp, mods_ctx, w, tm=CTX_MIXER_TM)
        a_lat, qt, k, vt = _mixer_lat(xs, mods_lat, w, cos, sin, tm=MIXER_TM)
        k_cache, vt_cache = _cache_kv(cache_ckv[:, l], kr_pad, w['w_k'], w['w_vup'])
        ob_lat = _attention(qt, k_cache, vt_cache, k, vt, heads=1, tq=ATTN_TQ, tk=ATTN_TK)

        merge = lambda x, a_part, out_b, m: _merge_ffn(
            x, a_part, out_b, m, w['norm_mix'], w['w_gb'], w['w_b'], w['w_o'],
            w['norm_ffn2'], w['ffn2_i'], w['ffn2_o'], fin, final_norm=last, tm=MERGE_TM)
        xp = merge(xp, a_ctx, ob_ctx, mods_ctx)
        xs = merge(xs, a_lat, ob_lat, mods_lat)
        ckv_list.append(ckv_l.reshape(batch, seq, KV_LORA_RANK))
        krope_list.append(krope_l.reshape(batch, seq, QK_ROPE_DIM))
    y_prompt = xp.reshape(batch, seq, D_MODEL)
    new_ckv = jnp.stack(ckv_list, axis=1)
    new_krope = jnp.stack(krope_list, axis=1)
    return (y_prompt, xs, new_ckv, new_krope)
```

```python
import functools
import math

import numpy as np
import jax
import jax.numpy as jnp
from jax import lax
from jax.experimental import pallas as pl
from jax.experimental.pallas import tpu as pltpu

D_MODEL = 1024
GRID_W = 64
FFN_DIM = 2816
GMLP_GROUPS = 8
GMLP_GROUP_DIM = 128
GMLP_DIM = GMLP_GROUPS * GMLP_GROUP_DIM
CHUNK = 128
MLA_HEADS = 8
QK_NOPE_DIM = 128
QK_ROPE_DIM = 64
V_HEAD_DIM = 128
Q_LORA_RANK = 256
KV_LORA_RANK = 256
ROPE_BASE = 10000.0
N_MOD = 9
EPS = 1e-6

V_ROWS = 144
QK_PAD = 256
ROPE_PAD = QK_PAD - QK_NOPE_DIM
V7X_VMEM_LIMIT = 56 * 1024 * 1024
FFN_TM = 1024
MERGE_TM = 512
ATTN_TQ = 2048
ATTN_TK = 512
MODS_TN = 3 * D_MODEL
Q_SCALE = (QK_NOPE_DIM + QK_ROPE_DIM) ** -0.5 * math.log2(math.e)

F32 = jnp.float32
BF16 = jnp.bfloat16


def _dot(a, b):
    return jnp.dot(a, b, preferred_element_type=F32)


def _rms(x, g):
    return x * lax.rsqrt(jnp.mean(x * x, axis=-1, keepdims=True) + EPS) * g


def _sigmoid(x):
    return 0.5 * jnp.tanh(0.5 * x) + 0.5


def _normed_input(x, mods_ref, gain_ref, base):
    shift = mods_ref[0, base:base + 1, :]
    scale = mods_ref[0, base + 1:base + 2, :]
    return _rms(x, gain_ref[...]) * (1.0 + scale) + shift


def _resident(shape):
    nd = len(shape)
    return pl.BlockSpec(shape, lambda *_: (0,) * nd, pipeline_mode=pl.Buffered(1))


def _params(n_grid):
    return pltpu.CompilerParams(dimension_semantics=("parallel",) * n_grid,
                                vmem_limit_bytes=V7X_VMEM_LIMIT)


def _mods_kernel(c_ref, w_ref, b_ref, o_ref):
    c = c_ref[...]
    s = (c * jax.nn.sigmoid(c)).astype(BF16)
    o_ref[...] = _dot(s, w_ref[...].astype(BF16)) + b_ref[...]


def _mods(cond, mod_w, mod_b):
    n = mod_w.shape[1]
    tn = MODS_TN
    return pl.pallas_call(
        _mods_kernel,
        grid=(n // tn,),
        in_specs=[pl.BlockSpec((8, D_MODEL), lambda j: (0, 0)),
                  pl.BlockSpec((D_MODEL, tn), lambda j: (0, j)),
                  pl.BlockSpec((1, tn), lambda j: (0, j))],
        out_specs=pl.BlockSpec((8, tn), lambda j: (0, j)),
        out_shape=jax.ShapeDtypeStruct((8, n), F32),
        compiler_params=_params(1),
        name="adaln_mods",
    )(cond, mod_w, mod_b.reshape(1, n))


def _swiglu_residual(x, mods_ref, gain_ref, wi_ref, wo_ref, mod_base):
    h = _normed_input(x, mods_ref, gain_ref, mod_base).astype(BF16)
    g = _dot(h, wi_ref[:, :FFN_DIM])
    u = _dot(h, wi_ref[:, FFN_DIM:])
    a = (g * jax.nn.sigmoid(g) * u).astype(BF16)
    gate = mods_ref[0, mod_base + 2:mod_base + 3, :]
    return x + 0.5 * gate * _dot(a, wo_ref[...])


def _ffn_kernel(x_ref, mods_ref, gain_ref, wi_ref, wo_ref, *refs, n_cast, t_parts):
    n_tin = sum(t_parts)
    cast_in, tr_in = refs[:n_cast], refs[n_cast:n_cast + n_tin]
    o_ref = refs[n_cast + n_tin]
    cast_out = refs[n_cast + n_tin + 1:n_cast + n_tin + 1 + n_cast]
    tr_out = refs[n_cast + n_tin + 1 + n_cast:]
    o_ref[0] = _swiglu_residual(x_ref[0], mods_ref, gain_ref, wi_ref, wo_ref, 0)
    for src, dst in zip(cast_in, cast_out):
        dst[...] = src[...].astype(BF16)
    at = 0
    for parts, dst in zip(t_parts, tr_out):
        rows = [r[...] for r in tr_in[at:at + parts]]
        at += parts
        slab = rows[0] if parts == 1 else jnp.concatenate(rows, axis=0)
        dst[...] = slab.T.astype(BF16)


HALF_SLAB = 64


def _ffn(x, mods, gain, wi, wo, *, tm, cast=(), transposed=None):
    nb, l, _ = x.shape
    steps_l = l // tm
    steps = nb * steps_l
    step = lambda b, i: b * steps_l + i
    tok = pl.BlockSpec((1, tm, D_MODEL), lambda b, i: (b, i, 0))
    slab = lambda a: pl.BlockSpec((a.shape[0] // steps, a.shape[1]), lambda b, i: (step(b, i), 0))
    for a in cast:
        assert a.shape[0] % (steps * 16) == 0, "row slabs must hold whole bf16 sublane tiles"
    tr_in, tr_specs, t_parts, tr_out_specs, tr_out_shapes = [], [], [], [], []
    if transposed is not None:
        wt, starts = transposed
        per_step = D_MODEL // steps
        assert per_step == 2 * HALF_SLAB, "one 128-column block of each transposed piece per step"
        for start in starts:
            assert start % HALF_SLAB == 0
            if start % per_step == 0:
                blocks = [(per_step, start // per_step, 1)]
            else:
                blocks = [(HALF_SLAB, start // HALF_SLAB + k, 2) for k in range(2)]
            for rows, first, stride in blocks:
                tr_in.append(wt)
                tr_specs.append(pl.BlockSpec(
                    (rows, wt.shape[1]),
                    lambda b, i, first=first, stride=stride: (first + stride * step(b, i), 0)))
            t_parts.append(len(blocks))
            tr_out_specs.append(pl.BlockSpec((wt.shape[1], per_step), lambda b, i: (0, step(b, i))))
            tr_out_shapes.append(jax.ShapeDtypeStruct((wt.shape[1], D_MODEL), BF16))
    out = pl.pallas_call(
        functools.partial(_ffn_kernel, n_cast=len(cast), t_parts=tuple(t_parts)),
        grid=(nb, steps_l),
        in_specs=[tok,
                  pl.BlockSpec((1, N_MOD, D_MODEL), lambda b, i: (b, 0, 0)),
                  _resident((1, D_MODEL)),
                  _resident(wi.shape), _resident(wo.shape)] + [slab(a) for a in cast] + tr_specs,
        out_specs=[tok] + [slab(a) for a in cast] + tr_out_specs,
        out_shape=[jax.ShapeDtypeStruct(x.shape, F32)]
                  + [jax.ShapeDtypeStruct(a.shape, BF16) for a in cast] + tr_out_shapes,
        compiler_params=_params(2),
        name="ffn",
    )(x, mods, gain, wi, wo, *cast, *tr_in)
    return out[0], out[1:1 + len(cast)], out[1 + len(cast):]


TOKEN_TILE = 256
MIXER_TM = 2 * TOKEN_TILE
CTX_MIXER_TM = 4 * TOKEN_TILE


def _store_kv(k_nope, vals, kr_b, k_ref, vt_ref, rows, tile, batch=0):
    ones = jnp.ones((V_ROWS - V_HEAD_DIM, vals.shape[0]), BF16)
    for hd in range(MLA_HEADS):
        cols = slice(hd * V_HEAD_DIM, (hd + 1) * V_HEAD_DIM)
        k_ref[batch, rows, hd * QK_PAD:hd * QK_PAD + QK_NOPE_DIM] = k_nope[:, cols].astype(BF16)
        k_ref[batch, rows, hd * QK_PAD + QK_NOPE_DIM:(hd + 1) * QK_PAD] = kr_b
        vt_ref[batch, tile, hd * V_ROWS:hd * V_ROWS + V_HEAD_DIM, :] = vals[:, cols].T.astype(BF16)
        vt_ref[batch, tile, hd * V_ROWS + V_HEAD_DIM:(hd + 1) * V_ROWS, :] = ones


def _tile_scores(q_t, k_nope, kr_b):
    return [_dot(jnp.concatenate([k_nope[:, hd * V_HEAD_DIM:(hd + 1) * V_HEAD_DIM].astype(BF16),
                                  kr_b], axis=1), q_t[hd]) for hd in range(MLA_HEADS)]


def _tile_attend(scores, vals, ob_ref, rows):
    ones = jnp.ones((V_ROWS - V_HEAD_DIM, vals.shape[0]), BF16)
    for hd, s in enumerate(scores):
        cols = slice(hd * V_HEAD_DIM, (hd + 1) * V_HEAD_DIM)
        vt_h = jnp.concatenate([vals[:, cols].T.astype(BF16), ones], axis=0)
        p = jnp.exp2(s - jnp.max(s, axis=0, keepdims=True))
        acc = _dot(vt_h, p.astype(BF16))
        ot = acc[:V_HEAD_DIM] * (1.0 / acc[V_HEAD_DIM:V_HEAD_DIM + 1])
        ob_ref[0, rows, cols] = ot.T.astype(BF16)


def _mixer_tokens(x_ref, mods_ref, gain_ref, wu_ref, wv_ref, wlat_ref, wga_ref, vnorm_ref,
                  ws_ref, bs_ref, qnorm_ref, wq_ref, kvnorm_ref, wk_ref, wvup_ref, wa_ref,
                  a_ref, mixed_ref, *, tile, rope_refs, latent_refs, qkv_refs, ob_ref):
    rows = slice(tile * TOKEN_TILE, (tile + 1) * TOKEN_TILE)
    x = x_ref[0, rows, :]
    h = _normed_input(x, mods_ref, gain_ref, 3).astype(BF16)
    lat = _dot(h, wlat_ref[...])
    v = _dot(h, wv_ref[...])
    u = _dot(h, wu_ref[...])
    ga_logit = _dot(h, wga_ref[...])

    q_lat = lat[:, 0:Q_LORA_RANK]
    ckv = _rms(lat[:, Q_LORA_RANK:Q_LORA_RANK + KV_LORA_RANK], kvnorm_ref[...])
    kr_off = Q_LORA_RANK + KV_LORA_RANK
    kr = lat[:, kr_off:kr_off + ROPE_PAD]
    if latent_refs is not None:
        ckv_ref, krope_ref = latent_refs
        ckv_ref[0, rows, :] = ckv
        krope_ref[0, rows, :] = kr[:, 0:QK_ROPE_DIM]
    qn = _rms(q_lat, qnorm_ref[...]).astype(BF16)
    ckv_b = ckv.astype(BF16)
    q_all = _dot(qn, wq_ref[...])
    if rope_refs is not None:
        wqsw_ref, cos_ref, sin_ref = rope_refs
        q_sw = _dot(qn, wqsw_ref[...])
    k_nope = _dot(ckv_b, wk_ref[...])
    vals = _dot(ckv_b, wvup_ref[...])

    vv = _rms(v, vnorm_ref[...]).astype(BF16)
    n_chunks = TOKEN_TILE // CHUNK
    for g in range(GMLP_GROUPS):
        cols = slice(g * GMLP_GROUP_DIM, (g + 1) * GMLP_GROUP_DIM)
        blk = jnp.concatenate(
            [vv[c * CHUNK:(c + 1) * CHUNK, cols] for c in range(n_chunks)], axis=1)
        mix = _dot(ws_ref[g], blk)
        for c in range(n_chunks):
            mixed_ref[tile, c * CHUNK:(c + 1) * CHUNK, cols] = (
                mix[:, c * CHUNK:(c + 1) * CHUNK] + bs_ref[:, cols])
    out_a = (u * mixed_ref[tile]).astype(BF16)
    if qkv_refs is not None:
        a_proj = _dot(out_a, wa_ref[...])

    if rope_refs is not None:
        cos = cos_ref[rows, :]
        sin = sin_ref[rows, :]
        kr = kr * cos + lat[:, kr_off + ROPE_PAD:kr_off + 2 * ROPE_PAD] * sin
    q_t = []
    for hd in range(MLA_HEADS):
        nope = q_all[:, hd * QK_PAD:hd * QK_PAD + QK_NOPE_DIM]
        rot = q_all[:, hd * QK_PAD + QK_NOPE_DIM:(hd + 1) * QK_PAD]
        if rope_refs is not None:
            rot = rot * cos + q_sw[:, hd * ROPE_PAD:(hd + 1) * ROPE_PAD] * sin
        nope_t = (nope * Q_SCALE).T.astype(BF16)
        rot_t = (rot * Q_SCALE).T.astype(BF16)
        if qkv_refs is not None:
            qkv_refs[0][0, tile, hd * QK_PAD:hd * QK_PAD + QK_NOPE_DIM, :] = nope_t
            qkv_refs[0][0, tile, hd * QK_PAD + QK_NOPE_DIM:(hd + 1) * QK_PAD, :] = rot_t
        else:
            q_t.append(jnp.concatenate([nope_t, rot_t], axis=0))
    if qkv_refs is not None:
        _store_kv(k_nope, vals, kr.astype(BF16), qkv_refs[1], qkv_refs[2], rows, tile)
    else:
        scores = _tile_scores(q_t, k_nope, kr.astype(BF16))
        a_proj = _dot(out_a, wa_ref[...])
    a_ref[0, rows, :] = _sigmoid(ga_logit) * a_proj
    if qkv_refs is None:
        return lambda: _tile_attend(scores, vals, ob_ref, rows)


def _mixer_ctx_kernel(x_ref, mods_ref, gain_ref, wu_ref, wv_ref, wlat_ref, wga_ref, vnorm_ref,
                      ws_ref, bs_ref, qnorm_ref, wq_ref, kvnorm_ref, wk_ref, wvup_ref, wa_ref,
                      a_ref, ob_ref, ckv_ref, krope_ref, mixed_ref):
    attend = [
        _mixer_tokens(x_ref, mods_ref, gain_ref, wu_ref, wv_ref, wlat_ref, wga_ref, vnorm_ref,
                      ws_ref, bs_ref, qnorm_ref, wq_ref, kvnorm_ref, wk_ref, wvup_ref, wa_ref,
                      a_ref, mixed_ref, tile=tile, rope_refs=None,
                      latent_refs=(ckv_ref, krope_ref), qkv_refs=None, ob_ref=ob_ref)
        for tile in range(x_ref.shape[1] // TOKEN_TILE)]
    for finish in attend:
        finish()


def _mixer_lat_kernel(x_ref, mods_ref, gain_ref, wu_ref, wv_ref, wlat_ref, wga_ref, vnorm_ref,
                      ws_ref, bs_ref, qnorm_ref, wq_ref, kvnorm_ref, wk_ref, wvup_ref, wa_ref,
                      wqsw_ref, cos_ref, sin_ref, a_ref, qt_ref, k_ref, vt_ref, mixed_ref):
    for tile in range(x_ref.shape[1] // TOKEN_TILE):
        _mixer_tokens(x_ref, mods_ref, gain_ref, wu_ref, wv_ref, wlat_ref, wga_ref, vnorm_ref,
                      ws_ref, bs_ref, qnorm_ref, wq_ref, kvnorm_ref, wk_ref, wvup_ref, wa_ref,
                      a_ref, mixed_ref, tile=tile, rope_refs=(wqsw_ref, cos_ref, sin_ref),
                      latent_refs=None, qkv_refs=(qt_ref, k_ref, vt_ref), ob_ref=None)


def _cache_kv_kernel(ckv_ref, kr_ref, wk_ref, wvup_ref, k_ref, vt_ref):
    for b in range(ckv_ref.shape[0]):
        ckv_b = ckv_ref[b].astype(BF16)
        _store_kv(_dot(ckv_b, wk_ref[...]), _dot(ckv_b, wvup_ref[...]), kr_ref[b].astype(BF16),
                  k_ref, vt_ref, slice(None), 0, batch=b)


def _mixer_weights(w):
    return [w['norm_mix'], w['w_u'], w['w_v'], w['w_lat'], w['w_ga'], w['v_norm'], w['w_s'],
            w['b_s'], w['q_norm'], w['w_q'], w['kv_norm'], w['w_k'], w['w_vup'], w['w_a']]


def _mixer_ctx(x, mods, w, *, tm):
    nb, l, _ = x.shape
    tok = lambda width: pl.BlockSpec((1, tm, width), lambda b, i: (b, i, 0))
    weights = _mixer_weights(w)
    return pl.pallas_call(
        _mixer_ctx_kernel,
        grid=(nb, l // tm),
        in_specs=[tok(D_MODEL), pl.BlockSpec((1, N_MOD, D_MODEL), lambda b, i: (b, 0, 0))]
                 + [_resident(a.shape) for a in weights],
        out_specs=[tok(D_MODEL), tok(MLA_HEADS * V_HEAD_DIM), tok(KV_LORA_RANK), tok(QK_ROPE_DIM)],
        out_shape=[jax.ShapeDtypeStruct((nb, l, D_MODEL), F32),
                   jax.ShapeDtypeStruct((nb, l, MLA_HEADS * V_HEAD_DIM), BF16),
                   jax.ShapeDtypeStruct((nb, l, KV_LORA_RANK), F32),
                   jax.ShapeDtypeStruct((nb, l, QK_ROPE_DIM), F32)],
        scratch_shapes=[pltpu.VMEM((tm // TOKEN_TILE, TOKEN_TILE, GMLP_DIM), F32)],
        compiler_params=_params(2),
        name="mixer_ctx",
    )(x, mods, *weights)


def _mixer_lat(x, mods, w, cos, sin, *, tm):
    nb, l, _ = x.shape
    tiles = tm // TOKEN_TILE
    tok = lambda width: pl.BlockSpec((1, tm, width), lambda b, i: (b, i, 0))
    tiled = lambda height: pl.BlockSpec((1, tiles, height, TOKEN_TILE), lambda b, i: (b, i, 0, 0))
    table = pl.BlockSpec((tm, ROPE_PAD), lambda b, i: (i, 0))
    weights = _mixer_weights(w) + [w['w_q_sw']]
    return pl.pallas_call(
        _mixer_lat_kernel,
        grid=(nb, l // tm),
        in_specs=[tok(D_MODEL), pl.BlockSpec((1, N_MOD, D_MODEL), lambda b, i: (b, 0, 0))]
                 + [_resident(a.shape) for a in weights] + [table, table],
        out_specs=[tok(D_MODEL), tiled(MLA_HEADS * QK_PAD), tok(MLA_HEADS * QK_PAD),
                   tiled(MLA_HEADS * V_ROWS)],
        out_shape=[jax.ShapeDtypeStruct((nb, l, D_MODEL), F32),
                   jax.ShapeDtypeStruct((nb, l // TOKEN_TILE, MLA_HEADS * QK_PAD, TOKEN_TILE), BF16),
                   jax.ShapeDtypeStruct((nb, l, MLA_HEADS * QK_PAD), BF16),
                   jax.ShapeDtypeStruct((nb, l // TOKEN_TILE, MLA_HEADS * V_ROWS, TOKEN_TILE), BF16)],
        scratch_shapes=[pltpu.VMEM((tiles, TOKEN_TILE, GMLP_DIM), F32)],
        compiler_params=_params(2),
        name="mixer_lat",
    )(x, mods, *weights, cos, sin)


def _cache_kv(ckv, kr_pad, wk, wvup):
    nb, past, _ = ckv.shape
    assert past == TOKEN_TILE, "the cached context must fill exactly one layout tile"
    return pl.pallas_call(
        _cache_kv_kernel,
        grid=(1,),
        in_specs=[pl.BlockSpec((nb, past, KV_LORA_RANK), lambda i: (0, 0, 0)),
                  pl.BlockSpec((nb, past, ROPE_PAD), lambda i: (0, 0, 0)),
                  _resident(wk.shape), _resident(wvup.shape)],
        out_specs=[pl.BlockSpec((nb, past, MLA_HEADS * QK_PAD), lambda i: (0, 0, 0)),
                   pl.BlockSpec((nb, 1, MLA_HEADS * V_ROWS, past), lambda i: (0, 0, 0, 0))],
        out_shape=[jax.ShapeDtypeStruct((nb, past, MLA_HEADS * QK_PAD), BF16),
                   jax.ShapeDtypeStruct((nb, 1, MLA_HEADS * V_ROWS, past), BF16)],
        compiler_params=_params(1),
        name="cache_kv",
    )(ckv, kr_pad, wk, wvup)


SCORES_AHEAD = 2


def _attn_kernel(qt_ref, kc_ref, vtc_ref, k_ref, vt_ref, o_ref, *, heads, tk):
    chunks = [(kc_ref, vtc_ref, 0, kc_ref.shape[1])]
    chunks += [(k_ref, vt_ref, lo, lo + tk) for lo in range(0, k_ref.shape[1], tk)]
    n_chunks = len(chunks)
    streams = [(j, t) for j in range(heads) for t in range(qt_ref.shape[1])]
    qts = [qt_ref[0, t, j * QK_PAD:(j + 1) * QK_PAD, :] for j, t in streams]

    def scores(i, c):
        j = streams[i][0]
        keys, _, lo, hi = chunks[c]
        return _dot(keys[0, lo:hi, j * QK_PAD:(j + 1) * QK_PAD], qts[i])

    def values(j, c, p):
        _, vt, lo, hi = chunks[c]
        acc = None
        for t in range(lo // TOKEN_TILE, hi // TOKEN_TILE):
            part = _dot(vt[0, t, j * V_ROWS:(j + 1) * V_ROWS, :],
                        p[t * TOKEN_TILE - lo:(t + 1) * TOKEN_TILE - lo])
            acc = part if acc is None else acc + part
        return acc

    pending = [[scores(i, c) for c in range(min(SCORES_AHEAD, n_chunks))]
               for i in range(len(streams))]
    state = [None] * len(streams)
    for c in range(n_chunks):
        for i, (j, _) in enumerate(streams):
            if c + SCORES_AHEAD < n_chunks:
                pending[i].append(scores(i, c + SCORES_AHEAD))
            s = pending[i].pop(0)
            m_c = jnp.max(s, axis=0, keepdims=True)
            if c == 0:
                p = jnp.exp2(s - m_c)
                state[i] = (m_c, values(j, c, p.astype(BF16)))
            else:
                m, acc = state[i]
                m_new = jnp.maximum(m, m_c)
                alpha = jnp.exp2(m - m_new)
                p = jnp.exp2(s - m_new)
                state[i] = (m_new, alpha * acc + values(j, c, p.astype(BF16)))
    for i, (j, t) in enumerate(streams):
        q0 = t * TOKEN_TILE
        _, acc = state[i]
        ot = acc[:V_HEAD_DIM] * (1.0 / acc[V_HEAD_DIM:V_HEAD_DIM + 1])
        o_ref[0, q0:q0 + TOKEN_TILE, j * V_HEAD_DIM:(j + 1) * V_HEAD_DIM] = ot.T.astype(BF16)


def _attention(qt, k_cache, vt_cache, k, vt, *, heads, tq, tk):
    nb, q_tiles = qt.shape[:2]
    seq_q = q_tiles * TOKEN_TILE
    past, seq_k = k_cache.shape[1], k.shape[1]
    return pl.pallas_call(
        functools.partial(_attn_kernel, heads=heads, tk=tk),
        grid=(nb, MLA_HEADS // heads, seq_q // tq),
        in_specs=[pl.BlockSpec((1, tq // TOKEN_TILE, heads * QK_PAD, TOKEN_TILE),
                               lambda b, h, i: (b, i, h, 0)),
                  pl.BlockSpec((1, past, heads * QK_PAD), lambda b, h, i: (b, 0, h)),
                  pl.BlockSpec((1, past // TOKEN_TILE, heads * V_ROWS, TOKEN_TILE),
                               lambda b, h, i: (b, 0, h, 0)),
                  pl.BlockSpec((1, seq_k, heads * QK_PAD), lambda b, h, i: (b, 0, h)),
                  pl.BlockSpec((1, seq_k // TOKEN_TILE, heads * V_ROWS, TOKEN_TILE),
                               lambda b, h, i: (b, 0, h, 0))],
        out_specs=pl.BlockSpec((1, tq, heads * V_HEAD_DIM), lambda b, h, i: (b, i, h)),
        out_shape=jax.ShapeDtypeStruct((nb, seq_q, MLA_HEADS * V_HEAD_DIM), BF16),
        compiler_params=_params(3),
        name="attention",
    )(qt, k_cache, vt_cache, k, vt)


def _merge_ffn_kernel(x_ref, a_ref, ob_ref, mods_ref, gmix_ref, wgb_ref, wb_ref, wo_ref,
                      gffn_ref, wi_ref, wo2_ref, fin_ref, o_ref, *, final_norm):
    tm = x_ref.shape[1]
    halves = [slice(0, tm // 2), slice(tm // 2, tm)]
    b_proj = [_dot(ob_ref[0, r, :], wb_ref[...]) for r in halves]
    merged_x = []
    for r, b in zip(halves, b_proj):
        x = x_ref[0, r, :]
        h = _normed_input(x, mods_ref, gmix_ref, 3).astype(BF16)
        gb = _sigmoid(_dot(h, wgb_ref[...]))
        merged = (a_ref[0, r, :] + gb * b).astype(BF16)
        merged_x.append(x + mods_ref[0, 5:6, :] * _dot(merged, wo_ref[...]))
    for r, x in zip(halves, merged_x):
        out = _swiglu_residual(x, mods_ref, gffn_ref, wi_ref, wo2_ref, 6)
        if final_norm:
            out = _rms(out, fin_ref[...])
        o_ref[0, r, :] = out


def _merge_ffn(x, a_part, out_b, mods, gmix, wgb, wb, wo, gffn, wi, wo2, fin, *, final_norm, tm):
    nb, l, _ = x.shape
    tok = pl.BlockSpec((1, tm, D_MODEL), lambda b, i: (b, i, 0))
    return pl.pallas_call(
        functools.partial(_merge_ffn_kernel, final_norm=final_norm),
        grid=(nb, l // tm),
        in_specs=[tok, tok, tok,
                  pl.BlockSpec((1, N_MOD, D_MODEL), lambda b, i: (b, 0, 0)),
                  _resident((1, D_MODEL)),
                  _resident(wgb.shape), _resident(wb.shape), _resident(wo.shape),
                  _resident((1, D_MODEL)), _resident(wi.shape), _resident(wo2.shape),
                  _resident((1, D_MODEL))],
        out_specs=tok,
        out_shape=jax.ShapeDtypeStruct(x.shape, F32),
        compiler_params=_params(2),
        name="merge_ffn",
    )(x, a_part, out_b, mods, gmix, wgb, wb, wo, gffn, wi, wo2, fin)


def _rope_tables(l):
    rows = l // GRID_W
    r = np.repeat(np.arange(rows, dtype=np.float32), GRID_W)
    col = np.tile(np.arange(GRID_W, dtype=np.float32), rows)
    half = QK_ROPE_DIM // 2
    inv = (1.0 / (np.float32(ROPE_BASE) ** (np.arange(0, half, 2, dtype=np.float32) / half))
           ).astype(np.float32)
    ang_r, ang_c = r[:, None] * inv, col[:, None] * inv
    pad = ROPE_PAD - QK_ROPE_DIM
    cos = np.concatenate([np.cos(ang_r), np.cos(ang_r), np.cos(ang_c), np.cos(ang_c),
                          np.ones((l, pad), np.float32)], axis=1)
    sin = np.concatenate([-np.sin(ang_r), np.sin(ang_r), -np.sin(ang_c), np.sin(ang_c),
                          np.zeros((l, pad), np.float32)], axis=1)
    return jnp.asarray(cos, F32), jnp.asarray(sin, F32)


def _layer_weights(l, norm_ffn1, ffn1_w_in, ffn1_w_out, norm_mix, w_in, gmlp_v_norm, gmlp_w_s,
                   gmlp_b_s, q_norm, w_q_up, kv_norm, w_kv_up, norm_ffn2):
    bf = lambda a: a.astype(BF16)
    row = lambda a: a.reshape(1, -1)
    swap = np.arange(QK_ROPE_DIM) ^ (QK_ROPE_DIM // 4)
    win_t = jnp.swapaxes(w_in[l], 0, 1)
    o_v, o_q, o_kv, o_kr, o_ga = np.cumsum(
        [GMLP_DIM, GMLP_DIM, Q_LORA_RANK, KV_LORA_RANK, QK_ROPE_DIM]).tolist()
    o_gb = o_ga + D_MODEL
    kr_w = win_t[o_kr:o_ga]
    pad_kr = jnp.zeros((ROPE_PAD - QK_ROPE_DIM, D_MODEL), F32)
    w_lat = jnp.concatenate([win_t[o_q:o_kr], kr_w, pad_kr, kr_w[swap], pad_kr], axis=0).T

    wq3 = w_q_up[l].reshape(Q_LORA_RANK, MLA_HEADS, QK_NOPE_DIM + QK_ROPE_DIM)
    pad_q = jnp.zeros((Q_LORA_RANK, MLA_HEADS, ROPE_PAD - QK_ROPE_DIM), F32)
    w_q = jnp.concatenate([wq3, pad_q], axis=2).reshape(Q_LORA_RANK, MLA_HEADS * QK_PAD)
    w_q_sw = jnp.concatenate([wq3[:, :, QK_NOPE_DIM:][:, :, swap], pad_q], axis=2).reshape(
        Q_LORA_RANK, MLA_HEADS * ROPE_PAD)
    wkv3 = w_kv_up[l].reshape(KV_LORA_RANK, MLA_HEADS, QK_NOPE_DIM + V_HEAD_DIM)
    return {
        'norm_ffn1': row(norm_ffn1[l]),
        'ffn1_i': bf(ffn1_w_in[l]),
        'ffn1_o': bf(ffn1_w_out[l]),
        'norm_mix': row(norm_mix[l]),
        'w_lat': bf(w_lat), 'w_in_t': win_t, 'w_in_starts': (0, o_v, o_ga, o_gb),
        'v_norm': row(gmlp_v_norm[l]),
        'w_s': bf(gmlp_w_s[l]),
        'b_s': jnp.repeat(gmlp_b_s[l], GMLP_GROUP_DIM, axis=1),
        'q_norm': row(q_norm[l]), 'w_q': bf(w_q), 'w_q_sw': bf(w_q_sw),
        'kv_norm': row(kv_norm[l]),
        'w_k': bf(wkv3[:, :, :QK_NOPE_DIM].reshape(KV_LORA_RANK, -1)),
        'w_vup': bf(wkv3[:, :, QK_NOPE_DIM:].reshape(KV_LORA_RANK, -1)),
        'norm_ffn2': row(norm_ffn2[l]),
    }


def kernel(x_prompt, x_sample, c, cache_ckv, cache_krope, c_ctx, mod_w, mod_b, norm_ffn1, ffn1_w_in, ffn1_w_out, norm_mix, w_in, gmlp_v_norm, gmlp_w_s, gmlp_b_s, q_norm, w_q_up, kv_norm, w_kv_up, w_a_proj, w_b_proj, w_o, norm_ffn2, ffn2_w_in, ffn2_w_out, norm_final):
    batch, seq, _ = x_prompt.shape
    dec_batch, dec_seq, _ = x_sample.shape
    depth = mod_w.shape[0]
    assert seq == TOKEN_TILE, "each mixer sub-tile of the context stream must be one sequence"
    fin = norm_final.reshape(1, D_MODEL)
    cos, sin = _rope_tables(dec_seq)
    cond = jnp.concatenate([c_ctx[None, :], c, jnp.zeros((8 - 1 - dec_batch, D_MODEL), F32)], axis=0)

    xp = x_prompt.reshape(1, batch * seq, D_MODEL)
    xs = x_sample
    ckv_list, krope_list = [], []
    for l in range(depth):
        w = _layer_weights(l, norm_ffn1, ffn1_w_in, ffn1_w_out, norm_mix, w_in, gmlp_v_norm,
                           gmlp_w_s, gmlp_b_s, q_norm, w_q_up, kv_norm, w_kv_up, norm_ffn2)
        mods = _mods(cond, mod_w[l], mod_b[l]).reshape(8, N_MOD, D_MODEL)
        mods_ctx, mods_lat = mods[0:1], mods[1:1 + dec_batch]
        last = l == depth - 1
        kr_pad = jnp.pad(cache_krope[:, l], ((0, 0), (0, 0), (0, ROPE_PAD - QK_ROPE_DIM)))

        xp, (w['w_o'],), (w['w_u'], w['w_v'], w['w_ga'], w['w_gb']) = _ffn(
            xp, mods_ctx, w['norm_ffn1'], w['ffn1_i'], w['ffn1_o'], tm=FFN_TM,
            cast=(w_o[l],), transposed=(w['w_in_t'], w['w_in_starts']))
        xs, (w['ffn2_i'], w['ffn2_o'], w['w_a'], w['w_b']), _ = _ffn(
            xs, mods_lat, w['norm_ffn1'], w['ffn1_i'], w['ffn1_o'], tm=FFN_TM,
            cast=(ffn2_w_in[l], ffn2_w_out[l], w_a_proj[l], w_b_proj[l]))

        a_ctx, ob_ctx, ckv_l, krope_l = _mixer_ctx(xp, mods_ctx, w, tm=CTX_MIXER_TM)
        a_lat, qt, k, vt = _mixer_lat(xs, mods_lat, w, cos, sin, tm=MIXER_TM)
        k_cache, vt_cache = _cache_kv(cache_ckv[:, l], kr_pad, w['w_k'], w['w_vup'])
        ob_lat = _attention(qt, k_cache, vt_cache, k, vt, heads=1, tq=ATTN_TQ, tk=ATTN_TK)

        merge = lambda x, a_part, out_b, m: _merge_ffn(
            x, a_part, out_b, m, w['norm_mix'], w['w_gb'], w['w_b'], w['w_o'],
            w['norm_ffn2'], w['ffn2_i'], w['ffn2_o'], fin, final_norm=last, tm=MERGE_TM)
        xp = merge(xp, a_ctx, ob_ctx, mods_ctx)
        xs = merge(xs, a_lat, ob_lat, mods_lat)
        ckv_list.append(ckv_l.reshape(batch, seq, KV_LORA_RANK))
        krope_list.append(krope_l.reshape(batch, seq, QK_ROPE_DIM))
    y_prompt = xp.reshape(batch, seq, D_MODEL)
    new_ckv = jnp.stack(ckv_list, axis=1)
    new_krope = jnp.stack(krope_list, axis=1)
    return (y_prompt, xs, new_ckv, new_krope)
```

```python
import functools
import math

import numpy as np
import jax
import jax.numpy as jnp
from jax import lax
from jax.experimental import pallas as pl
from jax.experimental.pallas import tpu as pltpu

D_MODEL = 1024
GRID_W = 64
FFN_DIM = 2816
GMLP_GROUPS = 8
GMLP_GROUP_DIM = 128
GMLP_DIM = GMLP_GROUPS * GMLP_GROUP_DIM
CHUNK = 128
MLA_HEADS = 8
QK_NOPE_DIM = 128
QK_ROPE_DIM = 64
V_HEAD_DIM = 128
Q_LORA_RANK = 256
KV_LORA_RANK = 256
ROPE_BASE = 10000.0
N_MOD = 9
EPS = 1e-6

V_ROWS = 144
QK_PAD = 256
ROPE_PAD = QK_PAD - QK_NOPE_DIM
V7X_VMEM_LIMIT = 56 * 1024 * 1024
FFN_TM = 1024
MERGE_TM = 512
ATTN_TQ = 2048
ATTN_TK = 512
MODS_TN = 3 * D_MODEL
Q_SCALE = (QK_NOPE_DIM + QK_ROPE_DIM) ** -0.5 * math.log2(math.e)

F32 = jnp.float32
BF16 = jnp.bfloat16


def _dot(a, b):
    return jnp.dot(a, b, preferred_element_type=F32)


def _rms(x, g):
    return x * lax.rsqrt(jnp.mean(x * x, axis=-1, keepdims=True) + EPS) * g


def _sigmoid(x):
    return 0.5 * jnp.tanh(0.5 * x) + 0.5


def _normed_input(x, mods_ref, gain_ref, base):
    shift = mods_ref[0, base:base + 1, :]
    scale = mods_ref[0, base + 1:base + 2, :]
    return _rms(x, gain_ref[...]) * (1.0 + scale) + shift


def _resident(shape):
    nd = len(shape)
    return pl.BlockSpec(shape, lambda *_: (0,) * nd, pipeline_mode=pl.Buffered(1))


def _params(n_grid):
    return pltpu.CompilerParams(dimension_semantics=("parallel",) * n_grid,
                                vmem_limit_bytes=V7X_VMEM_LIMIT)


def _mods_kernel(c_ref, w_ref, b_ref, o_ref):
    c = c_ref[...]
    s = (c * jax.nn.sigmoid(c)).astype(BF16)
    o_ref[...] = _dot(s, w_ref[...].astype(BF16)) + b_ref[...]


def _mods(cond, mod_w, mod_b):
    n = mod_w.shape[1]
    tn = MODS_TN
    return pl.pallas_call(
        _mods_kernel,
        grid=(n // tn,),
        in_specs=[pl.BlockSpec((8, D_MODEL), lambda j: (0, 0)),
                  pl.BlockSpec((D_MODEL, tn), lambda j: (0, j)),
                  pl.BlockSpec((1, tn), lambda j: (0, j))],
        out_specs=pl.BlockSpec((8, tn), lambda j: (0, j)),
        out_shape=jax.ShapeDtypeStruct((8, n), F32),
        compiler_params=_params(1),
        name="adaln_mods",
    )(cond, mod_w, mod_b.reshape(1, n))


def _swiglu_residual(x, mods_ref, gain_ref, wi_ref, wo_ref, mod_base):
    h = _normed_input(x, mods_ref, gain_ref, mod_base).astype(BF16)
    g = _dot(h, wi_ref[:, :FFN_DIM])
    u = _dot(h, wi_ref[:, FFN_DIM:])
    a = (g * jax.nn.sigmoid(g) * u).astype(BF16)
    gate = mods_ref[0, mod_base + 2:mod_base + 3, :]
    return x + 0.5 * gate * _dot(a, wo_ref[...])


def _ffn_kernel(x_ref, mods_ref, gain_ref, wi_ref, wo_ref, *refs, n_cast, t_parts):
    n_tin = sum(t_parts)
    cast_in, tr_in = refs[:n_cast], refs[n_cast:n_cast + n_tin]
    o_ref = refs[n_cast + n_tin]
    cast_out = refs[n_cast + n_tin + 1:n_cast + n_tin + 1 + n_cast]
    tr_out = refs[n_cast + n_tin + 1 + n_cast:]
    o_ref[0] = _swiglu_residual(x_ref[0], mods_ref, gain_ref, wi_ref, wo_ref, 0)
    for src, dst in zip(cast_in, cast_out):
        dst[...] = src[...].astype(BF16)
    at = 0
    for parts, dst in zip(t_parts, tr_out):
        rows = [r[...] for r in tr_in[at:at + parts]]
        at += parts
        slab = rows[0] if parts == 1 else jnp.concatenate(rows, axis=0)
        dst[...] = slab.T.astype(BF16)


HALF_SLAB = 64


def _ffn(x, mods, gain, wi, wo, *, tm, cast=(), transposed=()):
    nb, l, _ = x.shape
    steps_l = l // tm
    steps = nb * steps_l
    step = lambda b, i: b * steps_l + i
    tok = pl.BlockSpec((1, tm, D_MODEL), lambda b, i: (b, i, 0))
    slab = lambda a: pl.BlockSpec((a.shape[0] // steps, a.shape[1]), lambda b, i: (step(b, i), 0))
    for a in cast:
        assert a.shape[0] % (steps * 16) == 0, "row slabs must hold whole bf16 sublane tiles"
    tr_in, tr_specs, t_parts, tr_out_specs, tr_out_shapes = [], [], [], [], []
    per_step = 2 * HALF_SLAB
    for wt, start, n_rows in transposed:
        assert start % HALF_SLAB == 0 and n_rows % per_step == 0 and n_rows // per_step <= steps
        last = n_rows // per_step - 1
        if start % per_step == 0:
            blocks = [(per_step, start // per_step, 1)]
        else:
            blocks = [(HALF_SLAB, start // HALF_SLAB + k, 2) for k in range(2)]
        for rows, first, stride in blocks:
            tr_in.append(wt)
            tr_specs.append(pl.BlockSpec(
                (rows, wt.shape[1]),
                lambda b, i, first=first, stride=stride, last=last: (
                    first + stride * jnp.minimum(step(b, i), last), 0)))
        t_parts.append(len(blocks))
        tr_out_specs.append(pl.BlockSpec(
            (wt.shape[1], per_step), lambda b, i, last=last: (0, jnp.minimum(step(b, i), last))))
        tr_out_shapes.append(jax.ShapeDtypeStruct((wt.shape[1], n_rows), BF16))
    out = pl.pallas_call(
        functools.partial(_ffn_kernel, n_cast=len(cast), t_parts=tuple(t_parts)),
        grid=(nb, steps_l),
        in_specs=[tok,
                  pl.BlockSpec((1, N_MOD, D_MODEL), lambda b, i: (b, 0, 0)),
                  _resident((1, D_MODEL)),
                  _resident(wi.shape), _resident(wo.shape)] + [slab(a) for a in cast] + tr_specs,
        out_specs=[tok] + [slab(a) for a in cast] + tr_out_specs,
        out_shape=[jax.ShapeDtypeStruct(x.shape, F32)]
                  + [jax.ShapeDtypeStruct(a.shape, BF16) for a in cast] + tr_out_shapes,
        compiler_params=_params(2),
        name="ffn",
    )(x, mods, gain, wi, wo, *cast, *tr_in)
    return out[0], out[1:1 + len(cast)], out[1 + len(cast):]


TOKEN_TILE = 256
MIXER_TM = 2 * TOKEN_TILE
CTX_MIXER_TM = 4 * TOKEN_TILE


def _store_kv(k_nope, vals, kr_b, k_ref, vt_ref, rows, tile, batch=0):
    ones = jnp.ones((V_ROWS - V_HEAD_DIM, vals.shape[0]), BF16)
    for hd in range(MLA_HEADS):
        cols = slice(hd * V_HEAD_DIM, (hd + 1) * V_HEAD_DIM)
        k_ref[batch, rows, hd * QK_PAD:hd * QK_PAD + QK_NOPE_DIM] = k_nope[:, cols].astype(BF16)
        k_ref[batch, rows, hd * QK_PAD + QK_NOPE_DIM:(hd + 1) * QK_PAD] = kr_b
        vt_ref[batch, tile, hd * V_ROWS:hd * V_ROWS + V_HEAD_DIM, :] = vals[:, cols].T.astype(BF16)
        vt_ref[batch, tile, hd * V_ROWS + V_HEAD_DIM:(hd + 1) * V_ROWS, :] = ones


def _tile_scores(q_t, k_nope, kr_b):
    return [_dot(jnp.concatenate([k_nope[:, hd * V_HEAD_DIM:(hd + 1) * V_HEAD_DIM].astype(BF16),
                                  kr_b], axis=1), q_t[hd]) for hd in range(MLA_HEADS)]


def _tile_attend(scores, vals, ob_ref, rows):
    ones = jnp.ones((V_ROWS - V_HEAD_DIM, vals.shape[0]), BF16)
    for hd, s in enumerate(scores):
        cols = slice(hd * V_HEAD_DIM, (hd + 1) * V_HEAD_DIM)
        vt_h = jnp.concatenate([vals[:, cols].T.astype(BF16), ones], axis=0)
        p = jnp.exp2(s - jnp.max(s, axis=0, keepdims=True))
        acc = _dot(vt_h, p.astype(BF16))
        ot = acc[:V_HEAD_DIM] * (1.0 / acc[V_HEAD_DIM:V_HEAD_DIM + 1])
        ob_ref[0, rows, cols] = ot.T.astype(BF16)


def _mixer_tokens(x_ref, mods_ref, gain_ref, wu_ref, wv_ref, wlat_ref, wga_ref, vnorm_ref,
                  ws_ref, bs_ref, qnorm_ref, wq_ref, kvnorm_ref, wk_ref, wvup_ref, wa_ref,
                  a_ref, mixed_ref, *, tile, rope_refs, latent_refs, qkv_refs, ob_ref):
    rows = slice(tile * TOKEN_TILE, (tile + 1) * TOKEN_TILE)
    x = x_ref[0, rows, :]
    h = _normed_input(x, mods_ref, gain_ref, 3).astype(BF16)
    lat = _dot(h, wlat_ref[...])
    v = _dot(h, wv_ref[...])
    u = _dot(h, wu_ref[...])
    ga_logit = _dot(h, wga_ref[...])

    q_lat = lat[:, 0:Q_LORA_RANK]
    ckv = _rms(lat[:, Q_LORA_RANK:Q_LORA_RANK + KV_LORA_RANK], kvnorm_ref[...])
    kr_off = Q_LORA_RANK + KV_LORA_RANK
    kr = lat[:, kr_off:kr_off + ROPE_PAD]
    if latent_refs is not None:
        ckv_ref, krope_ref = latent_refs
        ckv_ref[0, rows, :] = ckv
        krope_ref[0, rows, :] = kr[:, 0:QK_ROPE_DIM]
    qn = _rms(q_lat, qnorm_ref[...]).astype(BF16)
    ckv_b = ckv.astype(BF16)
    q_all = _dot(qn, wq_ref[...])
    if rope_refs is not None:
        wqsw_ref, cos_ref, sin_ref = rope_refs
        q_sw = _dot(qn, wqsw_ref[...])
    k_nope = _dot(ckv_b, wk_ref[...])
    vals = _dot(ckv_b, wvup_ref[...])

    vv = _rms(v, vnorm_ref[...]).astype(BF16)
    n_chunks = TOKEN_TILE // CHUNK
    for g in range(GMLP_GROUPS):
        cols = slice(g * GMLP_GROUP_DIM, (g + 1) * GMLP_GROUP_DIM)
        blk = jnp.concatenate(
            [vv[c * CHUNK:(c + 1) * CHUNK, cols] for c in range(n_chunks)], axis=1)
        mix = _dot(ws_ref[g], blk)
        for c in range(n_chunks):
            mixed_ref[tile, c * CHUNK:(c + 1) * CHUNK, cols] = (
                mix[:, c * CHUNK:(c + 1) * CHUNK] + bs_ref[:, cols])
    out_a = (u * mixed_ref[tile]).astype(BF16)
    if qkv_refs is not None:
        a_proj = _dot(out_a, wa_ref[...])

    if rope_refs is not None:
        cos = cos_ref[rows, :]
        sin = sin_ref[rows, :]
        kr = kr * cos + lat[:, kr_off + ROPE_PAD:kr_off + 2 * ROPE_PAD] * sin
    q_t = []
    for hd in range(MLA_HEADS):
        nope = q_all[:, hd * QK_PAD:hd * QK_PAD + QK_NOPE_DIM]
        rot = q_all[:, hd * QK_PAD + QK_NOPE_DIM:(hd + 1) * QK_PAD]
        if rope_refs is not None:
            rot = rot * cos + q_sw[:, hd * ROPE_PAD:(hd + 1) * ROPE_PAD] * sin
        nope_t = (nope * Q_SCALE).T.astype(BF16)
        rot_t = (rot * Q_SCALE).T.astype(BF16)
        if qkv_refs is not None:
            qkv_refs[0][0, tile, hd * QK_PAD:hd * QK_PAD + QK_NOPE_DIM, :] = nope_t
            qkv_refs[0][0, tile, hd * QK_PAD + QK_NOPE_DIM:(hd + 1) * QK_PAD, :] = rot_t
        else:
            q_t.append(jnp.concatenate([nope_t, rot_t], axis=0))
    if qkv_refs is not None:
        _store_kv(k_nope, vals, kr.astype(BF16), qkv_refs[1], qkv_refs[2], rows, tile)
    else:
        scores = _tile_scores(q_t, k_nope, kr.astype(BF16))
        a_proj = _dot(out_a, wa_ref[...])
    a_ref[0, rows, :] = _sigmoid(ga_logit) * a_proj
    if qkv_refs is None:
        return lambda: _tile_attend(scores, vals, ob_ref, rows)


def _mixer_ctx_kernel(x_ref, mods_ref, gain_ref, wu_ref, wv_ref, wlat_ref, wga_ref, vnorm_ref,
                      ws_ref, bs_ref, qnorm_ref, wq_ref, kvnorm_ref, wk_ref, wvup_ref, wa_ref,
                      a_ref, ob_ref, ckv_ref, krope_ref, mixed_ref):
    attend = [
        _mixer_tokens(x_ref, mods_ref, gain_ref, wu_ref, wv_ref, wlat_ref, wga_ref, vnorm_ref,
                      ws_ref, bs_ref, qnorm_ref, wq_ref, kvnorm_ref, wk_ref, wvup_ref, wa_ref,
                      a_ref, mixed_ref, tile=tile, rope_refs=None,
                      latent_refs=(ckv_ref, krope_ref), qkv_refs=None, ob_ref=ob_ref)
        for tile in range(x_ref.shape[1] // TOKEN_TILE)]
    for finish in attend:
        finish()


def _mixer_lat_kernel(x_ref, mods_ref, gain_ref, wu_ref, wv_ref, wlat_ref, wga_ref, vnorm_ref,
                      ws_ref, bs_ref, qnorm_ref, wq_ref, kvnorm_ref, wk_ref, wvup_ref, wa_ref,
                      wqsw_ref, cos_ref, sin_ref, a_ref, qt_ref, k_ref, vt_ref, mixed_ref):
    for tile in range(x_ref.shape[1] // TOKEN_TILE):
        _mixer_tokens(x_ref, mods_ref, gain_ref, wu_ref, wv_ref, wlat_ref, wga_ref, vnorm_ref,
                      ws_ref, bs_ref, qnorm_ref, wq_ref, kvnorm_ref, wk_ref, wvup_ref, wa_ref,
                      a_ref, mixed_ref, tile=tile, rope_refs=(wqsw_ref, cos_ref, sin_ref),
                      latent_refs=None, qkv_refs=(qt_ref, k_ref, vt_ref), ob_ref=None)


def _cache_kv_kernel(ckv_ref, kr_ref, wk_ref, wvup_ref, k_ref, vt_ref):
    for b in range(ckv_ref.shape[0]):
        ckv_b = ckv_ref[b].astype(BF16)
        _store_kv(_dot(ckv_b, wk_ref[...]), _dot(ckv_b, wvup_ref[...]), kr_ref[b].astype(BF16),
                  k_ref, vt_ref, slice(None), 0, batch=b)


def _mixer_weights(w):
    return [w['norm_mix'], w['w_u'], w['w_v'], w['w_lat'], w['w_ga'], w['v_norm'], w['w_s'],
            w['b_s'], w['q_norm'], w['w_q'], w['kv_norm'], w['w_k'], w['w_vup'], w['w_a']]


def _mixer_ctx(x, mods, w, *, tm):
    nb, l, _ = x.shape
    tok = lambda width: pl.BlockSpec((1, tm, width), lambda b, i: (b, i, 0))
    weights = _mixer_weights(w)
    return pl.pallas_call(
        _mixer_ctx_kernel,
        grid=(nb, l // tm),
        in_specs=[tok(D_MODEL), pl.BlockSpec((1, N_MOD, D_MODEL), lambda b, i: (b, 0, 0))]
                 + [_resident(a.shape) for a in weights],
        out_specs=[tok(D_MODEL), tok(MLA_HEADS * V_HEAD_DIM), tok(KV_LORA_RANK), tok(QK_ROPE_DIM)],
        out_shape=[jax.ShapeDtypeStruct((nb, l, D_MODEL), F32),
                   jax.ShapeDtypeStruct((nb, l, MLA_HEADS * V_HEAD_DIM), BF16),
                   jax.ShapeDtypeStruct((nb, l, KV_LORA_RANK), F32),
                   jax.ShapeDtypeStruct((nb, l, QK_ROPE_DIM), F32)],
        scratch_shapes=[pltpu.VMEM((tm // TOKEN_TILE, TOKEN_TILE, GMLP_DIM), F32)],
        compiler_params=_params(2),
        name="mixer_ctx",
    )(x, mods, *weights)


def _mixer_lat(x, mods, w, cos, sin, *, tm):
    nb, l, _ = x.shape
    tiles = tm // TOKEN_TILE
    tok = lambda width: pl.BlockSpec((1, tm, width), lambda b, i: (b, i, 0))
    tiled = lambda height: pl.BlockSpec((1, tiles, height, TOKEN_TILE), lambda b, i: (b, i, 0, 0))
    table = pl.BlockSpec((tm, ROPE_PAD), lambda b, i: (i, 0))
    weights = _mixer_weights(w) + [w['w_q_sw']]
    return pl.pallas_call(
        _mixer_lat_kernel,
        grid=(nb, l // tm),
        in_specs=[tok(D_MODEL), pl.BlockSpec((1, N_MOD, D_MODEL), lambda b, i: (b, 0, 0))]
                 + [_resident(a.shape) for a in weights] + [table, table],
        out_specs=[tok(D_MODEL), tiled(MLA_HEADS * QK_PAD), tok(MLA_HEADS * QK_PAD),
                   tiled(MLA_HEADS * V_ROWS)],
        out_shape=[jax.ShapeDtypeStruct((nb, l, D_MODEL), F32),
                   jax.ShapeDtypeStruct((nb, l // TOKEN_TILE, MLA_HEADS * QK_PAD, TOKEN_TILE), BF16),
                   jax.ShapeDtypeStruct((nb, l, MLA_HEADS * QK_PAD), BF16),
                   jax.ShapeDtypeStruct((nb, l // TOKEN_TILE, MLA_HEADS * V_ROWS, TOKEN_TILE), BF16)],
        scratch_shapes=[pltpu.VMEM((tiles, TOKEN_TILE, GMLP_DIM), F32)],
        compiler_params=_params(2),
        name="mixer_lat",
    )(x, mods, *weights, cos, sin)


def _cache_kv(ckv, kr_pad, wk, wvup):
    nb, past, _ = ckv.shape
    assert past == TOKEN_TILE, "the cached context must fill exactly one layout tile"
    return pl.pallas_call(
        _cache_kv_kernel,
        grid=(1,),
        in_specs=[pl.BlockSpec((nb, past, KV_LORA_RANK), lambda i: (0, 0, 0)),
                  pl.BlockSpec((nb, past, ROPE_PAD), lambda i: (0, 0, 0)),
                  _resident(wk.shape), _resident(wvup.shape)],
        out_specs=[pl.BlockSpec((nb, past, MLA_HEADS * QK_PAD), lambda i: (0, 0, 0)),
                   pl.BlockSpec((nb, 1, MLA_HEADS * V_ROWS, past), lambda i: (0, 0, 0, 0))],
        out_shape=[jax.ShapeDtypeStruct((nb, past, MLA_HEADS * QK_PAD), BF16),
                   jax.ShapeDtypeStruct((nb, 1, MLA_HEADS * V_ROWS, past), BF16)],
        compiler_params=_params(1),
        name="cache_kv",
    )(ckv, kr_pad, wk, wvup)


SCORES_AHEAD = 2


def _attn_kernel(qt_ref, kc_ref, vtc_ref, k_ref, vt_ref, o_ref, *, heads, tk):
    chunks = [(kc_ref, vtc_ref, 0, kc_ref.shape[1])]
    chunks += [(k_ref, vt_ref, lo, lo + tk) for lo in range(0, k_ref.shape[1], tk)]
    n_chunks = len(chunks)
    streams = [(j, t) for j in range(heads) for t in range(qt_ref.shape[1])]
    qts = [qt_ref[0, t, j * QK_PAD:(j + 1) * QK_PAD, :] for j, t in streams]

    def scores(i, c):
        j = streams[i][0]
        keys, _, lo, hi = chunks[c]
        return _dot(keys[0, lo:hi, j * QK_PAD:(j + 1) * QK_PAD], qts[i])

    def values(j, c, p):
        _, vt, lo, hi = chunks[c]
        acc = None
        for t in range(lo // TOKEN_TILE, hi // TOKEN_TILE):
            part = _dot(vt[0, t, j * V_ROWS:(j + 1) * V_ROWS, :],
                        p[t * TOKEN_TILE - lo:(t + 1) * TOKEN_TILE - lo])
            acc = part if acc is None else acc + part
        return acc

    pending = [[scores(i, c) for c in range(min(SCORES_AHEAD, n_chunks))]
               for i in range(len(streams))]
    state = [None] * len(streams)
    for c in range(n_chunks):
        for i, (j, _) in enumerate(streams):
            if c + SCORES_AHEAD < n_chunks:
                pending[i].append(scores(i, c + SCORES_AHEAD))
            s = pending[i].pop(0)
            m_c = jnp.max(s, axis=0, keepdims=True)
            if c == 0:
                p = jnp.exp2(s - m_c)
                state[i] = (m_c, values(j, c, p.astype(BF16)))
            else:
                m, acc = state[i]
                m_new = jnp.maximum(m, m_c)
                alpha = jnp.exp2(m - m_new)
                p = jnp.exp2(s - m_new)
                state[i] = (m_new, alpha * acc + values(j, c, p.astype(BF16)))
    for i, (j, t) in enumerate(streams):
        q0 = t * TOKEN_TILE
        _, acc = state[i]
        ot = acc[:V_HEAD_DIM] * (1.0 / acc[V_HEAD_DIM:V_HEAD_DIM + 1])
        o_ref[0, q0:q0 + TOKEN_TILE, j * V_HEAD_DIM:(j + 1) * V_HEAD_DIM] = ot.T.astype(BF16)


def _attention(qt, k_cache, vt_cache, k, vt, *, heads, tq, tk):
    nb, q_tiles = qt.shape[:2]
    seq_q = q_tiles * TOKEN_TILE
    past, seq_k = k_cache.shape[1], k.shape[1]
    return pl.pallas_call(
        functools.partial(_attn_kernel, heads=heads, tk=tk),
        grid=(nb, MLA_HEADS // heads, seq_q // tq),
        in_specs=[pl.BlockSpec((1, tq // TOKEN_TILE, heads * QK_PAD, TOKEN_TILE),
                               lambda b, h, i: (b, i, h, 0)),
                  pl.BlockSpec((1, past, heads * QK_PAD), lambda b, h, i: (b, 0, h)),
                  pl.BlockSpec((1, past // TOKEN_TILE, heads * V_ROWS, TOKEN_TILE),
                               lambda b, h, i: (b, 0, h, 0)),
                  pl.BlockSpec((1, seq_k, heads * QK_PAD), lambda b, h, i: (b, 0, h)),
                  pl.BlockSpec((1, seq_k // TOKEN_TILE, heads * V_ROWS, TOKEN_TILE),
                               lambda b, h, i: (b, 0, h, 0))],
        out_specs=pl.BlockSpec((1, tq, heads * V_HEAD_DIM), lambda b, h, i: (b, i, h)),
        out_shape=jax.ShapeDtypeStruct((nb, seq_q, MLA_HEADS * V_HEAD_DIM), BF16),
        compiler_params=_params(3),
        name="attention",
    )(qt, k_cache, vt_cache, k, vt)


def _merge_ffn_kernel(x_ref, a_ref, ob_ref, mods_ref, gmix_ref, wgb_ref, wb_ref, wo_ref,
                      gffn_ref, wi_ref, wo2_ref, fin_ref, o_ref, *, final_norm):
    tm = x_ref.shape[1]
    halves = [slice(0, tm // 2), slice(tm // 2, tm)]
    b_proj = [_dot(ob_ref[0, r, :], wb_ref[...]) for r in halves]
    merged_x = []
    for r, b in zip(halves, b_proj):
        x = x_ref[0, r, :]
        h = _normed_input(x, mods_ref, gmix_ref, 3).astype(BF16)
        gb = _sigmoid(_dot(h, wgb_ref[...]))
        merged = (a_ref[0, r, :] + gb * b).astype(BF16)
        merged_x.append(x + mods_ref[0, 5:6, :] * _dot(merged, wo_ref[...]))
    for r, x in zip(halves, merged_x):
        out = _swiglu_residual(x, mods_ref, gffn_ref, wi_ref, wo2_ref, 6)
        if final_norm:
            out = _rms(out, fin_ref[...])
        o_ref[0, r, :] = out


def _merge_ffn(x, a_part, out_b, mods, gmix, wgb, wb, wo, gffn, wi, wo2, fin, *, final_norm, tm):
    nb, l, _ = x.shape
    tok = pl.BlockSpec((1, tm, D_MODEL), lambda b, i: (b, i, 0))
    return pl.pallas_call(
        functools.partial(_merge_ffn_kernel, final_norm=final_norm),
        grid=(nb, l // tm),
        in_specs=[tok, tok, tok,
                  pl.BlockSpec((1, N_MOD, D_MODEL), lambda b, i: (b, 0, 0)),
                  _resident((1, D_MODEL)),
                  _resident(wgb.shape), _resident(wb.shape), _resident(wo.shape),
                  _resident((1, D_MODEL)), _resident(wi.shape), _resident(wo2.shape),
                  _resident((1, D_MODEL))],
        out_specs=tok,
        out_shape=jax.ShapeDtypeStruct(x.shape, F32),
        compiler_params=_params(2),
        name="merge_ffn",
    )(x, a_part, out_b, mods, gmix, wgb, wb, wo, gffn, wi, wo2, fin)


def _rope_tables(l):
    rows = l // GRID_W
    r = np.repeat(np.arange(rows, dtype=np.float32), GRID_W)
    col = np.tile(np.arange(GRID_W, dtype=np.float32), rows)
    half = QK_ROPE_DIM // 2
    inv = (1.0 / (np.float32(ROPE_BASE) ** (np.arange(0, half, 2, dtype=np.float32) / half))
           ).astype(np.float32)
    ang_r, ang_c = r[:, None] * inv, col[:, None] * inv
    pad = ROPE_PAD - QK_ROPE_DIM
    cos = np.concatenate([np.cos(ang_r), np.cos(ang_r), np.cos(ang_c), np.cos(ang_c),
                          np.ones((l, pad), np.float32)], axis=1)
    sin = np.concatenate([-np.sin(ang_r), np.sin(ang_r), -np.sin(ang_c), np.sin(ang_c),
                          np.zeros((l, pad), np.float32)], axis=1)
    return jnp.asarray(cos, F32), jnp.asarray(sin, F32)


def _layer_weights(l, norm_ffn1, ffn1_w_in, ffn1_w_out, norm_mix, w_in, gmlp_v_norm, gmlp_w_s,
                   gmlp_b_s, q_norm, w_q_up, kv_norm, w_kv_up, norm_ffn2):
    bf = lambda a: a.astype(BF16)
    row = lambda a: a.reshape(1, -1)
    swap = np.arange(QK_ROPE_DIM) ^ (QK_ROPE_DIM // 4)
    win_t = jnp.swapaxes(w_in[l], 0, 1)
    o_v, o_q, o_kv, o_kr, o_ga = np.cumsum(
        [GMLP_DIM, GMLP_DIM, Q_LORA_RANK, KV_LORA_RANK, QK_ROPE_DIM]).tolist()
    o_gb = o_ga + D_MODEL
    kr_w = win_t[o_kr:o_ga]
    pad_kr = jnp.zeros((ROPE_PAD - QK_ROPE_DIM, D_MODEL), F32)
    w_lat_t = jnp.concatenate([win_t[o_q:o_kr], kr_w, pad_kr, kr_w[swap], pad_kr], axis=0)

    wq3 = w_q_up[l].reshape(Q_LORA_RANK, MLA_HEADS, QK_NOPE_DIM + QK_ROPE_DIM)
    pad_q = jnp.zeros((Q_LORA_RANK, MLA_HEADS, ROPE_PAD - QK_ROPE_DIM), F32)
    w_q = jnp.concatenate([wq3, pad_q], axis=2).reshape(Q_LORA_RANK, MLA_HEADS * QK_PAD)
    w_q_sw = jnp.concatenate([wq3[:, :, QK_NOPE_DIM:][:, :, swap], pad_q], axis=2).reshape(
        Q_LORA_RANK, MLA_HEADS * ROPE_PAD)
    wkv3 = w_kv_up[l].reshape(KV_LORA_RANK, MLA_HEADS, QK_NOPE_DIM + V_HEAD_DIM)
    return {
        'norm_ffn1': row(norm_ffn1[l]),
        'ffn1_i': bf(ffn1_w_in[l]),
        'ffn1_o': bf(ffn1_w_out[l]),
        'norm_mix': row(norm_mix[l]),
        'w_in_pieces': [(win_t, 0, GMLP_DIM), (win_t, o_v, GMLP_DIM), (win_t, o_ga, D_MODEL),
                        (win_t, o_gb, D_MODEL), (w_lat_t, 0, w_lat_t.shape[0])],
        'v_norm': row(gmlp_v_norm[l]),
        'w_s': bf(gmlp_w_s[l]),
        'b_s': jnp.repeat(gmlp_b_s[l], GMLP_GROUP_DIM, axis=1),
        'q_norm': row(q_norm[l]), 'w_q': bf(w_q), 'w_q_sw': bf(w_q_sw),
        'kv_norm': row(kv_norm[l]),
        'w_k': bf(wkv3[:, :, :QK_NOPE_DIM].reshape(KV_LORA_RANK, -1)),
        'w_vup': bf(wkv3[:, :, QK_NOPE_DIM:].reshape(KV_LORA_RANK, -1)),
        'norm_ffn2': row(norm_ffn2[l]),
    }


def kernel(x_prompt, x_sample, c, cache_ckv, cache_krope, c_ctx, mod_w, mod_b, norm_ffn1, ffn1_w_in, ffn1_w_out, norm_mix, w_in, gmlp_v_norm, gmlp_w_s, gmlp_b_s, q_norm, w_q_up, kv_norm, w_kv_up, w_a_proj, w_b_proj, w_o, norm_ffn2, ffn2_w_in, ffn2_w_out, norm_final):
    batch, seq, _ = x_prompt.shape
    dec_batch, dec_seq, _ = x_sample.shape
    depth = mod_w.shape[0]
    assert seq == TOKEN_TILE, "each mixer sub-tile of the context stream must be one sequence"
    fin = norm_final.reshape(1, D_MODEL)
    cos, sin = _rope_tables(dec_seq)
    cond = jnp.concatenate([c_ctx[None, :], c, jnp.zeros((8 - 1 - dec_batch, D_MODEL), F32)], axis=0)

    xp = x_prompt.reshape(1, batch * seq, D_MODEL)
    xs = x_sample
    ckv_list, krope_list = [], []
    for l in range(depth):
        w = _layer_weights(l, norm_ffn1, ffn1_w_in, ffn1_w_out, norm_mix, w_in, gmlp_v_norm,
                           gmlp_w_s, gmlp_b_s, q_norm, w_q_up, kv_norm, w_kv_up, norm_ffn2)
        mods = _mods(cond, mod_w[l], mod_b[l]).reshape(8, N_MOD, D_MODEL)
        mods_ctx, mods_lat = mods[0:1], mods[1:1 + dec_batch]
        last = l == depth - 1
        kr_pad = jnp.pad(cache_krope[:, l], ((0, 0), (0, 0), (0, ROPE_PAD - QK_ROPE_DIM)))

        xp, (w['w_o'],), (w['w_u'], w['w_v'], w['w_ga'], w['w_gb'], w['w_lat']) = _ffn(
            xp, mods_ctx, w['norm_ffn1'], w['ffn1_i'], w['ffn1_o'], tm=FFN_TM,
            cast=(w_o[l],), transposed=w['w_in_pieces'])
        xs, (w['ffn2_i'], w['ffn2_o'], w['w_a'], w['w_b']), _ = _ffn(
            xs, mods_lat, w['norm_ffn1'], w['ffn1_i'], w['ffn1_o'], tm=FFN_TM,
            cast=(ffn2_w_in[l], ffn2_w_out[l], w_a_proj[l], w_b_proj[l]))

        a_ctx, ob_ctx, ckv_l, krope_l = _mixer_ctx(xp, mods_ctx, w, tm=CTX_MIXER_TM)
        a_lat, qt, k, vt = _mixer_lat(xs, mods_lat, w, cos, sin, tm=MIXER_TM)
        k_cache, vt_cache = _cache_kv(cache_ckv[:, l], kr_pad, w['w_k'], w['w_vup'])
        ob_lat = _attention(qt, k_cache, vt_cache, k, vt, heads=1, tq=ATTN_TQ, tk=ATTN_TK)

        merge = lambda x, a_part, out_b, m: _merge_ffn(
            x, a_part, out_b, m, w['norm_mix'], w['w_gb'], w['w_b'], w['w_o'],
            w['norm_ffn2'], w['ffn2_i'], w['ffn2_o'], fin, final_norm=last, tm=MERGE_TM)
        xp = merge(xp, a_ctx, ob_ctx, mods_ctx)
        xs = merge(xs, a_lat, ob_lat, mods_lat)
        ckv_list.append(ckv_l.reshape(batch, seq, KV_LORA_RANK))
        krope_list.append(krope_l.reshape(batch, seq, QK_ROPE_DIM))
    y_prompt = xp.reshape(batch, seq, D_MODEL)
    new_ckv = jnp.stack(ckv_list, axis=1)
    new_krope = jnp.stack(krope_list, axis=1)
    return (y_prompt, xs, new_ckv, new_krope)
```
